```python
import math
import jax, jax.numpy as jnp
from jax import lax
import numpy as np

D_MODEL = 1024
BATCH = 32
SEQ = 2048
DEPTH = 1
DEC_BATCH = 16
DEC_SEQ = 64
PAST_LEN = 4096

CHUNK = 64
Q_BLOCK = 128
ROPE_THETA = 10000.0
LN_EPS = 1e-5
RMS_EPS = 1e-6

DIFF_HEADS = 4
DIFF_QK_DIM = 64
DIFF_V_DIM = 2 * DIFF_QK_DIM
DIFF_WIDTH = DIFF_HEADS * DIFF_V_DIM

MLA_HEADS = 4
MLA_NOPE = 64
MLA_ROPE = 32
MLA_V = 128
MLA_Q_LORA = 384
MLA_KV_LORA = 256
MLA_WIDTH = MLA_HEADS * MLA_V

MIX_WIDTH = DIFF_WIDTH + MLA_WIDTH

DQ_W = DIFF_HEADS * 2 * DIFF_QK_DIM
DK_W = DIFF_HEADS * 2 * DIFF_QK_DIM
DV_W = DIFF_HEADS * DIFF_V_DIM
CQ_W = MLA_Q_LORA
CKV_W = MLA_KV_LORA
KPE_W = MLA_ROPE
IN_WIDTH = DQ_W + DK_W + DV_W + CQ_W + CKV_W + KPE_W
IN_SPLITS = (DQ_W, DQ_W + DK_W, DQ_W + DK_W + DV_W, DQ_W + DK_W + DV_W + CQ_W,
             DQ_W + DK_W + DV_W + CQ_W + CKV_W)

N_EXPERTS = 64
EXPERT_DIM = 256
TOP_K = 8
N_GROUPS = 8
TOPK_GROUPS = 4
SHARED_DIM = 256
ROUTED_SCALE = 2.5
MOE_BLOCK = 512

DEEPNORM_ALPHA = (2.0 * DEPTH) ** 0.25
DEEPNORM_BETA = (8.0 * DEPTH) ** -0.25

kernel_name = 'hymba_diffattn_mla_moe_stream_step'


def rms_norm(x, g):
    xf = x.astype(jnp.float32)
    y = xf * lax.rsqrt(jnp.mean(xf * xf, axis=-1, keepdims=True) + RMS_EPS)
    return (y * g.astype(jnp.float32)).astype(x.dtype)


def layer_norm(x, g, b):
    xf = x.astype(jnp.float32)
    mu = jnp.mean(xf, axis=-1, keepdims=True)
    xc = xf - mu
    var = jnp.mean(xc * xc, axis=-1, keepdims=True)
    y = xc * lax.rsqrt(var + LN_EPS) * g.astype(jnp.float32) + b.astype(jnp.float32)
    return y.astype(x.dtype)


def rope_tables(pos, dim):
    inv = ROPE_THETA ** (-jnp.arange(0, dim, 2, dtype=jnp.float32) / dim)
    ang = pos.astype(jnp.float32)[:, None] * inv[None, :]
    return jnp.cos(ang), jnp.sin(ang)


def apply_rope(x, cos, sin):
    shape = (1, cos.shape[0]) + (1,) * (x.ndim - 3) + (cos.shape[1],)
    c = cos.reshape(shape)
    s = sin.reshape(shape)
    x1, x2 = jnp.split(x.astype(jnp.float32), 2, axis=-1)
    return jnp.concatenate([x1 * c - x2 * s, x2 * c + x1 * s], axis=-1).astype(x.dtype)


def chunk_mask(q_pos, k_pos):
    return (k_pos[None, :] // CHUNK) <= (q_pos[:, None] // CHUNK)


def masked_softmax(s, mask):
    return jax.nn.softmax(jnp.where(mask, s, -jnp.inf), axis=-1)


def diff_attention_block(q, q_pos, k, v, k_pos, lam):
    s = jnp.einsum('bqhcd,bkhcd->bhcqk', q, k).astype(jnp.float32) * (DIFF_QK_DIM ** -0.5)
    p = masked_softmax(s, chunk_mask(q_pos, k_pos))
    a = p[:, :, 0] - lam * p[:, :, 1]
    return jnp.einsum('bhqk,bkhv->bqhv', a.astype(v.dtype), v)


def mla_attention_block(q_nope, q_pe, q_pos, k_nope, k_pe, v, k_pos):
    s = (jnp.einsum('bqhd,bkhd->bhqk', q_nope, k_nope).astype(jnp.float32)
         + jnp.einsum('bqhr,bkr->bhqk', q_pe, k_pe).astype(jnp.float32)) * ((MLA_NOPE + MLA_ROPE) ** -0.5)
    p = masked_softmax(s, chunk_mask(q_pos, k_pos))
    return jnp.einsum('bhqk,bkhv->bqhv', p.astype(v.dtype), v)


def sweep_queries(attn_fn, q_parts, q_pos, kv_parts):
    b, sq = q_parts[0].shape[:2]
    if sq <= Q_BLOCK or sq % Q_BLOCK:
        return attn_fn(*q_parts, q_pos, *kv_parts)
    nb = sq // Q_BLOCK
    qb = tuple(jnp.moveaxis(a.reshape((b, nb, Q_BLOCK) + a.shape[2:]), 1, 0) for a in q_parts)
    pb = q_pos.reshape(nb, Q_BLOCK)
    out = lax.map(lambda t: attn_fn(*t[0], t[1], *kv_parts), (qb, pb))
    return jnp.moveaxis(out, 0, 1).reshape((b, sq) + out.shape[3:])


def token_mixer(x, pos, past, layer_idx, w_in, diff_lambda, diff_subln_g, mla_q_norm_g,
                mla_w_uq, mla_kv_norm_g, mla_w_ukv, w_out):
    b, s, _ = x.shape
    proj = jnp.einsum('bsd,de->bse', x, w_in)
    dq, dk, dv, cq, ckv, kpe = jnp.split(proj, IN_SPLITS, axis=-1)

    cos_d, sin_d = rope_tables(pos, DIFF_QK_DIM)
    dq = apply_rope(dq.reshape(b, s, DIFF_HEADS, 2, DIFF_QK_DIM), cos_d, sin_d)
    dk = apply_rope(dk.reshape(b, s, DIFF_HEADS, 2, DIFF_QK_DIM), cos_d, sin_d)
    dv = dv.reshape(b, s, DIFF_HEADS, DIFF_V_DIM)

    cos_m, sin_m = rope_tables(pos, MLA_ROPE)
    q = jnp.einsum('bsr,rhe->bshe', rms_norm(cq, mla_q_norm_g), mla_w_uq)
    q_nope = q[..., :MLA_NOPE]
    q_pe = apply_rope(q[..., MLA_NOPE:], cos_m, sin_m)
    c_kv = rms_norm(ckv, mla_kv_norm_g)
    k_pe = apply_rope(kpe, cos_m, sin_m)

    new_rows = (dk.reshape(b, s, DIFF_HEADS, 2 * DIFF_QK_DIM), dv, c_kv, k_pe)

    if past is None:
        k_d, v_d, ckv_all, kpe_all, k_pos = dk, dv, c_kv, k_pe, pos
    else:
        pk, pv, pc, pp = past
        p_len = pk.shape[1]
        k_d = jnp.concatenate([pk.reshape(b, p_len, DIFF_HEADS, 2, DIFF_QK_DIM).astype(dk.dtype), dk], axis=1)
        v_d = jnp.concatenate([pv.astype(dv.dtype), dv], axis=1)
        ckv_all = jnp.concatenate([pc.astype(c_kv.dtype), c_kv], axis=1)
        kpe_all = jnp.concatenate([pp.astype(k_pe.dtype), k_pe], axis=1)
        k_pos = jnp.concatenate([jnp.arange(p_len, dtype=jnp.int32), pos])

    kv = jnp.einsum('bkr,rhe->bkhe', ckv_all, mla_w_ukv)
    k_nope = kv[..., :MLA_NOPE]
    v_m = kv[..., MLA_NOPE:]

    lambda_init = 0.8 - 0.6 * math.exp(-0.3 * layer_idx)
    lp = diff_lambda.astype(jnp.float32)
    lam = jnp.exp(jnp.sum(lp[0] * lp[1])) - jnp.exp(jnp.sum(lp[2] * lp[3])) + lambda_init
    o_d = sweep_queries(diff_attention_block, (dq,), pos, (k_d, v_d, k_pos, lam))
    o_d = rms_norm(o_d, diff_subln_g) * (1.0 - lambda_init)

    o_m = sweep_queries(mla_attention_block, (q_nope, q_pe), pos, (k_nope, kpe_all, v_m, k_pos))

    mixed = jnp.concatenate([o_d.reshape(b, s, DIFF_WIDTH), o_m.reshape(b, s, MLA_WIDTH)], axis=-1)
    return jnp.einsum('bse,ed->bsd', mixed, w_out), new_rows


def route(t, w_router, b_router):
    scores = jax.nn.sigmoid(t.astype(jnp.float32) @ w_router.astype(jnp.float32))
    biased = scores + b_router.astype(jnp.float32)
    per_group = N_EXPERTS // N_GROUPS
    grp_score = lax.top_k(biased.reshape(-1, N_GROUPS, per_group), 2)[0].sum(-1)
    _, gidx = lax.top_k(grp_score, TOPK_GROUPS)
    gmask = jax.nn.one_hot(gidx, N_GROUPS, dtype=jnp.float32).sum(1)
    emask = jnp.repeat(gmask, per_group, axis=-1) > 0
    _, eidx = lax.top_k(jnp.where(emask, biased, -jnp.inf), TOP_K)
    w = jnp.take_along_axis(scores, eidx, axis=-1)
    w = w / jnp.sum(w, axis=-1, keepdims=True) * ROUTED_SCALE
    return jnp.sum(jax.nn.one_hot(eidx, N_EXPERTS, dtype=jnp.float32) * w[..., None], axis=1)


def moe_ffn(x, w_router, b_router, w_exp_gate, w_exp_up, w_exp_down, w_sh_gate, w_sh_up, w_sh_down):
    b, s, d = x.shape
    t = x.reshape(b * s, d)
    n_tok = b * s
    gate = route(t, w_router, b_router).astype(x.dtype)
    pad = (-n_tok) % MOE_BLOCK
    tb = jnp.pad(t, ((0, pad), (0, 0))).reshape(-1, MOE_BLOCK, d)
    gb = jnp.pad(gate, ((0, pad), (0, 0))).reshape(-1, MOE_BLOCK, N_EXPERTS)

    def expert_block(args):
        xb, g = args
        h = jax.nn.silu(jnp.einsum('td,edf->tef', xb, w_exp_gate)) * jnp.einsum('td,edf->tef', xb, w_exp_up)
        return jnp.einsum('tef,efd->td', h * g[:, :, None], w_exp_down)

    routed = lax.map(expert_block, (tb, gb)).reshape(-1, d)[:n_tok]
    shared = (jax.nn.silu(t @ w_sh_gate) * (t @ w_sh_up)) @ w_sh_down
    return (routed + shared).reshape(b, s, d)


def encoder_layer(x, pos, past, layer_idx, mixer_w, moe_w, norm_w):
    ln1_g, ln1_b, ln2_g, ln2_b = norm_w
    mix, new_rows = token_mixer(x, pos, past, layer_idx, *mixer_w)
    x = layer_norm(DEEPNORM_ALPHA * x + mix, ln1_g, ln1_b)
    x = layer_norm(DEEPNORM_ALPHA * x + moe_ffn(x, *moe_w), ln2_g, ln2_b)
    return x, new_rows


def setup_inputs(seed: int = 0) -> dict:
    key = jax.random.key(seed)
    ks = jax.random.split(key, 26)
    L = DEPTH

    def nrm(k, shape, scale=1.0):
        return jax.random.normal(k, shape, jnp.float32) * scale

    def gain(k, shape):
        return 1.0 + nrm(k, shape, 0.02)

    return {
        'x_prompt': nrm(ks[0], (BATCH, SEQ, D_MODEL)),
        'x_sample': nrm(ks[1], (DEC_BATCH, DEC_SEQ, D_MODEL)),
        'cache_diff_k': nrm(ks[2], (L, DEC_BATCH, PAST_LEN, DIFF_HEADS, 2 * DIFF_QK_DIM)),
        'cache_diff_v': nrm(ks[3], (L, DEC_BATCH, PAST_LEN, DIFF_HEADS, DIFF_V_DIM)),
        'cache_mla_ckv': nrm(ks[4], (L, DEC_BATCH, PAST_LEN, MLA_KV_LORA)),
        'cache_mla_kpe': nrm(ks[5], (L, DEC_BATCH, PAST_LEN, MLA_ROPE)),
        'w_in': nrm(ks[6], (L, D_MODEL, IN_WIDTH), D_MODEL ** -0.5),
        'diff_lambda': nrm(ks[7], (L, 4, DIFF_QK_DIM), 0.1),
        'diff_subln_g': gain(ks[8], (L, DIFF_V_DIM)),
        'mla_q_norm_g': gain(ks[9], (L, MLA_Q_LORA)),
        'mla_w_uq': nrm(ks[10], (L, MLA_Q_LORA, MLA_HEADS, MLA_NOPE + MLA_ROPE), MLA_Q_LORA ** -0.5),
        'mla_kv_norm_g': gain(ks[11], (L, MLA_KV_LORA)),
        'mla_w_ukv': nrm(ks[12], (L, MLA_KV_LORA, MLA_HEADS, MLA_NOPE + MLA_V), MLA_KV_LORA ** -0.5),
        'w_out': nrm(ks[13], (L, MIX_WIDTH, D_MODEL), DEEPNORM_BETA * MIX_WIDTH ** -0.5),
        'ln1_g': gain(ks[14], (L, D_MODEL)),
        'ln1_b': nrm(ks[15], (L, D_MODEL), 0.02),
        'w_router': nrm(ks[16], (L, D_MODEL, N_EXPERTS), D_MODEL ** -0.5),
        'b_router': nrm(ks[17], (L, N_EXPERTS), 0.01),
        'w_exp_gate': nrm(ks[18], (L, N_EXPERTS, D_MODEL, EXPERT_DIM), D_MODEL ** -0.5),
        'w_exp_up': nrm(ks[19], (L, N_EXPERTS, D_MODEL, EXPERT_DIM), D_MODEL ** -0.5),
        'w_exp_down': nrm(ks[20], (L, N_EXPERTS, EXPERT_DIM, D_MODEL), DEEPNORM_BETA * EXPERT_DIM ** -0.5),
        'w_sh_gate': nrm(ks[21], (L, D_MODEL, SHARED_DIM), D_MODEL ** -0.5),
        'w_sh_up': nrm(ks[22], (L, D_MODEL, SHARED_DIM), D_MODEL ** -0.5),
        'w_sh_down': nrm(ks[23], (L, SHARED_DIM, D_MODEL), DEEPNORM_BETA * SHARED_DIM ** -0.5),
        'ln2_g': gain(ks[24], (L, D_MODEL)),
        'ln2_b': nrm(ks[25], (L, D_MODEL), 0.02),
    }


def reference(x_prompt, x_sample, cache_diff_k, cache_diff_v, cache_mla_ckv, cache_mla_kpe,
              w_in, diff_lambda, diff_subln_g, mla_q_norm_g, mla_w_uq, mla_kv_norm_g, mla_w_ukv,
              w_out, ln1_g, ln1_b, w_router, b_router, w_exp_gate, w_exp_up, w_exp_down,
              w_sh_gate, w_sh_up, w_sh_down, ln2_g, ln2_b):
    past_len = cache_diff_k.shape[2]
    pos_p = jnp.arange(x_prompt.shape[1], dtype=jnp.int32)
    pos_s = past_len + jnp.arange(x_sample.shape[1], dtype=jnp.int32)
    h_p, h_s = x_prompt, x_sample
    rows_p, rows_s = [], []
    for l in range(DEPTH):
        mixer_w = (w_in[l], diff_lambda[l], diff_subln_g[l], mla_q_norm_g[l], mla_w_uq[l],
                   mla_kv_norm_g[l], mla_w_ukv[l], w_out[l])
        moe_w = (w_router[l], b_router[l], w_exp_gate[l], w_exp_up[l], w_exp_down[l],
                 w_sh_gate[l], w_sh_up[l], w_sh_down[l])
        norm_w = (ln1_g[l], ln1_b[l], ln2_g[l], ln2_b[l])
        h_p, r_p = encoder_layer(h_p, pos_p, None, l, mixer_w, moe_w, norm_w)
        past = (cache_diff_k[l], cache_diff_v[l], cache_mla_ckv[l], cache_mla_kpe[l])
        h_s, r_s = encoder_layer(h_s, pos_s, past, l, mixer_w, moe_w, norm_w)
        rows_p.append(r_p)
        rows_s.append(r_s)
    dk_p = jnp.stack([r[0] for r in rows_p], axis=0)
    dv_p = jnp.stack([r[1] for r in rows_p], axis=0)
    ckv_p = jnp.stack([r[2] for r in rows_p], axis=0)
    kpe_p = jnp.stack([r[3] for r in rows_p], axis=0)
    dk_s = jnp.stack([r[0] for r in rows_s], axis=0)
    dv_s = jnp.stack([r[1] for r in rows_s], axis=0)
    ckv_s = jnp.stack([r[2] for r in rows_s], axis=0)
    kpe_s = jnp.stack([r[3] for r in rows_s], axis=0)
    return (h_p, h_s, dk_p, dv_p, ckv_p, kpe_p, dk_s, dv_s, ckv_s, kpe_s)
```

```python
import functools
import math

import jax
import jax.numpy as jnp
from jax import lax
from jax.experimental import pallas as pl
from jax.experimental.pallas import tpu as pltpu

F32 = jnp.float32
BF16 = jnp.bfloat16

LANES = 128
VMEM_LIMIT = 52 * 1024 * 1024

CHUNK = 64
CHUNK_SHIFT = 6
ROPE_THETA = 10000.0
LN_EPS = 1e-5
RMS_EPS = 1e-6
LOG2E = 1.4426950408889634
NEG_BIG = -1e30

DIFF_HEADS = 4
DIFF_QK = 64
MLA_HEADS = 4
MLA_NOPE = 64
MLA_ROPE = 32
MLA_V = 128
Q_LORA = 384
KV_LORA = 256
N_EXPERTS = 64
N_GROUPS = 8
GROUP_SIZE = N_EXPERTS // N_GROUPS
TOPK_GROUPS = 4
TOP_K = 8
ROUTED_SCALE = 2.5
EXPERT_DIM = 256

DQ_W = DIFF_HEADS * 2 * DIFF_QK
HEAD_W = DIFF_HEADS * LANES
IN_WIDTH = 3 * DQ_W + Q_LORA + KV_LORA + MLA_ROPE
IN_PAD = 2304
OFF_DK, OFF_DV, OFF_CQ, OFF_CKV, OFF_KPE = 512, 1024, 1536, 1920, 2176


def _cparams(sem):
    return pltpu.CompilerParams(dimension_semantics=sem, vmem_limit_bytes=VMEM_LIMIT)


def _rms(x, g):
    return x * lax.rsqrt(jnp.mean(x * x, axis=-1, keepdims=True) + RMS_EPS) * g


def _layer_norm(x, g, b):
    mu = jnp.mean(x, axis=-1, keepdims=True)
    xc = x - mu
    var = jnp.mean(xc * xc, axis=-1, keepdims=True)
    return xc * lax.rsqrt(var + LN_EPS) * g + b


def _proj_kernel(x_ref, w_ref, wuq_ref, gq_ref, gkv_ref, cd_ref, sd_ref, cq_ref, sq_ref, ck_ref, sk_ref,
                 qd_ref, kd32_ref, kd16_ref, vd32_ref, vd16_ref, ckv_ref, kpe_ref, qm_ref,
                 *, scale_d, scale_m):
    tm = x_ref.shape[0]
    x = x_ref[...].astype(BF16)
    proj = jnp.dot(x, w_ref[...], preferred_element_type=F32)
    lane = lax.broadcasted_iota(jnp.int32, (tm, LANES), 1)

    first_d = (lane & 63) < 32
    cd = cd_ref[...]
    sd = sd_ref[...]

    def rope_d(blk):
        rot = jnp.where(first_d, pltpu.roll(blk, LANES - 32, 1), pltpu.roll(blk, 32, 1))
        return blk * cd + rot * sd

    for j in range(DIFF_HEADS):
        sl = slice(j * LANES, (j + 1) * LANES)
        qd_ref[:, sl] = (rope_d(proj[:, sl]) * scale_d).astype(BF16)
        kr = rope_d(proj[:, OFF_DK + j * LANES:OFF_DK + (j + 1) * LANES])
        kd32_ref[:, sl] = kr
        kd16_ref[:, sl] = kr.astype(BF16)
    dv = proj[:, OFF_DV:OFF_DV + DQ_W]
    vd32_ref[...] = dv
    vd16_ref[...] = dv.astype(BF16)

    cqn = _rms(proj[:, OFF_CQ:OFF_CQ + Q_LORA], gq_ref[...])
    q = jnp.dot(cqn.astype(BF16), wuq_ref[...], preferred_element_type=F32)
    cq = cq_ref[...]
    sq = sq_ref[...]
    first_q = lane < (MLA_NOPE + MLA_ROPE // 2)
    for h in range(MLA_HEADS):
        sl = slice(h * LANES, (h + 1) * LANES)
        blk = q[:, sl]
        rot = jnp.where(first_q, pltpu.roll(blk, LANES - 16, 1), pltpu.roll(blk, 16, 1))
        qm_ref[:, sl] = ((blk * cq + rot * sq) * scale_m).astype(BF16)

    ckv_ref[...] = _rms(proj[:, OFF_CKV:OFF_CKV + KV_LORA], gkv_ref[...])

    kb = proj[:, OFF_KPE:OFF_KPE + LANES]
    rot = jnp.where(lane < 16, pltpu.roll(kb, LANES - 16, 1), pltpu.roll(kb, 16, 1))
    kpe_ref[...] = (kb * ck_ref[...] + rot * sk_ref[...])[:, :MLA_ROPE]


def _proj(x, w_in, wuq, gq, gkv, tables, tm, n_pat):
    t = x.shape[0]
    row = lambda i: (i, 0)
    const = lambda i: (0, 0)
    pat = lambda i: (i % n_pat, 0)
    tab_spec = pl.BlockSpec((tm, LANES), pat)
    out_w = lambda w, dt: jax.ShapeDtypeStruct((t, w), dt)
    return pl.pallas_call(
        functools.partial(_proj_kernel, scale_d=LOG2E * DIFF_QK ** -0.5,
                          scale_m=LOG2E * (MLA_NOPE + MLA_ROPE) ** -0.5),
        grid=(t // tm,),
        in_specs=[pl.BlockSpec((tm, x.shape[1]), row),
                  pl.BlockSpec(w_in.shape, const), pl.BlockSpec(wuq.shape, const),
                  pl.BlockSpec(gq.shape, const), pl.BlockSpec(gkv.shape, const)] + [tab_spec] * 6,
        out_specs=[pl.BlockSpec((tm, HEAD_W), row)] * 5
        + [pl.BlockSpec((tm, KV_LORA), row), pl.BlockSpec((tm, MLA_ROPE), row), pl.BlockSpec((tm, HEAD_W), row)],
        out_shape=[out_w(HEAD_W, BF16), out_w(HEAD_W, F32), out_w(HEAD_W, BF16), out_w(HEAD_W, F32),
                   out_w(HEAD_W, BF16), out_w(KV_LORA, F32), out_w(MLA_ROPE, F32), out_w(HEAD_W, BF16)],
        compiler_params=_cparams(("parallel",)),
        name="proj",
    )(x, w_in, wuq, gq, gkv, *tables)


def _kvup_kernel(ckv_ref, kpe_ref, wuk_ref, place_ref, wuv_ref, k_ref, v_ref):
    c = ckv_ref[...].astype(BF16)
    k = jnp.dot(c, wuk_ref[...], preferred_element_type=F32)
    k = k + jnp.dot(kpe_ref[...].astype(BF16), place_ref[...], preferred_element_type=F32)
    k_ref[...] = k.astype(BF16)
    v_ref[...] = jnp.dot(c, wuv_ref[...], preferred_element_type=F32).astype(BF16)


def _kvup(ckv, kpe, wuk, place, wuv, tm):
    r = ckv.shape[0]
    row = lambda i: (i, 0)
    const = lambda i: (0, 0)
    return pl.pallas_call(
        _kvup_kernel,
        grid=(r // tm,),
        in_specs=[pl.BlockSpec((tm, KV_LORA), row), pl.BlockSpec((tm, MLA_ROPE), row),
                  pl.BlockSpec(wuk.shape, const), pl.BlockSpec(place.shape, const), pl.BlockSpec(wuv.shape, const)],
        out_specs=[pl.BlockSpec((tm, HEAD_W), row)] * 2,
        out_shape=[jax.ShapeDtypeStruct((r, HEAD_W), BF16)] * 2,
        compiler_params=_cparams(("parallel",)),
        name="kvup",
    )(ckv, kpe, wuk, place, wuv)


def _flash_kernel(lam_ref, q_ref, k_ref, v_ref, g_ref, o_ref, qs_ref, m_ref, l_ref, acc_ref,
                  *, n_comp, tq, tk, sk, q_pos0, out_scale):
    qi = pl.program_id(2)
    rows = n_comp * tq
    q = q_ref[...]
    if n_comp == 2:
        lane = lax.broadcasted_iota(jnp.int32, (tq, LANES), 1)
        zero = jnp.zeros_like(q)
        qs_ref[0:tq, :] = jnp.where(lane < DIFF_QK, q, zero)
        qs_ref[tq:rows, :] = jnp.where(lane >= DIFF_QK, q, zero)
    else:
        qs_ref[...] = q
    m_ref[...] = jnp.full(m_ref.shape, NEG_BIG, F32)
    l_ref[...] = jnp.zeros(l_ref.shape, F32)
    acc_ref[...] = jnp.zeros(acc_ref.shape, F32)

    q0 = q_pos0 + qi * tq
    lo_vis = ((q0 >> CHUNK_SHIFT) + 1) << CHUNK_SHIFT
    hi_vis = jnp.minimum((((q0 + tq - 1) >> CHUNK_SHIFT) + 1) << CHUNK_SHIFT, sk)
    n_full = jnp.minimum(lo_vis, sk) // tk
    n_tot = (hi_vis + tk - 1) // tk

    def step(j, masked):
        start = pl.multiple_of(j * tk, tk)
        k = k_ref[pl.ds(start, tk), :]
        v = v_ref[pl.ds(start, tk), :]
        s = lax.dot_general(qs_ref[...], k, (((1,), (1,)), ((), ())), preferred_element_type=F32)
        if masked:
            r = lax.broadcasted_iota(jnp.int32, (rows, tk), 0) & (tq - 1)
            c = lax.broadcasted_iota(jnp.int32, (rows, tk), 1)
            ok = ((start + c) >> CHUNK_SHIFT) <= ((q0 + r) >> CHUNK_SHIFT)
            s = jnp.where(ok, s, NEG_BIG)
        m_prev = m_ref[...]
        m_new = jnp.maximum(m_prev, jnp.max(s, axis=-1, keepdims=True))
        alpha = jnp.exp2(m_prev - m_new)
        p = jnp.exp2(s - m_new)
        l_ref[...] = alpha * l_ref[...] + jnp.sum(p, axis=-1, keepdims=True)
        acc_ref[...] = alpha * acc_ref[...] + jnp.dot(p.astype(BF16), v, preferred_element_type=F32)
        m_ref[...] = m_new

    def full_body(j, carry):
        step(j, False)
        return carry

    def masked_body(j, carry):
        step(j, True)
        return carry

    lax.fori_loop(0, n_full, full_body, 0)
    lax.fori_loop(n_full, n_tot, masked_body, 0)

    o = acc_ref[...] / l_ref[...]
    if n_comp == 2:
        o = o[0:tq, :] - lam_ref[0, 0] * o[tq:rows, :]
        o = _rms(o, g_ref[...]) * out_scale
    o_ref[...] = o.astype(BF16)


def _flash(q, k, v, lam, g, *, n_comp, tq, tk, q_pos0, out_scale):
    b, sq, _ = q.shape
    sk = k.shape[1]
    assert sq % tq == 0 and sk % tk == 0 and tq & (tq - 1) == 0
    rows = n_comp * tq
    return pl.pallas_call(
        functools.partial(_flash_kernel, n_comp=n_comp, tq=tq, tk=tk, sk=sk, q_pos0=q_pos0, out_scale=out_scale),
        grid=(b, DIFF_HEADS, sq // tq),
        in_specs=[pl.BlockSpec(memory_space=pltpu.SMEM),
                  pl.BlockSpec((None, tq, LANES), lambda bi, h, i: (bi, i, h)),
                  pl.BlockSpec((None, sk, LANES), lambda bi, h, i: (bi, 0, h)),
                  pl.BlockSpec((None, sk, LANES), lambda bi, h, i: (bi, 0, h)),
                  pl.BlockSpec((1, LANES), lambda bi, h, i: (0, 0))],
        out_specs=pl.BlockSpec((None, tq, LANES), lambda bi, h, i: (bi, i, h)),
        out_shape=jax.ShapeDtypeStruct((b, sq, HEAD_W), BF16),
        scratch_shapes=[pltpu.VMEM((rows, LANES), BF16), pltpu.VMEM((rows, 1), F32),
                        pltpu.VMEM((rows, 1), F32), pltpu.VMEM((rows, LANES), F32)],
        compiler_params=_cparams(("parallel", "parallel", "arbitrary")),
        name="flash_diff" if n_comp == 2 else "flash_mla",
    )(lam, q, k, v, g)


def _sublane_max(x):
    return jnp.max(x, axis=0, keepdims=True)


def _sublane_min(x):
    return jnp.min(x, axis=0, keepdims=True)


def _route_t(scores, biased):
    tm = scores[0].shape[1]
    sub = lax.broadcasted_iota(jnp.int32, (GROUP_SIZE, tm), 0)
    neg_inf = jnp.float32(-jnp.inf)
    gs = jnp.zeros((N_GROUPS, tm), F32)
    for g in range(N_GROUPS):
        bg = biased[g]
        m1 = _sublane_max(bg)
        i1 = _sublane_min(jnp.where(bg == m1, sub, GROUP_SIZE))
        m2 = _sublane_max(jnp.where(sub == i1, neg_inf, bg))
        gs = jnp.where(sub == g, m1 + m2, gs)
    keep = jnp.zeros((N_GROUPS, tm), jnp.bool_)
    cur = gs
    for _ in range(TOPK_GROUPS):
        mx = _sublane_max(cur)
        fi = _sublane_min(jnp.where(cur == mx, sub, N_GROUPS))
        hit = sub == fi
        keep = jnp.logical_or(keep, hit)
        cur = jnp.where(hit, neg_inf, cur)
    keep_f = jnp.where(keep, 1.0, 0.0)
    cand = []
    for g in range(N_GROUPS):
        kg = _sublane_max(jnp.where(sub == g, keep_f, 0.0)) > 0.5
        cand.append(jnp.where(kg, biased[g], neg_inf))
    chosen = [jnp.zeros((GROUP_SIZE, tm), jnp.bool_) for _ in range(N_GROUPS)]
    for _ in range(TOP_K):
        mx = cand[0]
        for g in range(1, N_GROUPS):
            mx = jnp.maximum(mx, cand[g])
        mx = _sublane_max(mx)
        fi = jnp.where(cand[0] == mx, sub, N_EXPERTS)
        for g in range(1, N_GROUPS):
            fi = jnp.minimum(fi, jnp.where(cand[g] == mx, sub + g * GROUP_SIZE, N_EXPERTS))
        fi = _sublane_min(fi)
        for g in range(N_GROUPS):
            hit = (sub + g * GROUP_SIZE) == fi
            chosen[g] = jnp.logical_or(chosen[g], hit)
            cand[g] = jnp.where(hit, neg_inf, cand[g])
    w = [jnp.where(chosen[g], scores[g], 0.0) for g in range(N_GROUPS)]
    tot = w[0]
    for g in range(1, N_GROUPS):
        tot = tot + w[g]
    tot = jnp.sum(tot, axis=0, keepdims=True)
    return [wg / tot * ROUTED_SCALE for wg in w]


def _mix_kernel(od_ref, om_ref, x_ref, wo_ref, g_ref, b_ref, wr_ref, br_ref,
                x1_ref, x1b_ref, gate_ref, gt_ref, *, alpha):
    mix = jnp.dot(od_ref[...], wo_ref[0:HEAD_W, :], preferred_element_type=F32)
    mix = mix + jnp.dot(om_ref[...], wo_ref[HEAD_W:2 * HEAD_W, :], preferred_element_type=F32)
    x1 = _layer_norm(alpha * x_ref[...] + mix, g_ref[...], b_ref[...])
    x1_ref[...] = x1
    x1b_ref[...] = x1.astype(BF16)
    logits = lax.dot_general(wr_ref[...], x1, (((1,), (1,)), ((), ())),
                             precision=lax.Precision.HIGHEST, preferred_element_type=F32)
    sc = 1.0 / (1.0 + jnp.exp(-logits))
    bi = sc + br_ref[...]
    scores = [sc[g * GROUP_SIZE:(g + 1) * GROUP_SIZE, :] for g in range(N_GROUPS)]
    biased = [bi[g * GROUP_SIZE:(g + 1) * GROUP_SIZE, :] for g in range(N_GROUPS)]
    gates = _route_t(scores, biased)
    tm = x1.shape[0]
    for g in range(N_GROUPS):
        gt_ref[g * GROUP_SIZE:(g + 1) * GROUP_SIZE, :] = gates[g]
    sub = lax.broadcasted_iota(jnp.int32, (LANES - N_EXPERTS, tm), 0)
    gt_ref[N_EXPERTS:LANES, :] = jnp.where(sub == 0, 1.0, 0.0)
    gate_ref[...] = gt_ref[...].T.astype(BF16)


def _mix(od, om, x, wo, g, b, wr_t, br, tm, alpha):
    t = x.shape[0]
    d = x.shape[1]
    row = lambda i: (i, 0)
    const = lambda i: (0, 0)
    return pl.pallas_call(
        functools.partial(_mix_kernel, alpha=alpha),
        grid=(t // tm,),
        in_specs=[pl.BlockSpec((tm, HEAD_W), row), pl.BlockSpec((tm, HEAD_W), row), pl.BlockSpec((tm, d), row),
                  pl.BlockSpec(wo.shape, const), pl.BlockSpec(g.shape, const), pl.BlockSpec(b.shape, const),
                  pl.BlockSpec(wr_t.shape, const), pl.BlockSpec(br.shape, const)],
        out_specs=[pl.BlockSpec((tm, d), row), pl.BlockSpec((tm, d), row), pl.BlockSpec((tm, LANES), row)],
        out_shape=[jax.ShapeDtypeStruct((t, d), F32), jax.ShapeDtypeStruct((t, d), BF16),
                   jax.ShapeDtypeStruct((t, LANES), BF16)],
        scratch_shapes=[pltpu.VMEM((LANES, tm), F32)],
        compiler_params=_cparams(("parallel",)),
        name="mix",
    )(od, om, x, wo, g, b, wr_t, br)


def _moe_kernel(xb_ref, x1_ref, gate_ref, rep_ref, wg_ref, wu_ref, wd_ref, g_ref, b_ref, o_ref, acc_ref, *, alpha):
    e = pl.program_id(1)

    @pl.when(e == 0)
    def _():
        acc_ref[...] = jnp.zeros(acc_ref.shape, F32)

    xb = xb_ref[...]
    hg = jnp.dot(xb, wg_ref[...], preferred_element_type=F32)
    hu = jnp.dot(xb, wu_ref[...], preferred_element_type=F32)
    gexp = jnp.dot(gate_ref[...], rep_ref[...], preferred_element_type=F32)
    h = hg * (1.0 / (1.0 + jnp.exp(-hg))) * hu * gexp
    acc_ref[...] += jnp.dot(h.astype(BF16), wd_ref[...], preferred_element_type=F32)

    @pl.when(e == pl.num_programs(1) - 1)
    def _():
        o_ref[...] = _layer_norm(alpha * x1_ref[...] + acc_ref[...], g_ref[...], b_ref[...])


def _moe(xb, x1, gate, rep, wg, wu, wd, g, b, tm, alpha):
    t, d = x1.shape
    ne = wg.shape[0]
    row = lambda i, e: (i, 0)
    const = lambda i, e: (0, 0)
    return pl.pallas_call(
        functools.partial(_moe_kernel, alpha=alpha),
        grid=(t // tm, ne),
        in_specs=[pl.BlockSpec((tm, d), row), pl.BlockSpec((tm, d), row), pl.BlockSpec((tm, LANES), row),
                  pl.BlockSpec((None, LANES, EXPERT_DIM), lambda i, e: (e, 0, 0)),
                  pl.BlockSpec((None, d, EXPERT_DIM), lambda i, e: (e, 0, 0)),
                  pl.BlockSpec((None, d, EXPERT_DIM), lambda i, e: (e, 0, 0)),
                  pl.BlockSpec((None, EXPERT_DIM, d), lambda i, e: (e, 0, 0)),
                  pl.BlockSpec(g.shape, const), pl.BlockSpec(b.shape, const)],
        out_specs=pl.BlockSpec((tm, d), row),
        out_shape=jax.ShapeDtypeStruct((t, d), F32),
        scratch_shapes=[pltpu.VMEM((tm, d), F32)],
        compiler_params=_cparams(("parallel", "arbitrary")),
        name="moe",
    )(xb, x1, gate, rep, wg, wu, wd, g, b)


def _rope_cs(pos, dim):
    inv = ROPE_THETA ** (-jnp.arange(0, dim, 2, dtype=F32) / dim)
    ang = pos.astype(F32)[:, None] * inv[None, :]
    return jnp.cos(ang), jnp.sin(ang)


def _tables(pos, reps):
    n = pos.shape[0]
    c32, s32 = _rope_cs(pos, DIFF_QK)
    c16, s16 = _rope_cs(pos, MLA_ROPE)
    one = lambda w: jnp.ones((n, w), F32)
    zero = lambda w: jnp.zeros((n, w), F32)
    cd = jnp.concatenate([c32] * 4, axis=1)
    sd = jnp.concatenate([-s32, s32] * 2, axis=1)
    cq = jnp.concatenate([one(MLA_NOPE), c16, c16, one(32)], axis=1)
    sq = jnp.concatenate([zero(MLA_NOPE), -s16, s16, zero(32)], axis=1)
    ck = jnp.concatenate([c16, c16, zero(96)], axis=1)
    sk = jnp.concatenate([-s16, s16, zero(96)], axis=1)
    return tuple(jnp.tile(a, (reps, 1)) for a in (cd, sd, cq, sq, ck, sk))


def kernel(x_prompt, x_sample, cache_diff_k, cache_diff_v, cache_mla_ckv, cache_mla_kpe, w_in, diff_lambda, diff_subln_g, mla_q_norm_g, mla_w_uq, mla_kv_norm_g, mla_w_ukv, w_out, ln1_g, ln1_b, w_router, b_router, w_exp_gate, w_exp_up, w_exp_down, w_sh_gate, w_sh_up, w_sh_down, ln2_g, ln2_b):
    depth = w_in.shape[0]
    assert depth == 1
    d_model = x_prompt.shape[-1]
    alpha = (2.0 * depth) ** 0.25
    past_len = cache_diff_k.shape[2]
    layer = 0
    lambda_init = 0.8 - 0.6 * math.exp(-0.3 * layer)

    w_in_b = jnp.pad(w_in[layer], ((0, 0), (0, IN_PAD - IN_WIDTH))).astype(BF16)
    wuq = jnp.pad(mla_w_uq[layer], ((0, 0), (0, 0), (0, LANES - MLA_NOPE - MLA_ROPE)))
    wuq = wuq.reshape(Q_LORA, HEAD_W).astype(BF16)
    wukv = mla_w_ukv[layer]
    wuk = jnp.pad(wukv[:, :, :MLA_NOPE], ((0, 0), (0, 0), (0, LANES - MLA_NOPE))).reshape(KV_LORA, HEAD_W).astype(BF16)
    wuv = wukv[:, :, MLA_NOPE:].reshape(KV_LORA, HEAD_W).astype(BF16)
    place = jnp.pad(jnp.eye(MLA_ROPE, dtype=F32), ((0, 0), (MLA_NOPE, LANES - MLA_NOPE - MLA_ROPE)))
    place = jnp.tile(place, (1, MLA_HEADS)).astype(BF16)
    gq = mla_q_norm_g[layer].reshape(1, Q_LORA)
    gkv = mla_kv_norm_g[layer].reshape(1, KV_LORA)
    gsub = diff_subln_g[layer].reshape(1, LANES)
    wo = w_out[layer].astype(BF16)
    g1, b1 = ln1_g[layer].reshape(1, d_model), ln1_b[layer].reshape(1, d_model)
    g2, b2 = ln2_g[layer].reshape(1, d_model), ln2_b[layer].reshape(1, d_model)
    wr_t = w_router[layer].T
    br = b_router[layer].reshape(N_EXPERTS, 1)
    wg_all = jnp.concatenate([w_exp_gate[layer], w_sh_gate[layer][None]], axis=0).astype(BF16)
    wu_all = jnp.concatenate([w_exp_up[layer], w_sh_up[layer][None]], axis=0).astype(BF16)
    wd_all = jnp.concatenate([w_exp_down[layer], w_sh_down[layer][None]], axis=0).astype(BF16)
    n_all = N_EXPERTS + 1
    rep = (jnp.arange(LANES)[None, :, None] == jnp.arange(n_all)[:, None, None])
    rep = jnp.broadcast_to(rep, (n_all, LANES, EXPERT_DIM)).astype(BF16)
    lp = diff_lambda[layer].astype(F32)
    lam = jnp.exp(jnp.sum(lp[0] * lp[1])) - jnp.exp(jnp.sum(lp[2] * lp[3])) + lambda_init
    lam = lam.reshape(1, 1)

    def group(x, pos, past, tm, tq, tk):
        b, s, _ = x.shape
        t = b * s
        tm = min(tm, t)
        xf = x.reshape(t, d_model)
        reps = max(1, tm // s)
        tables = _tables(pos, reps)
        n_pat = (s * reps) // tm
        qd, kd32, kd16, vd32, vd16, ckv, kpe, qm = _proj(xf, w_in_b, wuq, gq, gkv, tables, tm, n_pat)
        if past is None:
            k_d, v_d = kd16.reshape(b, s, HEAD_W), vd16.reshape(b, s, HEAD_W)
            ckv_all, kpe_all = ckv, kpe
            sk, q_pos0 = s, 0
        else:
            pk, pv, pc, pp = past
            sk = -(-(past_len + s) // tk) * tk
            padr = sk - past_len - s
            cat = lambda old, new: jnp.pad(jnp.concatenate([old, new], axis=1), ((0, 0), (0, padr), (0, 0)))
            k_d = cat(pk.reshape(b, past_len, HEAD_W).astype(BF16), kd16.reshape(b, s, HEAD_W))
            v_d = cat(pv.reshape(b, past_len, HEAD_W).astype(BF16), vd16.reshape(b, s, HEAD_W))
            ckv_all = cat(pc, ckv.reshape(b, s, KV_LORA)).reshape(b * sk, KV_LORA)
            kpe_all = cat(pp, kpe.reshape(b, s, MLA_ROPE)).reshape(b * sk, MLA_ROPE)
            q_pos0 = past_len
        k_m, v_m = _kvup(ckv_all, kpe_all, wuk, place, wuv, min(1024, ckv_all.shape[0]))
        o_d = _flash(qd.reshape(b, s, HEAD_W), k_d, v_d, lam, gsub, n_comp=2, tq=tq, tk=tk,
                     q_pos0=q_pos0, out_scale=1.0 - lambda_init)
        o_m = _flash(qm.reshape(b, s, HEAD_W), k_m.reshape(b, sk, HEAD_W), v_m.reshape(b, sk, HEAD_W), lam, gsub,
                     n_comp=1, tq=tq, tk=tk, q_pos0=q_pos0, out_scale=1.0)
        x1, x1b, gate = _mix(o_d.reshape(t, HEAD_W), o_m.reshape(t, HEAD_W), xf, wo, g1, b1, wr_t, br, tm, alpha)
        y = _moe(x1b, x1, gate, rep, wg_all, wu_all, wd_all, g2, b2, min(1024, t), alpha)
        rows = (kd32.reshape(1, b, s, DIFF_HEADS, LANES), vd32.reshape(1, b, s, DIFF_HEADS, LANES),
                ckv.reshape(1, b, s, KV_LORA), kpe.reshape(1, b, s, MLA_ROPE))
        return y.reshape(b, s, d_model), rows

    s_p = x_prompt.shape[1]
    s_s = x_sample.shape[1]
    pos_p = jnp.arange(s_p, dtype=jnp.int32)
    pos_s = past_len + jnp.arange(s_s, dtype=jnp.int32)
    y_p, r_p = group(x_prompt, pos_p, None, 512, 256, 256)
    past = (cache_diff_k[layer], cache_diff_v[layer], cache_mla_ckv[layer], cache_mla_kpe[layer])
    y_s, r_s = group(x_sample, pos_s, past, 512, s_s, 256)
    return (y_p, y_s) + r_p + r_s
```

```python
import functools
import math

import jax
import jax.numpy as jnp
from jax import lax
from jax.experimental import pallas as pl
from jax.experimental.pallas import tpu as pltpu

F32 = jnp.float32
BF16 = jnp.bfloat16

LANES = 128
VMEM_LIMIT = 52 * 1024 * 1024

CHUNK = 64
CHUNK_SHIFT = 6
ROPE_THETA = 10000.0
LN_EPS = 1e-5
RMS_EPS = 1e-6
LOG2E = 1.4426950408889634
NEG_BIG = -1e30
FLASH_SUB = 128

DIFF_HEADS = 4
DIFF_QK = 64
MLA_HEADS = 4
MLA_NOPE = 64
MLA_ROPE = 32
MLA_V = 128
Q_LORA = 384
KV_LORA = 256
N_EXPERTS = 64
N_GROUPS = 8
GROUP_SIZE = N_EXPERTS // N_GROUPS
TOPK_GROUPS = 4
TOP_K = 8
ROUTED_SCALE = 2.5
EXPERT_DIM = 256

DQ_W = DIFF_HEADS * 2 * DIFF_QK
HEAD_W = DIFF_HEADS * LANES
IN_WIDTH = 3 * DQ_W + Q_LORA + KV_LORA + MLA_ROPE
IN_PAD = 2304
OFF_DK, OFF_DV, OFF_CQ, OFF_CKV, OFF_KPE = 512, 1024, 1536, 1920, 2176


def _cparams(sem):
    return pltpu.CompilerParams(dimension_semantics=sem, vmem_limit_bytes=VMEM_LIMIT)


def _rms(x, g):
    return x * lax.rsqrt(jnp.mean(x * x, axis=-1, keepdims=True) + RMS_EPS) * g


def _layer_norm(x, g, b):
    mu = jnp.mean(x, axis=-1, keepdims=True)
    xc = x - mu
    var = jnp.mean(xc * xc, axis=-1, keepdims=True)
    return xc * lax.rsqrt(var + LN_EPS) * g + b


def _proj_kernel(x_ref, w_ref, wuq_ref, gq_ref, gkv_ref, cd_ref, sd_ref, cq_ref, sq_ref, ck_ref, sk_ref,
                 qd_ref, kd32_ref, kd16_ref, vd32_ref, vd16_ref, ckv_ref, kpe_ref, qm_ref,
                 *, scale_d, scale_m):
    tm = x_ref.shape[0]
    x = x_ref[...].astype(BF16)
    proj = jnp.dot(x, w_ref[...], preferred_element_type=F32)
    lane = lax.broadcasted_iota(jnp.int32, (tm, LANES), 1)

    first_d = (lane & 63) < 32
    cd = cd_ref[...]
    sd = sd_ref[...]

    def rope_d(blk):
        rot = jnp.where(first_d, pltpu.roll(blk, LANES - 32, 1), pltpu.roll(blk, 32, 1))
        return blk * cd + rot * sd

    for j in range(DIFF_HEADS):
        sl = slice(j * LANES, (j + 1) * LANES)
        qd_ref[:, sl] = (rope_d(proj[:, sl]) * scale_d).astype(BF16)
        kr = rope_d(proj[:, OFF_DK + j * LANES:OFF_DK + (j + 1) * LANES])
        kd32_ref[:, sl] = kr
        kd16_ref[:, sl] = kr.astype(BF16)
    dv = proj[:, OFF_DV:OFF_DV + DQ_W]
    vd32_ref[...] = dv
    vd16_ref[...] = dv.astype(BF16)

    cqn = _rms(proj[:, OFF_CQ:OFF_CQ + Q_LORA], gq_ref[...])
    q = jnp.dot(cqn.astype(BF16), wuq_ref[...], preferred_element_type=F32)
    cq = cq_ref[...]
    sq = sq_ref[...]
    first_q = lane < (MLA_NOPE + MLA_ROPE // 2)
    for h in range(MLA_HEADS):
        sl = slice(h * LANES, (h + 1) * LANES)
        blk = q[:, sl]
        rot = jnp.where(first_q, pltpu.roll(blk, LANES - 16, 1), pltpu.roll(blk, 16, 1))
        qm_ref[:, sl] = ((blk * cq + rot * sq) * scale_m).astype(BF16)

    ckv_ref[...] = _rms(proj[:, OFF_CKV:OFF_CKV + KV_LORA], gkv_ref[...])

    kb = proj[:, OFF_KPE:OFF_KPE + LANES]
    rot = jnp.where(lane < 16, pltpu.roll(kb, LANES - 16, 1), pltpu.roll(kb, 16, 1))
    kpe_ref[...] = (kb * ck_ref[...] + rot * sk_ref[...])[:, :MLA_ROPE]


def _proj(x, w_in, wuq, gq, gkv, tables, tm, n_pat):
    t = x.shape[0]
    row = lambda i: (i, 0)
    const = lambda i: (0, 0)
    pat = lambda i: (i % n_pat, 0)
    tab_spec = pl.BlockSpec((tm, LANES), pat)
    out_w = lambda w, dt: jax.ShapeDtypeStruct((t, w), dt)
    return pl.pallas_call(
        functools.partial(_proj_kernel, scale_d=LOG2E * DIFF_QK ** -0.5,
                          scale_m=LOG2E * (MLA_NOPE + MLA_ROPE) ** -0.5),
        grid=(t // tm,),
        in_specs=[pl.BlockSpec((tm, x.shape[1]), row),
                  pl.BlockSpec(w_in.shape, const), pl.BlockSpec(wuq.shape, const),
                  pl.BlockSpec(gq.shape, const), pl.BlockSpec(gkv.shape, const)] + [tab_spec] * 6,
        out_specs=[pl.BlockSpec((tm, HEAD_W), row)] * 5
        + [pl.BlockSpec((tm, KV_LORA), row), pl.BlockSpec((tm, MLA_ROPE), row), pl.BlockSpec((tm, HEAD_W), row)],
        out_shape=[out_w(HEAD_W, BF16), out_w(HEAD_W, F32), out_w(HEAD_W, BF16), out_w(HEAD_W, F32),
                   out_w(HEAD_W, BF16), out_w(KV_LORA, F32), out_w(MLA_ROPE, F32), out_w(HEAD_W, BF16)],
        compiler_params=_cparams(("parallel",)),
        name="proj",
    )(x, w_in, wuq, gq, gkv, *tables)


def _kvup_kernel(ckv_ref, kpe_ref, wuk_ref, place_ref, wuv_ref, k_ref, v_ref):
    c = ckv_ref[...].astype(BF16)
    k = jnp.dot(c, wuk_ref[...], preferred_element_type=F32)
    k = k + jnp.dot(kpe_ref[...].astype(BF16), place_ref[...], preferred_element_type=F32)
    k_ref[...] = k.astype(BF16)
    v_ref[...] = jnp.dot(c, wuv_ref[...], preferred_element_type=F32).astype(BF16)


def _kvup(ckv, kpe, wuk, place, wuv, tm):
    r = ckv.shape[0]
    row = lambda i: (i, 0)
    const = lambda i: (0, 0)
    return pl.pallas_call(
        _kvup_kernel,
        grid=(r // tm,),
        in_specs=[pl.BlockSpec((tm, KV_LORA), row), pl.BlockSpec((tm, MLA_ROPE), row),
                  pl.BlockSpec(wuk.shape, const), pl.BlockSpec(place.shape, const), pl.BlockSpec(wuv.shape, const)],
        out_specs=[pl.BlockSpec((tm, HEAD_W), row)] * 2,
        out_shape=[jax.ShapeDtypeStruct((r, HEAD_W), BF16)] * 2,
        compiler_params=_cparams(("parallel",)),
        name="kvup",
    )(ckv, kpe, wuk, place, wuv)


def _flash_kernel(lam_ref, q_ref, k_ref, v_ref, g_ref, o_ref, qs_ref, vx_ref, m_ref, acc_ref,
                  *, n_comp, tq, tk, tkm, sub, sk, q_pos0, out_scale):
    qi = pl.program_id(2)
    rows = n_comp * tq

    @pl.when(qi == 0)
    def _():
        vx_ref[:, 0:LANES] = v_ref[...]
        vx_ref[:, LANES:2 * LANES] = jnp.ones((sk, LANES), BF16)

    q = q_ref[...]
    if n_comp == 2:
        lane = lax.broadcasted_iota(jnp.int32, (tq, LANES), 1)
        zero = jnp.zeros_like(q)
        qs_ref[0:tq, :] = jnp.where(lane < DIFF_QK, q, zero)
        qs_ref[tq:rows, :] = jnp.where(lane >= DIFF_QK, q, zero)
    else:
        qs_ref[...] = q
    m_ref[...] = jnp.full(m_ref.shape, NEG_BIG, F32)
    acc_ref[...] = jnp.zeros(acc_ref.shape, F32)

    q0 = q_pos0 + qi * tq
    lo_vis = jnp.minimum(((q0 >> CHUNK_SHIFT) + 1) << CHUNK_SHIFT, sk)
    hi_vis = jnp.minimum((((q0 + tq - 1) >> CHUNK_SHIFT) + 1) << CHUNK_SHIFT, sk)
    n_full = lo_vis // tk
    m_lo = n_full * (tk // tkm)
    m_hi = (hi_vis + tkm - 1) // tkm

    def step(start, width, masked):
        k = k_ref[pl.ds(start, width), :]
        vx = vx_ref[pl.ds(start, width), :]
        for r0 in range(0, rows, sub):
            rs = slice(r0, r0 + sub)
            s = lax.dot_general(qs_ref[rs, :], k, (((1,), (1,)), ((), ())), preferred_element_type=F32)
            if masked:
                r = (lax.broadcasted_iota(jnp.int32, (sub, width), 0) + r0) & (tq - 1)
                c = lax.broadcasted_iota(jnp.int32, (sub, width), 1)
                ok = ((start + c) >> CHUNK_SHIFT) <= ((q0 + r) >> CHUNK_SHIFT)
                s = jnp.where(ok, s, NEG_BIG)
            m_prev = m_ref[rs, :]
            m_new = jnp.maximum(m_prev, jnp.max(s, axis=-1, keepdims=True))
            alpha = jnp.exp2(m_prev - m_new)
            p = jnp.exp2(s - jnp.concatenate([m_new] * (width // LANES), axis=1))
            pv = jnp.dot(p.astype(BF16), vx, preferred_element_type=F32)
            acc_ref[rs, :] = jnp.concatenate([alpha, alpha], axis=1) * acc_ref[rs, :] + pv
            m_ref[rs, :] = m_new

    def full_body(j, carry):
        step(pl.multiple_of(j * tk, tk), tk, False)
        return carry

    def masked_body(j, carry):
        step(pl.multiple_of(j * tkm, tkm), tkm, True)
        return carry

    lax.fori_loop(0, n_full, full_body, 0)
    lax.fori_loop(m_lo, m_hi, masked_body, 0)

    o = acc_ref[:, 0:LANES] / acc_ref[:, LANES:2 * LANES]
    if n_comp == 2:
        o = o[0:tq, :] - lam_ref[0, 0] * o[tq:rows, :]
        o = _rms(o, g_ref[...]) * out_scale
    o_ref[...] = o.astype(BF16)


def _flash(q, k, v, lam, g, *, n_comp, tq, tk, tkm, q_pos0, out_scale):
    b, sq, _ = q.shape
    sk = k.shape[1]
    rows = n_comp * tq
    sub = min(rows, FLASH_SUB)
    assert sq % tq == 0 and sk % tk == 0 and tk % tkm == 0 and tq & (tq - 1) == 0 and rows % sub == 0
    return pl.pallas_call(
        functools.partial(_flash_kernel, n_comp=n_comp, tq=tq, tk=tk, tkm=tkm, sub=sub, sk=sk, q_pos0=q_pos0,
                          out_scale=out_scale),
        grid=(b, DIFF_HEADS, sq // tq),
        in_specs=[pl.BlockSpec(memory_space=pltpu.SMEM),
                  pl.BlockSpec((None, tq, LANES), lambda bi, h, i: (bi, i, h)),
                  pl.BlockSpec((None, sk, LANES), lambda bi, h, i: (bi, 0, h)),
                  pl.BlockSpec((None, sk, LANES), lambda bi, h, i: (bi, 0, h)),
                  pl.BlockSpec((1, LANES), lambda bi, h, i: (0, 0))],
        out_specs=pl.BlockSpec((None, tq, LANES), lambda bi, h, i: (bi, i, h)),
        out_shape=jax.ShapeDtypeStruct((b, sq, HEAD_W), BF16),
        scratch_shapes=[pltpu.VMEM((rows, LANES), BF16), pltpu.VMEM((sk, 2 * LANES), BF16),
                        pltpu.VMEM((rows, LANES), F32), pltpu.VMEM((rows, 2 * LANES), F32)],
        compiler_params=_cparams(("parallel", "parallel", "arbitrary")),
        name="flash_diff" if n_comp == 2 else "flash_mla",
    )(lam, q, k, v, g)


def _sublane_max(x):
    return jnp.max(x, axis=0, keepdims=True)


def _sublane_min(x):
    return jnp.min(x, axis=0, keepdims=True)


def _route_t(scores, biased):
    tm = scores[0].shape[1]
    sub = lax.broadcasted_iota(jnp.int32, (GROUP_SIZE, tm), 0)
    neg_inf = jnp.float32(-jnp.inf)
    gs = jnp.zeros((N_GROUPS, tm), F32)
    for g in range(N_GROUPS):
        bg = biased[g]
        m1 = _sublane_max(bg)
        i1 = _sublane_min(jnp.where(bg == m1, sub, GROUP_SIZE))
        m2 = _sublane_max(jnp.where(sub == i1, neg_inf, bg))
        gs = jnp.where(sub == g, m1 + m2, gs)
    keep = jnp.zeros((N_GROUPS, tm), jnp.bool_)
    cur = gs
    for _ in range(TOPK_GROUPS):
        mx = _sublane_max(cur)
        fi = _sublane_min(jnp.where(cur == mx, sub, N_GROUPS))
        hit = sub == fi
        keep = jnp.logical_or(keep, hit)
        cur = jnp.where(hit, neg_inf, cur)
    keep_f = jnp.where(keep, 1.0, 0.0)
    cand = []
    for g in range(N_GROUPS):
        kg = _sublane_max(jnp.where(sub == g, keep_f, 0.0)) > 0.5
        cand.append(jnp.where(kg, biased[g], neg_inf))
    chosen = [jnp.zeros((GROUP_SIZE, tm), jnp.bool_) for _ in range(N_GROUPS)]
    for _ in range(TOP_K):
        mx = cand[0]
        for g in range(1, N_GROUPS):
            mx = jnp.maximum(mx, cand[g])
        mx = _sublane_max(mx)
        fi = jnp.where(cand[0] == mx, sub, N_EXPERTS)
        for g in range(1, N_GROUPS):
            fi = jnp.minimum(fi, jnp.where(cand[g] == mx, sub + g * GROUP_SIZE, N_EXPERTS))
        fi = _sublane_min(fi)
        for g in range(N_GROUPS):
            hit = (sub + g * GROUP_SIZE) == fi
            chosen[g] = jnp.logical_or(chosen[g], hit)
            cand[g] = jnp.where(hit, neg_inf, cand[g])
    w = [jnp.where(chosen[g], scores[g], 0.0) for g in range(N_GROUPS)]
    tot = w[0]
    for g in range(1, N_GROUPS):
        tot = tot + w[g]
    tot = jnp.sum(tot, axis=0, keepdims=True)
    return [wg / tot * ROUTED_SCALE for wg in w]


def _mix_kernel(od_ref, om_ref, x_ref, wo_ref, g_ref, b_ref, wr_ref, br_ref,
                x1_ref, x1b_ref, gate_ref, gt_ref, *, alpha):
    mix = jnp.dot(od_ref[...], wo_ref[0:HEAD_W, :], preferred_element_type=F32)
    mix = mix + jnp.dot(om_ref[...], wo_ref[HEAD_W:2 * HEAD_W, :], preferred_element_type=F32)
    x1 = _layer_norm(alpha * x_ref[...] + mix, g_ref[...], b_ref[...])
    x1_ref[...] = x1
    x1b_ref[...] = x1.astype(BF16)
    logits = lax.dot_general(wr_ref[...], x1, (((1,), (1,)), ((), ())),
                             precision=lax.Precision.HIGHEST, preferred_element_type=F32)
    sc = 1.0 / (1.0 + jnp.exp(-logits))
    bi = sc + br_ref[...]
    scores = [sc[g * GROUP_SIZE:(g + 1) * GROUP_SIZE, :] for g in range(N_GROUPS)]
    biased = [bi[g * GROUP_SIZE:(g + 1) * GROUP_SIZE, :] for g in range(N_GROUPS)]
    gates = _route_t(scores, biased)
    tm = x1.shape[0]
    for g in range(N_GROUPS):
        gt_ref[g * GROUP_SIZE:(g + 1) * GROUP_SIZE, :] = gates[g]
    sub = lax.broadcasted_iota(jnp.int32, (LANES - N_EXPERTS, tm), 0)
    gt_ref[N_EXPERTS:LANES, :] = jnp.where(sub == 0, 1.0, 0.0)
    gate_ref[...] = gt_ref[...].T.astype(BF16)


def _mix(od, om, x, wo, g, b, wr_t, br, tm, alpha):
    t = x.shape[0]
    d = x.shape[1]
    row = lambda i: (i, 0)
    const = lambda i: (0, 0)
    return pl.pallas_call(
        functools.partial(_mix_kernel, alpha=alpha),
        grid=(t // tm,),
        in_specs=[pl.BlockSpec((tm, HEAD_W), row), pl.BlockSpec((tm, HEAD_W), row), pl.BlockSpec((tm, d), row),
                  pl.BlockSpec(wo.shape, const), pl.BlockSpec(g.shape, const), pl.BlockSpec(b.shape, const),
                  pl.BlockSpec(wr_t.shape, const), pl.BlockSpec(br.shape, const)],
        out_specs=[pl.BlockSpec((tm, d), row), pl.BlockSpec((tm, d), row), pl.BlockSpec((tm, LANES), row)],
        out_shape=[jax.ShapeDtypeStruct((t, d), F32), jax.ShapeDtypeStruct((t, d), BF16),
                   jax.ShapeDtypeStruct((t, LANES), BF16)],
        scratch_shapes=[pltpu.VMEM((LANES, tm), F32)],
        compiler_params=_cparams(("parallel",)),
        name="mix",
    )(od, om, x, wo, g, b, wr_t, br)


def _moe_kernel(xb_ref, x1_ref, gate_ref, rep_ref, wg_ref, wu_ref, wd_ref, g_ref, b_ref, o_ref, acc_ref, *, alpha):
    e = pl.program_id(1)

    @pl.when(e == 0)
    def _():
        acc_ref[...] = jnp.zeros(acc_ref.shape, F32)

    xb = xb_ref[...]
    hg = jnp.dot(xb, wg_ref[...], preferred_element_type=F32)
    hu = jnp.dot(xb, wu_ref[...], preferred_element_type=F32)
    gexp = jnp.dot(gate_ref[...], rep_ref[...], preferred_element_type=F32)
    h = hg * (1.0 / (1.0 + jnp.exp(-hg))) * hu * gexp
    acc_ref[...] += jnp.dot(h.astype(BF16), wd_ref[...], preferred_element_type=F32)

    @pl.when(e == pl.num_programs(1) - 1)
    def _():
        o_ref[...] = _layer_norm(alpha * x1_ref[...] + acc_ref[...], g_ref[...], b_ref[...])


def _moe(xb, x1, gate, rep, wg, wu, wd, g, b, tm, alpha):
    t, d = x1.shape
    ne = wg.shape[0]
    row = lambda i, e: (i, 0)
    const = lambda i, e: (0, 0)
    return pl.pallas_call(
        functools.partial(_moe_kernel, alpha=alpha),
        grid=(t // tm, ne),
        in_specs=[pl.BlockSpec((tm, d), row), pl.BlockSpec((tm, d), row), pl.BlockSpec((tm, LANES), row),
                  pl.BlockSpec((None, LANES, EXPERT_DIM), lambda i, e: (e, 0, 0)),
                  pl.BlockSpec((None, d, EXPERT_DIM), lambda i, e: (e, 0, 0)),
                  pl.BlockSpec((None, d, EXPERT_DIM), lambda i, e: (e, 0, 0)),
                  pl.BlockSpec((None, EXPERT_DIM, d), lambda i, e: (e, 0, 0)),
                  pl.BlockSpec(g.shape, const), pl.BlockSpec(b.shape, const)],
        out_specs=pl.BlockSpec((tm, d), row),
        out_shape=jax.ShapeDtypeStruct((t, d), F32),
        scratch_shapes=[pltpu.VMEM((tm, d), F32)],
        compiler_params=_cparams(("parallel", "arbitrary")),
        name="moe",
    )(xb, x1, gate, rep, wg, wu, wd, g, b)


def _rope_cs(pos, dim):
    inv = ROPE_THETA ** (-jnp.arange(0, dim, 2, dtype=F32) / dim)
    ang = pos.astype(F32)[:, None] * inv[None, :]
    return jnp.cos(ang), jnp.sin(ang)


def _tables(pos, reps):
    n = pos.shape[0]
    c32, s32 = _rope_cs(pos, DIFF_QK)
    c16, s16 = _rope_cs(pos, MLA_ROPE)
    one = lambda w: jnp.ones((n, w), F32)
    zero = lambda w: jnp.zeros((n, w), F32)
    cd = jnp.concatenate([c32] * 4, axis=1)
    sd = jnp.concatenate([-s32, s32] * 2, axis=1)
    cq = jnp.concatenate([one(MLA_NOPE), c16, c16, one(32)], axis=1)
    sq = jnp.concatenate([zero(MLA_NOPE), -s16, s16, zero(32)], axis=1)
    ck = jnp.concatenate([c16, c16, zero(96)], axis=1)
    sk = jnp.concatenate([-s16, s16, zero(96)], axis=1)
    return tuple(jnp.tile(a, (reps, 1)) for a in (cd, sd, cq, sq, ck, sk))


def kernel(x_prompt, x_sample, cache_diff_k, cache_diff_v, cache_mla_ckv, cache_mla_kpe, w_in, diff_lambda, diff_subln_g, mla_q_norm_g, mla_w_uq, mla_kv_norm_g, mla_w_ukv, w_out, ln1_g, ln1_b, w_router, b_router, w_exp_gate, w_exp_up, w_exp_down, w_sh_gate, w_sh_up, w_sh_down, ln2_g, ln2_b):
    depth = w_in.shape[0]
    assert depth == 1
    d_model = x_prompt.shape[-1]
    alpha = (2.0 * depth) ** 0.25
    past_len = cache_diff_k.shape[2]
    layer = 0
    lambda_init = 0.8 - 0.6 * math.exp(-0.3 * layer)

    w_in_b = jnp.pad(w_in[layer], ((0, 0), (0, IN_PAD - IN_WIDTH))).astype(BF16)
    wuq = jnp.pad(mla_w_uq[layer], ((0, 0), (0, 0), (0, LANES - MLA_NOPE - MLA_ROPE)))
    wuq = wuq.reshape(Q_LORA, HEAD_W).astype(BF16)
    wukv = mla_w_ukv[layer]
    wuk = jnp.pad(wukv[:, :, :MLA_NOPE], ((0, 0), (0, 0), (0, LANES - MLA_NOPE))).reshape(KV_LORA, HEAD_W).astype(BF16)
    wuv = wukv[:, :, MLA_NOPE:].reshape(KV_LORA, HEAD_W).astype(BF16)
    place = jnp.pad(jnp.eye(MLA_ROPE, dtype=F32), ((0, 0), (MLA_NOPE, LANES - MLA_NOPE - MLA_ROPE)))
    place = jnp.tile(place, (1, MLA_HEADS)).astype(BF16)
    gq = mla_q_norm_g[layer].reshape(1, Q_LORA)
    gkv = mla_kv_norm_g[layer].reshape(1, KV_LORA)
    gsub = diff_subln_g[layer].reshape(1, LANES)
    wo = w_out[layer].astype(BF16)
    g1, b1 = ln1_g[layer].reshape(1, d_model), ln1_b[layer].reshape(1, d_model)
    g2, b2 = ln2_g[layer].reshape(1, d_model), ln2_b[layer].reshape(1, d_model)
    wr_t = w_router[layer].T
    br = b_router[layer].reshape(N_EXPERTS, 1)
    wg_all = jnp.concatenate([w_exp_gate[layer], w_sh_gate[layer][None]], axis=0).astype(BF16)
    wu_all = jnp.concatenate([w_exp_up[layer], w_sh_up[layer][None]], axis=0).astype(BF16)
    wd_all = jnp.concatenate([w_exp_down[layer], w_sh_down[layer][None]], axis=0).astype(BF16)
    n_all = N_EXPERTS + 1
    rep = (jnp.arange(LANES)[None, :, None] == jnp.arange(n_all)[:, None, None])
    rep = jnp.broadcast_to(rep, (n_all, LANES, EXPERT_DIM)).astype(BF16)
    lp = diff_lambda[layer].astype(F32)
    lam = jnp.exp(jnp.sum(lp[0] * lp[1])) - jnp.exp(jnp.sum(lp[2] * lp[3])) + lambda_init
    lam = lam.reshape(1, 1)

    def group(x, pos, past, tm, tq, tk, tkm):
        b, s, _ = x.shape
        t = b * s
        tm = min(tm, t)
        xf = x.reshape(t, d_model)
        reps = max(1, tm // s)
        tables = _tables(pos, reps)
        n_pat = (s * reps) // tm
        qd, kd32, kd16, vd32, vd16, ckv, kpe, qm = _proj(xf, w_in_b, wuq, gq, gkv, tables, tm, n_pat)
        if past is None:
            k_d, v_d = kd16.reshape(b, s, HEAD_W), vd16.reshape(b, s, HEAD_W)
            ckv_all, kpe_all = ckv, kpe
            sk, q_pos0 = s, 0
        else:
            pk, pv, pc, pp = past
            sk = -(-(past_len + s) // tk) * tk
            padr = sk - past_len - s
            cat = lambda old, new: jnp.pad(jnp.concatenate([old, new], axis=1), ((0, 0), (0, padr), (0, 0)))
            k_d = cat(pk.reshape(b, past_len, HEAD_W).astype(BF16), kd16.reshape(b, s, HEAD_W))
            v_d = cat(pv.reshape(b, past_len, HEAD_W).astype(BF16), vd16.reshape(b, s, HEAD_W))
            ckv_all = cat(pc, ckv.reshape(b, s, KV_LORA)).reshape(b * sk, KV_LORA)
            kpe_all = cat(pp, kpe.reshape(b, s, MLA_ROPE)).reshape(b * sk, MLA_ROPE)
            q_pos0 = past_len
        k_m, v_m = _kvup(ckv_all, kpe_all, wuk, place, wuv, min(1024, ckv_all.shape[0]))
        o_d = _flash(qd.reshape(b, s, HEAD_W), k_d, v_d, lam, gsub, n_comp=2, tq=tq, tk=tk, tkm=tkm,
                     q_pos0=q_pos0, out_scale=1.0 - lambda_init)
        o_m = _flash(qm.reshape(b, s, HEAD_W), k_m.reshape(b, sk, HEAD_W), v_m.reshape(b, sk, HEAD_W), lam, gsub,
                     n_comp=1, tq=tq, tk=tk, tkm=tkm, q_pos0=q_pos0, out_scale=1.0)
        x1, x1b, gate = _mix(o_d.reshape(t, HEAD_W), o_m.reshape(t, HEAD_W), xf, wo, g1, b1, wr_t, br, tm, alpha)
        y = _moe(x1b, x1, gate, rep, wg_all, wu_all, wd_all, g2, b2, min(1024, t), alpha)
        rows = (kd32.reshape(1, b, s, DIFF_HEADS, LANES), vd32.reshape(1, b, s, DIFF_HEADS, LANES),
                ckv.reshape(1, b, s, KV_LORA), kpe.reshape(1, b, s, MLA_ROPE))
        return y.reshape(b, s, d_model), rows

    s_p = x_prompt.shape[1]
    s_s = x_sample.shape[1]
    pos_p = jnp.arange(s_p, dtype=jnp.int32)
    pos_s = past_len + jnp.arange(s_s, dtype=jnp.int32)
    y_p, r_p = group(x_prompt, pos_p, None, 512, 256, 512, 256)
    past = (cache_diff_k[layer], cache_diff_v[layer], cache_mla_ckv[layer], cache_mla_kpe[layer])
    y_s, r_s = group(x_sample, pos_s, past, 512, s_s, 512, 256)
    return (y_p, y_s) + r_p + r_s
```

```python
import functools
import math

import jax
import jax.numpy as jnp
from jax import lax
from jax.experimental import pallas as pl
from jax.experimental.pallas import tpu as pltpu

F32 = jnp.float32
BF16 = jnp.bfloat16

LANES = 128
VMEM_LIMIT = 52 * 1024 * 1024

CHUNK = 64
CHUNK_SHIFT = 6
ROPE_THETA = 10000.0
LN_EPS = 1e-5
RMS_EPS = 1e-6
LOG2E = 1.4426950408889634
NEG_BIG = -1e30
FLASH_SUB = 128

DIFF_HEADS = 4
DIFF_QK = 64
MLA_HEADS = 4
MLA_NOPE = 64
MLA_ROPE = 32
MLA_V = 128
Q_LORA = 384
KV_LORA = 256
N_EXPERTS = 64
N_GROUPS = 8
GROUP_SIZE = N_EXPERTS // N_GROUPS
TOPK_GROUPS = 4
TOP_K = 8
ROUTED_SCALE = 2.5
EXPERT_DIM = 256

MOE_BLOCK = 256
SEG_ALIGN = 16
DISP_SUB = 256
BLOCK_ROWS = -(-(TOP_K * MOE_BLOCK + N_EXPERTS * (SEG_ALIGN - 1)) // DISP_SUB) * DISP_SUB
FFN_TILE = 512

DQ_W = DIFF_HEADS * 2 * DIFF_QK
HEAD_W = DIFF_HEADS * LANES
IN_WIDTH = 3 * DQ_W + Q_LORA + KV_LORA + MLA_ROPE
IN_PAD = 2304
OFF_DK, OFF_DV, OFF_CQ, OFF_CKV, OFF_KPE = 512, 1024, 1536, 1920, 2176


def _cparams(sem):
    return pltpu.CompilerParams(dimension_semantics=sem, vmem_limit_bytes=VMEM_LIMIT)


def _rms(x, g):
    return x * lax.rsqrt(jnp.mean(x * x, axis=-1, keepdims=True) + RMS_EPS) * g


def _layer_norm(x, g, b):
    mu = jnp.mean(x, axis=-1, keepdims=True)
    xc = x - mu
    var = jnp.mean(xc * xc, axis=-1, keepdims=True)
    return xc * lax.rsqrt(var + LN_EPS) * g + b


def _proj_kernel(x_ref, w_ref, wuq_ref, gq_ref, gkv_ref, cd_ref, sd_ref, cq_ref, sq_ref, ck_ref, sk_ref,
                 qd_ref, kd32_ref, kd16_ref, vd32_ref, vd16_ref, ckv_ref, kpe_ref, qm_ref,
                 *, scale_d, scale_m):
    tm = x_ref.shape[0]
    x = x_ref[...].astype(BF16)
    proj = jnp.dot(x, w_ref[...], preferred_element_type=F32)
    lane = lax.broadcasted_iota(jnp.int32, (tm, LANES), 1)

    first_d = (lane & 63) < 32
    cd = cd_ref[...]
    sd = sd_ref[...]

    def rope_d(blk):
        rot = jnp.where(first_d, pltpu.roll(blk, LANES - 32, 1), pltpu.roll(blk, 32, 1))
        return blk * cd + rot * sd

    for j in range(DIFF_HEADS):
        sl = slice(j * LANES, (j + 1) * LANES)
        qd_ref[:, sl] = (rope_d(proj[:, sl]) * scale_d).astype(BF16)
        kr = rope_d(proj[:, OFF_DK + j * LANES:OFF_DK + (j + 1) * LANES])
        kd32_ref[:, sl] = kr
        kd16_ref[:, sl] = kr.astype(BF16)
    dv = proj[:, OFF_DV:OFF_DV + DQ_W]
    vd32_ref[...] = dv
    vd16_ref[...] = dv.astype(BF16)

    cqn = _rms(proj[:, OFF_CQ:OFF_CQ + Q_LORA], gq_ref[...])
    q = jnp.dot(cqn.astype(BF16), wuq_ref[...], preferred_element_type=F32)
    cq = cq_ref[...]
    sq = sq_ref[...]
    first_q = lane < (MLA_NOPE + MLA_ROPE // 2)
    for h in range(MLA_HEADS):
        sl = slice(h * LANES, (h + 1) * LANES)
        blk = q[:, sl]
        rot = jnp.where(first_q, pltpu.roll(blk, LANES - 16, 1), pltpu.roll(blk, 16, 1))
        qm_ref[:, sl] = ((blk * cq + rot * sq) * scale_m).astype(BF16)

    ckv_ref[...] = _rms(proj[:, OFF_CKV:OFF_CKV + KV_LORA], gkv_ref[...])

    kb = proj[:, OFF_KPE:OFF_KPE + LANES]
    rot = jnp.where(lane < 16, pltpu.roll(kb, LANES - 16, 1), pltpu.roll(kb, 16, 1))
    kpe_ref[...] = (kb * ck_ref[...] + rot * sk_ref[...])[:, :MLA_ROPE]


def _proj(x, w_in, wuq, gq, gkv, tables, tm, n_pat):
    t = x.shape[0]
    row = lambda i: (i, 0)
    const = lambda i: (0, 0)
    pat = lambda i: (i % n_pat, 0)
    tab_spec = pl.BlockSpec((tm, LANES), pat)
    out_w = lambda w, dt: jax.ShapeDtypeStruct((t, w), dt)
    return pl.pallas_call(
        functools.partial(_proj_kernel, scale_d=LOG2E * DIFF_QK ** -0.5,
                          scale_m=LOG2E * (MLA_NOPE + MLA_ROPE) ** -0.5),
        grid=(t // tm,),
        in_specs=[pl.BlockSpec((tm, x.shape[1]), row),
                  pl.BlockSpec(w_in.shape, const), pl.BlockSpec(wuq.shape, const),
                  pl.BlockSpec(gq.shape, const), pl.BlockSpec(gkv.shape, const)] + [tab_spec] * 6,
        out_specs=[pl.BlockSpec((tm, HEAD_W), row)] * 5
        + [pl.BlockSpec((tm, KV_LORA), row), pl.BlockSpec((tm, MLA_ROPE), row), pl.BlockSpec((tm, HEAD_W), row)],
        out_shape=[out_w(HEAD_W, BF16), out_w(HEAD_W, F32), out_w(HEAD_W, BF16), out_w(HEAD_W, F32),
                   out_w(HEAD_W, BF16), out_w(KV_LORA, F32), out_w(MLA_ROPE, F32), out_w(HEAD_W, BF16)],
        compiler_params=_cparams(("parallel",)),
        name="proj",
    )(x, w_in, wuq, gq, gkv, *tables)


def _kvup_kernel(ckv_ref, kpe_ref, wuk_ref, place_ref, wuv_ref, k_ref, v_ref):
    c = ckv_ref[...].astype(BF16)
    k = jnp.dot(c, wuk_ref[...], preferred_element_type=F32)
    k = k + jnp.dot(kpe_ref[...].astype(BF16), place_ref[...], preferred_element_type=F32)
    k_ref[...] = k.astype(BF16)
    v_ref[...] = jnp.dot(c, wuv_ref[...], preferred_element_type=F32).astype(BF16)


def _kvup(ckv, kpe, wuk, place, wuv, tm):
    r = ckv.shape[0]
    row = lambda i: (i, 0)
    const = lambda i: (0, 0)
    return pl.pallas_call(
        _kvup_kernel,
        grid=(r // tm,),
        in_specs=[pl.BlockSpec((tm, KV_LORA), row), pl.BlockSpec((tm, MLA_ROPE), row),
                  pl.BlockSpec(wuk.shape, const), pl.BlockSpec(place.shape, const), pl.BlockSpec(wuv.shape, const)],
        out_specs=[pl.BlockSpec((tm, HEAD_W), row)] * 2,
        out_shape=[jax.ShapeDtypeStruct((r, HEAD_W), BF16)] * 2,
        compiler_params=_cparams(("parallel",)),
        name="kvup",
    )(ckv, kpe, wuk, place, wuv)


def _flash_kernel(lam_ref, q_ref, k_ref, v_ref, g_ref, o_ref, qs_ref, vx_ref, m_ref, acc_ref,
                  *, n_comp, tq, tk, tkm, sub, sk, q_pos0, out_scale):
    qi = pl.program_id(2)
    rows = n_comp * tq

    @pl.when(qi == 0)
    def _():
        vx_ref[:, 0:LANES] = v_ref[...]
        vx_ref[:, LANES:2 * LANES] = jnp.ones((sk, LANES), BF16)

    q = q_ref[...]
    if n_comp == 2:
        lane = lax.broadcasted_iota(jnp.int32, (tq, LANES), 1)
        zero = jnp.zeros_like(q)
        qs_ref[0:tq, :] = jnp.where(lane < DIFF_QK, q, zero)
        qs_ref[tq:rows, :] = jnp.where(lane >= DIFF_QK, q, zero)
    else:
        qs_ref[...] = q
    m_ref[...] = jnp.full(m_ref.shape, NEG_BIG, F32)
    acc_ref[...] = jnp.zeros(acc_ref.shape, F32)

    q0 = q_pos0 + qi * tq
    lo_vis = jnp.minimum(((q0 >> CHUNK_SHIFT) + 1) << CHUNK_SHIFT, sk)
    hi_vis = jnp.minimum((((q0 + tq - 1) >> CHUNK_SHIFT) + 1) << CHUNK_SHIFT, sk)
    n_full = lo_vis // tk
    m_lo = n_full * (tk // tkm)
    m_hi = (hi_vis + tkm - 1) // tkm

    def step(start, width, masked):
        k = k_ref[pl.ds(start, width), :]
        vx = vx_ref[pl.ds(start, width), :]
        for r0 in range(0, rows, sub):
            rs = slice(r0, r0 + sub)
            s = lax.dot_general(qs_ref[rs, :], k, (((1,), (1,)), ((), ())), preferred_element_type=F32)
            if masked:
                r = (lax.broadcasted_iota(jnp.int32, (sub, width), 0) + r0) & (tq - 1)
                c = lax.broadcasted_iota(jnp.int32, (sub, width), 1)
                ok = ((start + c) >> CHUNK_SHIFT) <= ((q0 + r) >> CHUNK_SHIFT)
                s = jnp.where(ok, s, NEG_BIG)
            m_prev = m_ref[rs, :]
            m_new = jnp.maximum(m_prev, jnp.max(s, axis=-1, keepdims=True))
            alpha = jnp.exp2(m_prev - m_new)
            p = jnp.exp2(s - jnp.concatenate([m_new] * (width // LANES), axis=1))
            pv = jnp.dot(p.astype(BF16), vx, preferred_element_type=F32)
            acc_ref[rs, :] = jnp.concatenate([alpha, alpha], axis=1) * acc_ref[rs, :] + pv
            m_ref[rs, :] = m_new

    def full_body(j, carry):
        step(pl.multiple_of(j * tk, tk), tk, False)
        return carry

    def masked_body(j, carry):
        step(pl.multiple_of(j * tkm, tkm), tkm, True)
        return carry

    lax.fori_loop(0, n_full, full_body, 0)
    lax.fori_loop(m_lo, m_hi, masked_body, 0)

    o = acc_ref[:, 0:LANES] / acc_ref[:, LANES:2 * LANES]
    if n_comp == 2:
        o = o[0:tq, :] - lam_ref[0, 0] * o[tq:rows, :]
        o = _rms(o, g_ref[...]) * out_scale
    o_ref[...] = o.astype(BF16)


def _flash(q, k, v, lam, g, *, n_comp, tq, tk, tkm, q_pos0, out_scale):
    b, sq, _ = q.shape
    sk = k.shape[1]
    rows = n_comp * tq
    sub = min(rows, FLASH_SUB)
    assert sq % tq == 0 and sk % tk == 0 and tk % tkm == 0 and tq & (tq - 1) == 0 and rows % sub == 0
    return pl.pallas_call(
        functools.partial(_flash_kernel, n_comp=n_comp, tq=tq, tk=tk, tkm=tkm, sub=sub, sk=sk, q_pos0=q_pos0,
                          out_scale=out_scale),
        grid=(b, DIFF_HEADS, sq // tq),
        in_specs=[pl.BlockSpec(memory_space=pltpu.SMEM),
                  pl.BlockSpec((None, tq, LANES), lambda bi, h, i: (bi, i, h)),
                  pl.BlockSpec((None, sk, LANES), lambda bi, h, i: (bi, 0, h)),
                  pl.BlockSpec((None, sk, LANES), lambda bi, h, i: (bi, 0, h)),
                  pl.BlockSpec((1, LANES), lambda bi, h, i: (0, 0))],
        out_specs=pl.BlockSpec((None, tq, LANES), lambda bi, h, i: (bi, i, h)),
        out_shape=jax.ShapeDtypeStruct((b, sq, HEAD_W), BF16),
        scratch_shapes=[pltpu.VMEM((rows, LANES), BF16), pltpu.VMEM((sk, 2 * LANES), BF16),
                        pltpu.VMEM((rows, LANES), F32), pltpu.VMEM((rows, 2 * LANES), F32)],
        compiler_params=_cparams(("parallel", "parallel", "arbitrary")),
        name="flash_diff" if n_comp == 2 else "flash_mla",
    )(lam, q, k, v, g)


def _sublane_max(x):
    return jnp.max(x, axis=0, keepdims=True)


def _sublane_min(x):
    return jnp.min(x, axis=0, keepdims=True)


def _route_t(scores, biased):
    tm = scores[0].shape[1]
    sub = lax.broadcasted_iota(jnp.int32, (GROUP_SIZE, tm), 0)
    neg_inf = jnp.float32(-jnp.inf)
    gs = jnp.zeros((N_GROUPS, tm), F32)
    for g in range(N_GROUPS):
        bg = biased[g]
        m1 = _sublane_max(bg)
        i1 = _sublane_min(jnp.where(bg == m1, sub, GROUP_SIZE))
        m2 = _sublane_max(jnp.where(sub == i1, neg_inf, bg))
        gs = jnp.where(sub == g, m1 + m2, gs)
    keep = jnp.zeros((N_GROUPS, tm), jnp.bool_)
    cur = gs
    for _ in range(TOPK_GROUPS):
        mx = _sublane_max(cur)
        fi = _sublane_min(jnp.where(cur == mx, sub, N_GROUPS))
        hit = sub == fi
        keep = jnp.logical_or(keep, hit)
        cur = jnp.where(hit, neg_inf, cur)
    keep_f = jnp.where(keep, 1.0, 0.0)
    cand = []
    for g in range(N_GROUPS):
        kg = _sublane_max(jnp.where(sub == g, keep_f, 0.0)) > 0.5
        cand.append(jnp.where(kg, biased[g], neg_inf))
    chosen = [jnp.zeros((GROUP_SIZE, tm), jnp.bool_) for _ in range(N_GROUPS)]
    picks = []
    for _ in range(TOP_K):
        mx = cand[0]
        for g in range(1, N_GROUPS):
            mx = jnp.maximum(mx, cand[g])
        mx = _sublane_max(mx)
        fi = jnp.where(cand[0] == mx, sub, N_EXPERTS)
        for g in range(1, N_GROUPS):
            fi = jnp.minimum(fi, jnp.where(cand[g] == mx, sub + g * GROUP_SIZE, N_EXPERTS))
        fi = _sublane_min(fi)
        picks.append(fi)
        for g in range(N_GROUPS):
            hit = (sub + g * GROUP_SIZE) == fi
            chosen[g] = jnp.logical_or(chosen[g], hit)
            cand[g] = jnp.where(hit, neg_inf, cand[g])
    w = [jnp.where(chosen[g], scores[g], 0.0) for g in range(N_GROUPS)]
    tot = w[0]
    for g in range(1, N_GROUPS):
        tot = tot + w[g]
    tot = jnp.sum(tot, axis=0, keepdims=True)
    return [wg / tot * ROUTED_SCALE for wg in w], chosen, picks


def _mix_kernel(od_ref, om_ref, x_ref, wo_ref, g_ref, b_ref, wr_ref, br_ref, before_ref, below_ref, eye_ref,
                x1_ref, x1b_ref, sk_ref, skt_ref, cnt_ref, t_ref, *, alpha):
    mix = jnp.dot(od_ref[...], wo_ref[0:HEAD_W, :], preferred_element_type=F32)
    mix = mix + jnp.dot(om_ref[...], wo_ref[HEAD_W:2 * HEAD_W, :], preferred_element_type=F32)
    x1 = _layer_norm(alpha * x_ref[...] + mix, g_ref[...], b_ref[...])
    x1_ref[...] = x1
    x1b_ref[...] = x1.astype(BF16)
    logits = lax.dot_general(wr_ref[...], x1, (((1,), (1,)), ((), ())),
                             precision=lax.Precision.HIGHEST, preferred_element_type=F32)
    sc = 1.0 / (1.0 + jnp.exp(-logits))
    bi = sc + br_ref[...]
    scores = [sc[g * GROUP_SIZE:(g + 1) * GROUP_SIZE, :] for g in range(N_GROUPS)]
    biased = [bi[g * GROUP_SIZE:(g + 1) * GROUP_SIZE, :] for g in range(N_GROUPS)]
    gates, chosen, picks = _route_t(scores, biased)
    tm = x1.shape[0]
    ch = jnp.concatenate([jnp.where(c, 1.0, 0.0) for c in chosen], axis=0)
    gate_all = jnp.concatenate(gates, axis=0)
    sub = lax.broadcasted_iota(jnp.int32, (N_EXPERTS, MOE_BLOCK), 0)
    for blk in range(tm // MOE_BLOCK):
        ls = slice(blk * MOE_BLOCK, (blk + 1) * MOE_BLOCK)
        chb = ch[:, ls]
        rank = jnp.dot(chb.astype(BF16), before_ref[...], preferred_element_type=F32)
        cnt = jnp.sum(chb, axis=1, keepdims=True)
        seg = jnp.floor((cnt + (SEG_ALIGN - 1.0)) * (1.0 / SEG_ALIGN)) * SEG_ALIGN
        seg_b = jnp.broadcast_to(seg, (N_EXPERTS, LANES))
        off = jnp.dot(below_ref[...], seg_b, precision=lax.Precision.HIGHEST, preferred_element_type=F32)
        slot = jnp.concatenate([off] * (MOE_BLOCK // LANES), axis=1) + rank
        cnt_ref[blk] = jnp.sum(seg_b * eye_ref[...], axis=0, keepdims=True)
        gb = gate_all[:, ls]
        for k in range(TOP_K):
            hit = sub == picks[k][:, ls]
            t_ref[k:k + 1, ls] = jnp.sum(jnp.where(hit, slot, 0.0), axis=0, keepdims=True)
            t_ref[TOP_K + k:TOP_K + k + 1, ls] = jnp.sum(jnp.where(hit, gb, 0.0), axis=0, keepdims=True)
    t_ref[2 * TOP_K:LANES, :] = jnp.zeros((LANES - 2 * TOP_K, tm), F32)
    sk_ref[...] = t_ref[0:2 * TOP_K, :]
    skt_ref[...] = t_ref[...].T


def _mix(od, om, x, wo, g, b, wr_t, br, tm, alpha):
    t = x.shape[0]
    d = x.shape[1]
    nblk = tm // MOE_BLOCK
    row = lambda i: (i, 0)
    const = lambda i: (0, 0)
    idx = jnp.arange(MOE_BLOCK)
    before = (idx[:, None] < idx[None, :]).astype(BF16)
    ide = jnp.arange(N_EXPERTS)
    below = (ide[None, :] < ide[:, None]).astype(F32)
    eye = (ide[:, None] == jnp.arange(LANES)[None, :]).astype(F32)
    return pl.pallas_call(
        functools.partial(_mix_kernel, alpha=alpha),
        grid=(t // tm,),
        in_specs=[pl.BlockSpec((tm, HEAD_W), row), pl.BlockSpec((tm, HEAD_W), row), pl.BlockSpec((tm, d), row),
                  pl.BlockSpec(wo.shape, const), pl.BlockSpec(g.shape, const), pl.BlockSpec(b.shape, const),
                  pl.BlockSpec(wr_t.shape, const), pl.BlockSpec(br.shape, const),
                  pl.BlockSpec(before.shape, const), pl.BlockSpec(below.shape, const), pl.BlockSpec(eye.shape, const)],
        out_specs=[pl.BlockSpec((tm, d), row), pl.BlockSpec((tm, d), row),
                   pl.BlockSpec((2 * TOP_K, tm), lambda i: (0, i)), pl.BlockSpec((tm, LANES), row),
                   pl.BlockSpec((nblk, 1, LANES), lambda i: (i, 0, 0))],
        out_shape=[jax.ShapeDtypeStruct((t, d), F32), jax.ShapeDtypeStruct((t, d), BF16),
                   jax.ShapeDtypeStruct((2 * TOP_K, t), F32), jax.ShapeDtypeStruct((t, LANES), F32),
                   jax.ShapeDtypeStruct((t // MOE_BLOCK, 1, LANES), F32)],
        scratch_shapes=[pltpu.VMEM((LANES, tm), F32)],
        compiler_params=_cparams(("parallel",)),
        name="mix",
    )(od, om, x, wo, g, b, wr_t, br, before, below, eye)


def _moe_plan(cnt, n_tiles):
    seg = cnt.astype(jnp.int32)
    nb = seg.shape[0]
    before_blocks = jnp.cumsum(seg, axis=0) - seg
    length = jnp.sum(seg, axis=0)
    padded = -(-length // FFN_TILE) * FFN_TILE
    ends = jnp.cumsum(padded)
    start = ends - padded
    dst = (start[None, :] + before_blocks).reshape(nb * N_EXPERTS)
    off = (jnp.cumsum(seg, axis=1) - seg).reshape(nb * N_EXPERTS)
    nch = seg // SEG_ALIGN
    tot = jnp.sum(nch, axis=1)
    n_used = ends[-1] // FFN_TILE
    tile_start = jnp.arange(n_tiles, dtype=jnp.int32) * FFN_TILE
    tile_e = jnp.sum((ends[None, :] <= tile_start[:, None]).astype(jnp.int32), axis=1)
    last_e = tile_e[jnp.maximum(n_used - 1, 0)]
    tile_e = jnp.where(jnp.arange(n_tiles) < n_used, tile_e, last_e)
    tail_start = start + length
    tail_nch = (padded - length) // SEG_ALIGN
    return dict(off=off, nch=nch.reshape(nb * N_EXPERTS), dst=dst, tot=tot, n_used=n_used.reshape(1),
                tile_e=tile_e, tail_start=tail_start, tail_nch=tail_nch, tail_tot=jnp.sum(tail_nch).reshape(1))


def _drain(copy, n):
    def body(i, c):
        copy.wait()
        return c
    lax.fori_loop(0, n, body, 0)


def _dispatch_kernel(off_ref, nch_ref, dst_ref, tot_ref, tstart_ref, tnch_ref, ttot_ref,
                     x_ref, sk_ref, xs_ref, buf_ref, zero_ref, sem, tail_sem):
    b = pl.program_id(0)
    nb = pl.num_programs(0)
    slot = b % 2

    def seg_copy(s, src_row, dst_row):
        return pltpu.make_async_copy(buf_ref.at[s, pl.ds(pl.multiple_of(src_row, SEG_ALIGN), SEG_ALIGN), :],
                                     xs_ref.at[pl.ds(pl.multiple_of(dst_row, SEG_ALIGN), SEG_ALIGN), :], sem.at[s])

    @pl.when(b >= 2)
    def _():
        _drain(seg_copy(slot, 0, 0), tot_ref[b - 2])

    x = x_ref[...]
    sk = sk_ref[0:TOP_K, :]
    n_sub = (tot_ref[b] * SEG_ALIGN + DISP_SUB - 1) // DISP_SUB

    def build(jc, carry):
        j0 = pl.multiple_of(jc * DISP_SUB, DISP_SUB)
        jf = (lax.broadcasted_iota(jnp.int32, (DISP_SUB, MOE_BLOCK), 0) + j0).astype(F32)
        hit = jf == sk[0:1, :]
        for k in range(1, TOP_K):
            hit = jnp.logical_or(hit, jf == sk[k:k + 1, :])
        sel = jnp.where(hit, 1.0, 0.0).astype(BF16)
        buf_ref[slot, pl.ds(j0, DISP_SUB), :] = jnp.dot(sel, x, preferred_element_type=F32).astype(BF16)
        return carry

    lax.fori_loop(0, n_sub, build, 0)

    def per_expert(e, carry):
        i = b * N_EXPERTS + e
        o = off_ref[i]
        d = dst_ref[i]

        def per_chunk(c, carry2):
            seg_copy(slot, o + c * SEG_ALIGN, d + c * SEG_ALIGN).start()
            return carry2

        lax.fori_loop(0, nch_ref[i], per_chunk, 0)
        return carry

    lax.fori_loop(0, N_EXPERTS, per_expert, 0)

    @pl.when(b == nb - 1)
    def _():
        _drain(seg_copy(slot, 0, 0), tot_ref[b])

        @pl.when(b >= 1)
        def _():
            _drain(seg_copy(1 - slot, 0, 0), tot_ref[jnp.maximum(b - 1, 0)])

        zero_ref[...] = jnp.zeros(zero_ref.shape, BF16)

        def tail_copy(dst_row):
            return pltpu.make_async_copy(zero_ref, xs_ref.at[pl.ds(pl.multiple_of(dst_row, SEG_ALIGN), SEG_ALIGN), :],
                                         tail_sem.at[0])

        def tail_expert(e, carry):
            def tail_chunk(c, carry2):
                tail_copy(tstart_ref[e] + c * SEG_ALIGN).start()
                return carry2
            lax.fori_loop(0, tnch_ref[e], tail_chunk, 0)
            return carry

        lax.fori_loop(0, N_EXPERTS, tail_expert, 0)
        _drain(tail_copy(0), ttot_ref[0])


def _dispatch(plan, x1b, sk, n_rows):
    t, d = x1b.shape
    nb = t // MOE_BLOCK
    grid_spec = pltpu.PrefetchScalarGridSpec(
        num_scalar_prefetch=7,
        grid=(nb,),
        in_specs=[pl.BlockSpec((MOE_BLOCK, d), lambda b, *_: (b, 0)),
                  pl.BlockSpec((2 * TOP_K, MOE_BLOCK), lambda b, *_: (0, b))],
        out_specs=pl.BlockSpec(memory_space=pl.ANY),
        scratch_shapes=[pltpu.VMEM((2, BLOCK_ROWS, d), BF16), pltpu.VMEM((SEG_ALIGN, d), BF16),
                        pltpu.SemaphoreType.DMA((2,)), pltpu.SemaphoreType.DMA((1,))],
    )
    return pl.pallas_call(
        _dispatch_kernel,
        grid_spec=grid_spec,
        out_shape=jax.ShapeDtypeStruct((n_rows, d), BF16),
        compiler_params=_cparams(("arbitrary",)),
        name="dispatch",
    )(plan["off"], plan["nch"], plan["dst"], plan["tot"], plan["tail_start"], plan["tail_nch"], plan["tail_tot"],
      x1b, sk)


def _ffn_kernel(te_ref, nu_ref, xs_ref, wg_ref, wu_ref, wd_ref, y_ref):
    @pl.when(pl.program_id(0) < nu_ref[0])
    def _():
        x = xs_ref[...]
        hg = jnp.dot(x, wg_ref[...], preferred_element_type=F32)
        hu = jnp.dot(x, wu_ref[...], preferred_element_type=F32)
        h = hg * (1.0 / (1.0 + jnp.exp(-hg))) * hu
        y_ref[...] = jnp.dot(h.astype(BF16), wd_ref[...], preferred_element_type=F32).astype(BF16)


def _ffn(plan, xs, wg, wu, wd):
    n_rows, d = xs.shape
    n_tiles = n_rows // FFN_TILE
    used = lambda i, te, nu: (jnp.minimum(i, nu[0] - 1), 0)
    wsel = lambda i, te, nu: (te[i], 0, 0)
    grid_spec = pltpu.PrefetchScalarGridSpec(
        num_scalar_prefetch=2,
        grid=(n_tiles,),
        in_specs=[pl.BlockSpec((FFN_TILE, d), used),
                  pl.BlockSpec((None, d, EXPERT_DIM), wsel), pl.BlockSpec((None, d, EXPERT_DIM), wsel),
                  pl.BlockSpec((None, EXPERT_DIM, d), wsel)],
        out_specs=pl.BlockSpec((FFN_TILE, d), used),
    )
    return pl.pallas_call(
        _ffn_kernel,
        grid_spec=grid_spec,
        out_shape=jax.ShapeDtypeStruct((n_rows, d), BF16),
        compiler_params=_cparams(("arbitrary",)),
        name="ffn",
    )(plan["tile_e"], plan["n_used"], xs, wg, wu, wd)


def _combine_kernel(off_ref, nch_ref, dst_ref, tot_ref, y_ref, skt_ref, x1_ref, x1b_ref, wsg_ref, wsu_ref, wsd_ref,
                    g_ref, b_ref, o_ref, buf_ref, acc_ref, sem, *, alpha):
    b = pl.program_id(0)
    nb = pl.num_programs(0)
    slot = b % 2

    def seg_copy(s, src_row, dst_row):
        return pltpu.make_async_copy(y_ref.at[pl.ds(pl.multiple_of(src_row, SEG_ALIGN), SEG_ALIGN), :],
                                     buf_ref.at[s, pl.ds(pl.multiple_of(dst_row, SEG_ALIGN), SEG_ALIGN), :], sem.at[s])

    def fetch(blk, s):
        def per_expert(e, carry):
            i = blk * N_EXPERTS + e
            o = off_ref[i]
            d = dst_ref[i]

            def per_chunk(c, carry2):
                seg_copy(s, d + c * SEG_ALIGN, o + c * SEG_ALIGN).start()
                return carry2

            lax.fori_loop(0, nch_ref[i], per_chunk, 0)
            return carry

        lax.fori_loop(0, N_EXPERTS, per_expert, 0)

    @pl.when(b == 0)
    def _():
        buf_ref[...] = jnp.zeros(buf_ref.shape, BF16)
        fetch(0, 0)

    @pl.when(b + 1 < nb)
    def _():
        fetch(b + 1, 1 - slot)

    xb = x1b_ref[...]
    hg = jnp.dot(xb, wsg_ref[...], preferred_element_type=F32)
    hu = jnp.dot(xb, wsu_ref[...], preferred_element_type=F32)
    h = hg * (1.0 / (1.0 + jnp.exp(-hg))) * hu
    acc_ref[...] = jnp.dot(h.astype(BF16), wsd_ref[...], preferred_element_type=F32)

    _drain(seg_copy(slot, 0, 0), tot_ref[b])

    skt = skt_ref[...]
    slot_b = [jnp.broadcast_to(skt[:, k:k + 1], (MOE_BLOCK, DISP_SUB)) for k in range(TOP_K)]
    gate_b = [jnp.broadcast_to(skt[:, TOP_K + k:TOP_K + k + 1], (MOE_BLOCK, DISP_SUB)) for k in range(TOP_K)]
    n_sub = (tot_ref[b] * SEG_ALIGN + DISP_SUB - 1) // DISP_SUB

    def gather(jc, carry):
        j0 = pl.multiple_of(jc * DISP_SUB, DISP_SUB)
        jf = (lax.broadcasted_iota(jnp.int32, (MOE_BLOCK, DISP_SUB), 1) + j0).astype(F32)
        w = jnp.zeros((MOE_BLOCK, DISP_SUB), F32)
        for k in range(TOP_K):
            w = jnp.where(slot_b[k] == jf, gate_b[k], w)
        acc_ref[...] += jnp.dot(w.astype(BF16), buf_ref[slot, pl.ds(j0, DISP_SUB), :], preferred_element_type=F32)
        return carry

    lax.fori_loop(0, n_sub, gather, 0)
    o_ref[...] = _layer_norm(alpha * x1_ref[...] + acc_ref[...], g_ref[...], b_ref[...])


def _combine(plan, y, skt, x1, x1b, wsg, wsu, wsd, g, b, alpha):
    t, d = x1.shape
    nb = t // MOE_BLOCK
    row = lambda i, *_: (i, 0)
    const = lambda i, *_: (0, 0)
    grid_spec = pltpu.PrefetchScalarGridSpec(
        num_scalar_prefetch=4,
        grid=(nb,),
        in_specs=[pl.BlockSpec(memory_space=pl.ANY), pl.BlockSpec((MOE_BLOCK, LANES), row),
                  pl.BlockSpec((MOE_BLOCK, d), row), pl.BlockSpec((MOE_BLOCK, d), row),
                  pl.BlockSpec(wsg.shape, const), pl.BlockSpec(wsu.shape, const), pl.BlockSpec(wsd.shape, const),
                  pl.BlockSpec(g.shape, const), pl.BlockSpec(b.shape, const)],
        out_specs=pl.BlockSpec((MOE_BLOCK, d), row),
        scratch_shapes=[pltpu.VMEM((2, BLOCK_ROWS, d), BF16), pltpu.VMEM((MOE_BLOCK, d), F32),
                        pltpu.SemaphoreType.DMA((2,))],
    )
    return pl.pallas_call(
        functools.partial(_combine_kernel, alpha=alpha),
        grid_spec=grid_spec,
        out_shape=jax.ShapeDtypeStruct((t, d), F32),
        compiler_params=_cparams(("arbitrary",)),
        name="combine",
    )(plan["off"], plan["nch"], plan["dst"], plan["tot"], y, skt, x1, x1b, wsg, wsu, wsd, g, b)


def _moe(x1, x1b, sk, skt, cnt, wg, wu, wd, wsg, wsu, wsd, g, b, alpha):
    t = x1.shape[0]
    nb = t // MOE_BLOCK
    n_tiles = -(-(nb * BLOCK_ROWS) // FFN_TILE) + N_EXPERTS
    plan = _moe_plan(cnt[:, 0, :N_EXPERTS], n_tiles)
    xs = _dispatch(plan, x1b, sk, n_tiles * FFN_TILE)
    y = _ffn(plan, xs, wg, wu, wd)
    return _combine(plan, y, skt, x1, x1b, wsg, wsu, wsd, g, b, alpha)


def _rope_cs(pos, dim):
    inv = ROPE_THETA ** (-jnp.arange(0, dim, 2, dtype=F32) / dim)
    ang = pos.astype(F32)[:, None] * inv[None, :]
    return jnp.cos(ang), jnp.sin(ang)


def _tables(pos, reps):
    n = pos.shape[0]
    c32, s32 = _rope_cs(pos, DIFF_QK)
    c16, s16 = _rope_cs(pos, MLA_ROPE)
    one = lambda w: jnp.ones((n, w), F32)
    zero = lambda w: jnp.zeros((n, w), F32)
    cd = jnp.concatenate([c32] * 4, axis=1)
    sd = jnp.concatenate([-s32, s32] * 2, axis=1)
    cq = jnp.concatenate([one(MLA_NOPE), c16, c16, one(32)], axis=1)
    sq = jnp.concatenate([zero(MLA_NOPE), -s16, s16, zero(32)], axis=1)
    ck = jnp.concatenate([c16, c16, zero(96)], axis=1)
    sk = jnp.concatenate([-s16, s16, zero(96)], axis=1)
    return tuple(jnp.tile(a, (reps, 1)) for a in (cd, sd, cq, sq, ck, sk))


def kernel(x_prompt, x_sample, cache_diff_k, cache_diff_v, cache_mla_ckv, cache_mla_kpe, w_in, diff_lambda, diff_subln_g, mla_q_norm_g, mla_w_uq, mla_kv_norm_g, mla_w_ukv, w_out, ln1_g, ln1_b, w_router, b_router, w_exp_gate, w_exp_up, w_exp_down, w_sh_gate, w_sh_up, w_sh_down, ln2_g, ln2_b):
    depth = w_in.shape[0]
    assert depth == 1
    d_model = x_prompt.shape[-1]
    alpha = (2.0 * depth) ** 0.25
    past_len = cache_diff_k.shape[2]
    layer = 0
    lambda_init = 0.8 - 0.6 * math.exp(-0.3 * layer)

    w_in_b = jnp.pad(w_in[layer], ((0, 0), (0, IN_PAD - IN_WIDTH))).astype(BF16)
    wuq = jnp.pad(mla_w_uq[layer], ((0, 0), (0, 0), (0, LANES - MLA_NOPE - MLA_ROPE)))
    wuq = wuq.reshape(Q_LORA, HEAD_W).astype(BF16)
    wukv = mla_w_ukv[layer]
    wuk = jnp.pad(wukv[:, :, :MLA_NOPE], ((0, 0), (0, 0), (0, LANES - MLA_NOPE))).reshape(KV_LORA, HEAD_W).astype(BF16)
    wuv = wukv[:, :, MLA_NOPE:].reshape(KV_LORA, HEAD_W).astype(BF16)
    place = jnp.pad(jnp.eye(MLA_ROPE, dtype=F32), ((0, 0), (MLA_NOPE, LANES - MLA_NOPE - MLA_ROPE)))
    place = jnp.tile(place, (1, MLA_HEADS)).astype(BF16)
    gq = mla_q_norm_g[layer].reshape(1, Q_LORA)
    gkv = mla_kv_norm_g[layer].reshape(1, KV_LORA)
    gsub = diff_subln_g[layer].reshape(1, LANES)
    wo = w_out[layer].astype(BF16)
    g1, b1 = ln1_g[layer].reshape(1, d_model), ln1_b[layer].reshape(1, d_model)
    g2, b2 = ln2_g[layer].reshape(1, d_model), ln2_b[layer].reshape(1, d_model)
    wr_t = w_router[layer].T
    br = b_router[layer].reshape(N_EXPERTS, 1)
    wg, wu, wd = (w[layer].astype(BF16) for w in (w_exp_gate, w_exp_up, w_exp_down))
    wsg, wsu, wsd = (w[layer].astype(BF16) for w in (w_sh_gate, w_sh_up, w_sh_down))
    lp = diff_lambda[layer].astype(F32)
    lam = jnp.exp(jnp.sum(lp[0] * lp[1])) - jnp.exp(jnp.sum(lp[2] * lp[3])) + lambda_init
    lam = lam.reshape(1, 1)

    def group(x, pos, past, tm, tq, tk, tkm):
        b, s, _ = x.shape
        t = b * s
        tm = min(tm, t)
        xf = x.reshape(t, d_model)
        reps = max(1, tm // s)
        tables = _tables(pos, reps)
        n_pat = (s * reps) // tm
        qd, kd32, kd16, vd32, vd16, ckv, kpe, qm = _proj(xf, w_in_b, wuq, gq, gkv, tables, tm, n_pat)
        if past is None:
            k_d, v_d = kd16.reshape(b, s, HEAD_W), vd16.reshape(b, s, HEAD_W)
            ckv_all, kpe_all = ckv, kpe
            sk, q_pos0 = s, 0
        else:
            pk, pv, pc, pp = past
            sk = -(-(past_len + s) // tk) * tk
            padr = sk - past_len - s
            cat = lambda old, new: jnp.pad(jnp.concatenate([old, new], axis=1), ((0, 0), (0, padr), (0, 0)))
            k_d = cat(pk.reshape(b, past_len, HEAD_W).astype(BF16), kd16.reshape(b, s, HEAD_W))
            v_d = cat(pv.reshape(b, past_len, HEAD_W).astype(BF16), vd16.reshape(b, s, HEAD_W))
            ckv_all = cat(pc, ckv.reshape(b, s, KV_LORA)).reshape(b * sk, KV_LORA)
            kpe_all = cat(pp, kpe.reshape(b, s, MLA_ROPE)).reshape(b * sk, MLA_ROPE)
            q_pos0 = past_len
        k_m, v_m = _kvup(ckv_all, kpe_all, wuk, place, wuv, min(1024, ckv_all.shape[0]))
        o_d = _flash(qd.reshape(b, s, HEAD_W), k_d, v_d, lam, gsub, n_comp=2, tq=tq, tk=tk, tkm=tkm,
                     q_pos0=q_pos0, out_scale=1.0 - lambda_init)
        o_m = _flash(qm.reshape(b, s, HEAD_W), k_m.reshape(b, sk, HEAD_W), v_m.reshape(b, sk, HEAD_W), lam, gsub,
                     n_comp=1, tq=tq, tk=tk, tkm=tkm, q_pos0=q_pos0, out_scale=1.0)
        x1, x1b, sk, skt, cnt = _mix(o_d.reshape(t, HEAD_W), o_m.reshape(t, HEAD_W), xf, wo, g1, b1, wr_t, br,
                                     tm, alpha)
        y = _moe(x1, x1b, sk, skt, cnt, wg, wu, wd, wsg, wsu, wsd, g2, b2, alpha)
        rows = (kd32.reshape(1, b, s, DIFF_HEADS, LANES), vd32.reshape(1, b, s, DIFF_HEADS, LANES),
                ckv.reshape(1, b, s, KV_LORA), kpe.reshape(1, b, s, MLA_ROPE))
        return y.reshape(b, s, d_model), rows

    s_p = x_prompt.shape[1]
    s_s = x_sample.shape[1]
    pos_p = jnp.arange(s_p, dtype=jnp.int32)
    pos_s = past_len + jnp.arange(s_s, dtype=jnp.int32)
    y_p, r_p = group(x_prompt, pos_p, None, 512, 256, 512, 256)
    past = (cache_diff_k[layer], cache_diff_v[layer], cache_mla_ckv[layer], cache_mla_kpe[layer])
    y_s, r_s = group(x_sample, pos_s, past, 512, s_s, 512, 256)
    return (y_p, y_s) + r_p + r_s
```

```python
import functools
import math

import jax
import jax.numpy as jnp
from jax import lax
from jax.experimental import pallas as pl
from jax.experimental.pallas import tpu as pltpu

F32 = jnp.float32
BF16 = jnp.bfloat16

LANES = 128
VMEM_LIMIT = 52 * 1024 * 1024

CHUNK = 64
CHUNK_SHIFT = 6
ROPE_THETA = 10000.0
LN_EPS = 1e-5
RMS_EPS = 1e-6
LOG2E = 1.4426950408889634
NEG_BIG = -1e30
FLASH_SUB = 128

DIFF_HEADS = 4
DIFF_QK = 64
MLA_HEADS = 4
MLA_NOPE = 64
MLA_ROPE = 32
MLA_V = 128
Q_LORA = 384
KV_LORA = 256
N_EXPERTS = 64
N_GROUPS = 8
GROUP_SIZE = N_EXPERTS // N_GROUPS
TOPK_GROUPS = 4
TOP_K = 8
ROUTED_SCALE = 2.5
EXPERT_DIM = 256

MOE_BLOCK = 256
SEG_ALIGN = 16
ORD_SUB = 256
DISP_SUB = 1024
BLOCK_ROWS = -(-(TOP_K * MOE_BLOCK + N_EXPERTS * (SEG_ALIGN - 1)) // DISP_SUB) * DISP_SUB
BLOCK_CHUNKS = BLOCK_ROWS // SEG_ALIGN
FFN_TILE = 1024

DQ_W = DIFF_HEADS * 2 * DIFF_QK
HEAD_W = DIFF_HEADS * LANES
IN_WIDTH = 3 * DQ_W + Q_LORA + KV_LORA + MLA_ROPE
IN_PAD = 2304
OFF_DK, OFF_DV, OFF_CQ, OFF_CKV, OFF_KPE = 512, 1024, 1536, 1920, 2176


def _cparams(sem):
    return pltpu.CompilerParams(dimension_semantics=sem, vmem_limit_bytes=VMEM_LIMIT)


def _rms(x, g):
    return x * lax.rsqrt(jnp.mean(x * x, axis=-1, keepdims=True) + RMS_EPS) * g


def _layer_norm(x, g, b):
    mu = jnp.mean(x, axis=-1, keepdims=True)
    xc = x - mu
    var = jnp.mean(xc * xc, axis=-1, keepdims=True)
    return xc * lax.rsqrt(var + LN_EPS) * g + b


def _proj_kernel(x_ref, w_ref, wuq_ref, gq_ref, gkv_ref, cd_ref, sd_ref, cq_ref, sq_ref, ck_ref, sk_ref,
                 qd_ref, kd32_ref, kd16_ref, vd32_ref, vd16_ref, ckv_ref, kpe_ref, qm_ref,
                 *, scale_d, scale_m):
    tm = x_ref.shape[0]
    x = x_ref[...].astype(BF16)
    proj = jnp.dot(x, w_ref[...], preferred_element_type=F32)
    lane = lax.broadcasted_iota(jnp.int32, (tm, LANES), 1)

    first_d = (lane & 63) < 32
    cd = cd_ref[...]
    sd = sd_ref[...]

    def rope_d(blk):
        rot = jnp.where(first_d, pltpu.roll(blk, LANES - 32, 1), pltpu.roll(blk, 32, 1))
        return blk * cd + rot * sd

    for j in range(DIFF_HEADS):
        sl = slice(j * LANES, (j + 1) * LANES)
        qd_ref[:, sl] = (rope_d(proj[:, sl]) * scale_d).astype(BF16)
        kr = rope_d(proj[:, OFF_DK + j * LANES:OFF_DK + (j + 1) * LANES])
        kd32_ref[:, j, :] = kr
        kd16_ref[:, sl] = kr.astype(BF16)
        dv = proj[:, OFF_DV + j * LANES:OFF_DV + (j + 1) * LANES]
        vd32_ref[:, j, :] = dv
        vd16_ref[:, sl] = dv.astype(BF16)

    cqn = _rms(proj[:, OFF_CQ:OFF_CQ + Q_LORA], gq_ref[...])
    q = jnp.dot(cqn.astype(BF16), wuq_ref[...], preferred_element_type=F32)
    cq = cq_ref[...]
    sq = sq_ref[...]
    first_q = lane < (MLA_NOPE + MLA_ROPE // 2)
    for h in range(MLA_HEADS):
        sl = slice(h * LANES, (h + 1) * LANES)
        blk = q[:, sl]
        rot = jnp.where(first_q, pltpu.roll(blk, LANES - 16, 1), pltpu.roll(blk, 16, 1))
        qm_ref[:, sl] = ((blk * cq + rot * sq) * scale_m).astype(BF16)

    ckv_ref[...] = _rms(proj[:, OFF_CKV:OFF_CKV + KV_LORA], gkv_ref[...])

    kb = proj[:, OFF_KPE:OFF_KPE + LANES]
    rot = jnp.where(lane < 16, pltpu.roll(kb, LANES - 16, 1), pltpu.roll(kb, 16, 1))
    kpe_ref[...] = (kb * ck_ref[...] + rot * sk_ref[...])[:, :MLA_ROPE]


def _proj(x, w_in, wuq, gq, gkv, tables, tm, n_pat):
    t = x.shape[0]
    row = lambda i: (i, 0)
    const = lambda i: (0, 0)
    pat = lambda i: (i % n_pat, 0)
    tab_spec = pl.BlockSpec((tm, LANES), pat)
    out_w = lambda w, dt: jax.ShapeDtypeStruct((t, w), dt)
    cache_spec = pl.BlockSpec((tm, DIFF_HEADS, LANES), lambda i: (i, 0, 0))
    cache_shape = jax.ShapeDtypeStruct((t, DIFF_HEADS, LANES), F32)
    return pl.pallas_call(
        functools.partial(_proj_kernel, scale_d=LOG2E * DIFF_QK ** -0.5,
                          scale_m=LOG2E * (MLA_NOPE + MLA_ROPE) ** -0.5),
        grid=(t // tm,),
        in_specs=[pl.BlockSpec((tm, x.shape[1]), row),
                  pl.BlockSpec(w_in.shape, const), pl.BlockSpec(wuq.shape, const),
                  pl.BlockSpec(gq.shape, const), pl.BlockSpec(gkv.shape, const)] + [tab_spec] * 6,
        out_specs=[pl.BlockSpec((tm, HEAD_W), row), cache_spec, pl.BlockSpec((tm, HEAD_W), row), cache_spec,
                   pl.BlockSpec((tm, HEAD_W), row), pl.BlockSpec((tm, KV_LORA), row),
                   pl.BlockSpec((tm, MLA_ROPE), row), pl.BlockSpec((tm, HEAD_W), row)],
        out_shape=[out_w(HEAD_W, BF16), cache_shape, out_w(HEAD_W, BF16), cache_shape,
                   out_w(HEAD_W, BF16), out_w(KV_LORA, F32), out_w(MLA_ROPE, F32), out_w(HEAD_W, BF16)],
        compiler_params=_cparams(("parallel",)),
        name="proj",
    )(x, w_in, wuq, gq, gkv, *tables)


def _kvup_kernel(ckv_ref, kpe_ref, wuk_ref, place_ref, wuv_ref, k_ref, v_ref):
    c = ckv_ref[...].astype(BF16)
    k = jnp.dot(c, wuk_ref[...], preferred_element_type=F32)
    k = k + jnp.dot(kpe_ref[...].astype(BF16), place_ref[...], preferred_element_type=F32)
    k_ref[...] = k.astype(BF16)
    v_ref[...] = jnp.dot(c, wuv_ref[...], preferred_element_type=F32).astype(BF16)


def _kvup(ckv, kpe, wuk, place, wuv, tm):
    r = ckv.shape[0]
    row = lambda i: (i, 0)
    const = lambda i: (0, 0)
    return pl.pallas_call(
        _kvup_kernel,
        grid=(r // tm,),
        in_specs=[pl.BlockSpec((tm, KV_LORA), row), pl.BlockSpec((tm, MLA_ROPE), row),
                  pl.BlockSpec(wuk.shape, const), pl.BlockSpec(place.shape, const), pl.BlockSpec(wuv.shape, const)],
        out_specs=[pl.BlockSpec((tm, HEAD_W), row)] * 2,
        out_shape=[jax.ShapeDtypeStruct((r, HEAD_W), BF16)] * 2,
        compiler_params=_cparams(("parallel",)),
        name="kvup",
    )(ckv, kpe, wuk, place, wuv)


def _flash_kernel(lam_ref, q_ref, k_ref, v_ref, g_ref, o_ref, qs_ref, vx_ref, m_ref, acc_ref,
                  *, n_comp, tq, tk, tkm, sub, sk, q_pos0, out_scale):
    qi = pl.program_id(2)
    rows = n_comp * tq

    @pl.when(qi == 0)
    def _():
        vx_ref[:, 0:LANES] = v_ref[...]
        vx_ref[:, LANES:2 * LANES] = jnp.ones((sk, LANES), BF16)

    q = q_ref[...]
    if n_comp == 2:
        lane = lax.broadcasted_iota(jnp.int32, (tq, LANES), 1)
        zero = jnp.zeros_like(q)
        qs_ref[0:tq, :] = jnp.where(lane < DIFF_QK, q, zero)
        qs_ref[tq:rows, :] = jnp.where(lane >= DIFF_QK, q, zero)
    else:
        qs_ref[...] = q
    m_ref[...] = jnp.full(m_ref.shape, NEG_BIG, F32)
    acc_ref[...] = jnp.zeros(acc_ref.shape, F32)

    q0 = q_pos0 + qi * tq
    lo_vis = jnp.minimum(((q0 >> CHUNK_SHIFT) + 1) << CHUNK_SHIFT, sk)
    hi_vis = jnp.minimum((((q0 + tq - 1) >> CHUNK_SHIFT) + 1) << CHUNK_SHIFT, sk)
    n_full = lo_vis // tk
    m_lo = n_full * (tk // tkm)
    m_hi = (hi_vis + tkm - 1) // tkm

    def step(start, width, masked):
        k = k_ref[pl.ds(start, width), :]
        vx = vx_ref[pl.ds(start, width), :]
        for r0 in range(0, rows, sub):
            rs = slice(r0, r0 + sub)
            s = lax.dot_general(qs_ref[rs, :], k, (((1,), (1,)), ((), ())), preferred_element_type=F32)
            if masked:
                r = (lax.broadcasted_iota(jnp.int32, (sub, width), 0) + r0) & (tq - 1)
                c = lax.broadcasted_iota(jnp.int32, (sub, width), 1)
                ok = ((start + c) >> CHUNK_SHIFT) <= ((q0 + r) >> CHUNK_SHIFT)
                s = jnp.where(ok, s, NEG_BIG)
            m_prev = m_ref[rs, :]
            m_new = jnp.maximum(m_prev, jnp.max(s, axis=-1, keepdims=True))
            alpha = jnp.exp2(m_prev - m_new)
            p = jnp.exp2(s - jnp.concatenate([m_new] * (width // LANES), axis=1))
            pv = jnp.dot(p.astype(BF16), vx, preferred_element_type=F32)
            acc_ref[rs, :] = jnp.concatenate([alpha, alpha], axis=1) * acc_ref[rs, :] + pv
            m_ref[rs, :] = m_new

    def full_body(j, carry):
        step(pl.multiple_of(j * tk, tk), tk, False)
        return carry

    def masked_body(j, carry):
        step(pl.multiple_of(j * tkm, tkm), tkm, True)
        return carry

    lax.fori_loop(0, n_full, full_body, 0)
    lax.fori_loop(m_lo, m_hi, masked_body, 0)

    o = acc_ref[:, 0:LANES] / acc_ref[:, LANES:2 * LANES]
    if n_comp == 2:
        o = o[0:tq, :] - lam_ref[0, 0] * o[tq:rows, :]
        o = _rms(o, g_ref[...]) * out_scale
    o_ref[...] = o.astype(BF16)


def _flash(q, k, v, lam, g, *, n_comp, tq, tk, tkm, q_pos0, out_scale):
    b, sq, _ = q.shape
    sk = k.shape[1]
    rows = n_comp * tq
    sub = min(rows, FLASH_SUB)
    assert sq % tq == 0 and sk % tk == 0 and tk % tkm == 0 and tq & (tq - 1) == 0 and rows % sub == 0
    return pl.pallas_call(
        functools.partial(_flash_kernel, n_comp=n_comp, tq=tq, tk=tk, tkm=tkm, sub=sub, sk=sk, q_pos0=q_pos0,
                          out_scale=out_scale),
        grid=(b, DIFF_HEADS, sq // tq),
        in_specs=[pl.BlockSpec(memory_space=pltpu.SMEM),
                  pl.BlockSpec((None, tq, LANES), lambda bi, h, i: (bi, i, h)),
                  pl.BlockSpec((None, sk, LANES), lambda bi, h, i: (bi, 0, h)),
                  pl.BlockSpec((None, sk, LANES), lambda bi, h, i: (bi, 0, h)),
                  pl.BlockSpec((1, LANES), lambda bi, h, i: (0, 0))],
        out_specs=pl.BlockSpec((None, tq, LANES), lambda bi, h, i: (bi, i, h)),
        out_shape=jax.ShapeDtypeStruct((b, sq, HEAD_W), BF16),
        scratch_shapes=[pltpu.VMEM((rows, LANES), BF16), pltpu.VMEM((sk, 2 * LANES), BF16),
                        pltpu.VMEM((rows, LANES), F32), pltpu.VMEM((rows, 2 * LANES), F32)],
        compiler_params=_cparams(("parallel", "parallel", "arbitrary")),
        name="flash_diff" if n_comp == 2 else "flash_mla",
    )(lam, q, k, v, g)


def _sublane_max(x):
    return jnp.max(x, axis=0, keepdims=True)


def _sublane_min(x):
    return jnp.min(x, axis=0, keepdims=True)


def _route_t(scores, biased):
    tm = scores[0].shape[1]
    sub = lax.broadcasted_iota(jnp.int32, (GROUP_SIZE, tm), 0)
    neg_inf = jnp.float32(-jnp.inf)
    gs = jnp.zeros((N_GROUPS, tm), F32)
    for g in range(N_GROUPS):
        bg = biased[g]
        m1 = _sublane_max(bg)
        i1 = _sublane_min(jnp.where(bg == m1, sub, GROUP_SIZE))
        m2 = _sublane_max(jnp.where(sub == i1, neg_inf, bg))
        gs = jnp.where(sub == g, m1 + m2, gs)
    keep = jnp.zeros((N_GROUPS, tm), jnp.bool_)
    cur = gs
    for _ in range(TOPK_GROUPS):
        mx = _sublane_max(cur)
        fi = _sublane_min(jnp.where(cur == mx, sub, N_GROUPS))
        hit = sub == fi
        keep = jnp.logical_or(keep, hit)
        cur = jnp.where(hit, neg_inf, cur)
    keep_f = jnp.where(keep, 1.0, 0.0)
    cand = []
    for g in range(N_GROUPS):
        kg = _sublane_max(jnp.where(sub == g, keep_f, 0.0)) > 0.5
        cand.append(jnp.where(kg, biased[g], neg_inf))
    chosen = [jnp.zeros((GROUP_SIZE, tm), jnp.bool_) for _ in range(N_GROUPS)]
    picks = []
    for _ in range(TOP_K):
        mx = cand[0]
        for g in range(1, N_GROUPS):
            mx = jnp.maximum(mx, cand[g])
        mx = _sublane_max(mx)
        fi = jnp.where(cand[0] == mx, sub, N_EXPERTS)
        for g in range(1, N_GROUPS):
            fi = jnp.minimum(fi, jnp.where(cand[g] == mx, sub + g * GROUP_SIZE, N_EXPERTS))
        fi = _sublane_min(fi)
        picks.append(fi)
        for g in range(N_GROUPS):
            hit = (sub + g * GROUP_SIZE) == fi
            chosen[g] = jnp.logical_or(chosen[g], hit)
            cand[g] = jnp.where(hit, neg_inf, cand[g])
    w = [jnp.where(chosen[g], scores[g], 0.0) for g in range(N_GROUPS)]
    tot = w[0]
    for g in range(1, N_GROUPS):
        tot = tot + w[g]
    tot = jnp.sum(tot, axis=0, keepdims=True)
    return [wg / tot * ROUTED_SCALE for wg in w], chosen, picks


def _mix_kernel(od_ref, om_ref, x_ref, wo_ref, g_ref, b_ref, wr_ref, br_ref, before_ref, below_ref, eye_ref,
                x1_ref, x1b_ref, sk_ref, cnt_ref, *, alpha):
    mix = jnp.dot(od_ref[...], wo_ref[0:HEAD_W, :], preferred_element_type=F32)
    mix = mix + jnp.dot(om_ref[...], wo_ref[HEAD_W:2 * HEAD_W, :], preferred_element_type=F32)
    x1 = _layer_norm(alpha * x_ref[...] + mix, g_ref[...], b_ref[...])
    x1_ref[...] = x1
    x1b_ref[...] = x1.astype(BF16)
    logits = lax.dot_general(wr_ref[...], x1, (((1,), (1,)), ((), ())),
                             precision=lax.Precision.HIGHEST, preferred_element_type=F32)
    sc = 1.0 / (1.0 + jnp.exp(-logits))
    bi = sc + br_ref[...]
    scores = [sc[g * GROUP_SIZE:(g + 1) * GROUP_SIZE, :] for g in range(N_GROUPS)]
    biased = [bi[g * GROUP_SIZE:(g + 1) * GROUP_SIZE, :] for g in range(N_GROUPS)]
    gates, chosen, picks = _route_t(scores, biased)
    tm = x1.shape[0]
    ch = jnp.concatenate([jnp.where(c, 1.0, 0.0) for c in chosen], axis=0)
    gate_all = jnp.concatenate(gates, axis=0)
    sub = lax.broadcasted_iota(jnp.int32, (N_EXPERTS, MOE_BLOCK), 0)
    for blk in range(tm // MOE_BLOCK):
        ls = slice(blk * MOE_BLOCK, (blk + 1) * MOE_BLOCK)
        chb = ch[:, ls]
        rank = jnp.dot(chb.astype(BF16), before_ref[...], preferred_element_type=F32)
        cnt = jnp.sum(chb, axis=1, keepdims=True)
        seg = jnp.floor((cnt + (SEG_ALIGN - 1.0)) * (1.0 / SEG_ALIGN)) * SEG_ALIGN
        seg_b = jnp.broadcast_to(seg, (N_EXPERTS, LANES))
        off = jnp.dot(below_ref[...], seg_b, precision=lax.Precision.HIGHEST, preferred_element_type=F32)
        slot = jnp.concatenate([off] * (MOE_BLOCK // LANES), axis=1) + rank
        cnt_ref[blk] = jnp.sum(seg_b * eye_ref[...], axis=0, keepdims=True)
        gb = gate_all[:, ls]
        for k in range(TOP_K):
            hit = sub == picks[k][:, ls]
            sk_ref[k:k + 1, ls] = jnp.sum(jnp.where(hit, slot, 0.0), axis=0, keepdims=True)
            sk_ref[TOP_K + k:TOP_K + k + 1, ls] = jnp.sum(jnp.where(hit, gb, 0.0), axis=0, keepdims=True)


def _mix(od, om, x, wo, g, b, wr_t, br, tm, alpha):
    t = x.shape[0]
    d = x.shape[1]
    nblk = tm // MOE_BLOCK
    row = lambda i: (i, 0)
    const = lambda i: (0, 0)
    idx = jnp.arange(MOE_BLOCK)
    before = (idx[:, None] < idx[None, :]).astype(BF16)
    ide = jnp.arange(N_EXPERTS)
    below = (ide[None, :] < ide[:, None]).astype(F32)
    eye = (ide[:, None] == jnp.arange(LANES)[None, :]).astype(F32)
    return pl.pallas_call(
        functools.partial(_mix_kernel, alpha=alpha),
        grid=(t // tm,),
        in_specs=[pl.BlockSpec((tm, HEAD_W), row), pl.BlockSpec((tm, HEAD_W), row), pl.BlockSpec((tm, d), row),
                  pl.BlockSpec(wo.shape, const), pl.BlockSpec(g.shape, const), pl.BlockSpec(b.shape, const),
                  pl.BlockSpec(wr_t.shape, const), pl.BlockSpec(br.shape, const),
                  pl.BlockSpec(before.shape, const), pl.BlockSpec(below.shape, const), pl.BlockSpec(eye.shape, const)],
        out_specs=[pl.BlockSpec((tm, d), row), pl.BlockSpec((tm, d), row),
                   pl.BlockSpec((2 * TOP_K, tm), lambda i: (0, i)),
                   pl.BlockSpec((nblk, 1, LANES), lambda i: (i, 0, 0))],
        out_shape=[jax.ShapeDtypeStruct((t, d), F32), jax.ShapeDtypeStruct((t, d), BF16),
                   jax.ShapeDtypeStruct((2 * TOP_K, t), F32),
                   jax.ShapeDtypeStruct((t // MOE_BLOCK, 1, LANES), F32)],
        compiler_params=_cparams(("parallel",)),
        name="mix",
    )(od, om, x, wo, g, b, wr_t, br, before, below, eye)


def _moe_plan(cnt, n_tiles, trash_row):
    seg = cnt.astype(jnp.int32)
    nb = seg.shape[0]
    before_blocks = jnp.cumsum(seg, axis=0) - seg
    length = jnp.sum(seg, axis=0)
    padded = -(-length // FFN_TILE) * FFN_TILE
    ends = jnp.cumsum(padded)
    start = ends - padded
    dst = start[None, :] + before_blocks
    off = (jnp.cumsum(seg, axis=1) - seg) // SEG_ALIGN
    tot = jnp.sum(seg, axis=1) // SEG_ALIGN
    chunk = jnp.arange(BLOCK_CHUNKS, dtype=jnp.int32)
    owner = jnp.sum((off[:, None, :] <= chunk[None, :, None]).astype(jnp.int32), axis=2) - 1
    row = (jnp.take_along_axis(dst, owner, axis=1)
           + SEG_ALIGN * (chunk[None, :] - jnp.take_along_axis(off, owner, axis=1)))
    valid = chunk[None, :] < tot[:, None]
    trash = trash_row + ((jnp.arange(nb, dtype=jnp.int32) % 3) * BLOCK_ROWS)[:, None] + SEG_ALIGN * chunk[None, :]
    first = trash_row + 2 * BLOCK_ROWS + SEG_ALIGN * chunk[None, :]
    put = jnp.concatenate([first, jnp.where(valid, row, trash)], axis=0).reshape(-1)
    get = jnp.concatenate([jnp.where(valid, row, 0), jnp.zeros((1, BLOCK_CHUNKS), jnp.int32)], axis=0).reshape(-1)
    n_used = ends[-1] // FFN_TILE
    tile_start = jnp.arange(n_tiles, dtype=jnp.int32) * FFN_TILE
    tile_e = jnp.sum((ends[None, :] <= tile_start[:, None]).astype(jnp.int32), axis=1)
    last_e = tile_e[jnp.maximum(n_used - 1, 0)]
    tile_e = jnp.where(jnp.arange(n_tiles) < n_used, tile_e, last_e)
    tail_start = start + length
    tail_nch = (padded - length) // SEG_ALIGN
    return dict(put=put, get=get, n_used=n_used.reshape(1), tile_e=tile_e, tail_start=tail_start,
                tail_nch=tail_nch, tail_tot=jnp.sum(tail_nch).reshape(1))


def _drain(copy, n):
    def body(i, c):
        copy.wait()
        return c
    lax.fori_loop(0, n, body, 0)


def _order_rows(sk, j0, vals):
    rows = lax.broadcasted_iota(jnp.int32, (ORD_SUB, MOE_BLOCK), 0).astype(F32).astype(BF16)
    out = jnp.zeros((ORD_SUB, MOE_BLOCK), BF16)
    for k in range(TOP_K):
        rel = (sk[k:k + 1, :] - j0).astype(BF16)
        out = jnp.where(rows == rel, vals[k], out)
    return out


def _dispatch_kernel(put_ref, tstart_ref, tnch_ref, ttot_ref, x_ref, sk_ref, xs_ref, buf_ref, zero_ref, sem, tail_sem):
    b = pl.program_id(0)
    last = pl.num_programs(0) - 1
    slot = lax.rem(b, 3)
    send = lax.rem(b + 2, 3)

    def whole(s):
        return pltpu.make_async_copy(buf_ref.at[s], xs_ref.at[pl.ds(0, BLOCK_ROWS), :], sem.at[s])

    @pl.when(b == 0)
    def _():
        buf_ref[...] = jnp.zeros(buf_ref.shape, BF16)

    @pl.when(b >= 2)
    def _():
        whole(slot).wait()

    x = x_ref[...]
    sk = sk_ref[0:TOP_K, :]
    ones = [jnp.ones((1, MOE_BLOCK), BF16)] * TOP_K
    base = b * BLOCK_CHUNKS
    for sub in range(BLOCK_ROWS // DISP_SUB):
        j0 = sub * DISP_SUB
        sel = jnp.concatenate([_order_rows(sk, float(j0 + i * ORD_SUB), ones) for i in range(DISP_SUB // ORD_SUB)],
                              axis=0)
        buf_ref[slot, j0:j0 + DISP_SUB, :] = jnp.dot(sel, x, preferred_element_type=F32).astype(BF16)
        for c in range(sub * (DISP_SUB // SEG_ALIGN), (sub + 1) * (DISP_SUB // SEG_ALIGN)):
            pltpu.make_async_copy(
                buf_ref.at[send, c * SEG_ALIGN:(c + 1) * SEG_ALIGN, :],
                xs_ref.at[pl.ds(pl.multiple_of(put_ref[base + c], SEG_ALIGN), SEG_ALIGN), :], sem.at[send]).start()

    @pl.when(b == last)
    def _():
        whole(send).wait()

        @pl.when(b >= 1)
        def _():
            whole(lax.rem(b + 1, 3)).wait()

        zero_ref[...] = jnp.zeros(zero_ref.shape, BF16)

        def tail_copy(dst_row):
            return pltpu.make_async_copy(zero_ref, xs_ref.at[pl.ds(pl.multiple_of(dst_row, SEG_ALIGN), SEG_ALIGN), :],
                                         tail_sem.at[0])

        def tail_expert(e, carry):
            def tail_chunk(c, carry2):
                tail_copy(tstart_ref[e] + c * SEG_ALIGN).start()
                return carry2
            lax.fori_loop(0, tnch_ref[e], tail_chunk, 0)
            return carry

        lax.fori_loop(0, N_EXPERTS, tail_expert, 0)
        _drain(tail_copy(0), ttot_ref[0])


def _dispatch(plan, x1b, sk, n_rows):
    t, d = x1b.shape
    nb = t // MOE_BLOCK
    grid_spec = pltpu.PrefetchScalarGridSpec(
        num_scalar_prefetch=4,
        grid=(nb + 1,),
        in_specs=[pl.BlockSpec((MOE_BLOCK, d), lambda b, *_: (jnp.minimum(b, nb - 1), 0)),
                  pl.BlockSpec((2 * TOP_K, MOE_BLOCK), lambda b, *_: (0, jnp.minimum(b, nb - 1)))],
        out_specs=pl.BlockSpec(memory_space=pl.ANY),
        scratch_shapes=[pltpu.VMEM((3, BLOCK_ROWS, d), BF16), pltpu.VMEM((SEG_ALIGN, d), BF16),
                        pltpu.SemaphoreType.DMA((3,)), pltpu.SemaphoreType.DMA((1,))],
    )
    return pl.pallas_call(
        _dispatch_kernel,
        grid_spec=grid_spec,
        out_shape=jax.ShapeDtypeStruct((n_rows, d), BF16),
        compiler_params=_cparams(("arbitrary",)),
        name="dispatch",
    )(plan["put"], plan["tail_start"], plan["tail_nch"], plan["tail_tot"], x1b, sk)


def _ffn_kernel(te_ref, nu_ref, xs_ref, wg_ref, wu_ref, wd_ref, y_ref):
    @pl.when(pl.program_id(0) < nu_ref[0])
    def _():
        x = xs_ref[...]
        hg = jnp.dot(x, wg_ref[...], preferred_element_type=F32)
        hu = jnp.dot(x, wu_ref[...], preferred_element_type=F32)
        h = hg * (1.0 / (1.0 + jnp.exp(-hg))) * hu
        y_ref[...] = jnp.dot(h.astype(BF16), wd_ref[...], preferred_element_type=F32).astype(BF16)


def _ffn(plan, xs, wg, wu, wd, n_tiles):
    d = xs.shape[1]
    used = lambda i, te, nu: (jnp.minimum(i, nu[0] - 1), 0)
    wsel = lambda i, te, nu: (te[i], 0, 0)
    grid_spec = pltpu.PrefetchScalarGridSpec(
        num_scalar_prefetch=2,
        grid=(n_tiles,),
        in_specs=[pl.BlockSpec((FFN_TILE, d), used),
                  pl.BlockSpec((None, d, EXPERT_DIM), wsel), pl.BlockSpec((None, d, EXPERT_DIM), wsel),
                  pl.BlockSpec((None, EXPERT_DIM, d), wsel)],
        out_specs=pl.BlockSpec((FFN_TILE, d), used),
    )
    return pl.pallas_call(
        _ffn_kernel,
        grid_spec=grid_spec,
        out_shape=jax.ShapeDtypeStruct((n_tiles * FFN_TILE, d), BF16),
        compiler_params=_cparams(("arbitrary",)),
        name="ffn",
    )(plan["tile_e"], plan["n_used"], xs, wg, wu, wd)


def _combine_kernel(get_ref, y_ref, sk_ref, x1_ref, x1b_ref, wsg_ref, wsu_ref, wsd_ref,
                    g_ref, b_ref, o_ref, buf_ref, sem, *, alpha):
    b = pl.program_id(0)
    slot = lax.rem(b, 2)
    nxt = 1 - slot

    def whole(s):
        return pltpu.make_async_copy(y_ref.at[pl.ds(0, BLOCK_ROWS), :], buf_ref.at[s], sem.at[s])

    def chunk_copy(s, c, src_row):
        return pltpu.make_async_copy(y_ref.at[pl.ds(pl.multiple_of(src_row, SEG_ALIGN), SEG_ALIGN), :],
                                     buf_ref.at[s, pl.ds(pl.multiple_of(c * SEG_ALIGN, SEG_ALIGN), SEG_ALIGN), :],
                                     sem.at[s])

    @pl.when(b == 0)
    def _():
        def first(c, carry):
            chunk_copy(0, c, get_ref[c]).start()
            return carry
        lax.fori_loop(0, BLOCK_CHUNKS, first, 0)

    whole(slot).wait()

    sk = sk_ref[0:TOP_K, :]
    gates = [sk_ref[TOP_K + k:TOP_K + k + 1, :].astype(BF16) for k in range(TOP_K)]
    base = (b + 1) * BLOCK_CHUNKS
    acc = None
    for sub in range(BLOCK_ROWS // DISP_SUB):
        j0 = sub * DISP_SUB
        w = jnp.concatenate([_order_rows(sk, float(j0 + i * ORD_SUB), gates) for i in range(DISP_SUB // ORD_SUB)],
                            axis=0)
        part = lax.dot_general(w, buf_ref[slot, j0:j0 + DISP_SUB, :], (((0,), (0,)), ((), ())),
                               preferred_element_type=F32)
        acc = part if acc is None else acc + part
        for c in range(sub * (DISP_SUB // SEG_ALIGN), (sub + 1) * (DISP_SUB // SEG_ALIGN)):
            pltpu.make_async_copy(
                y_ref.at[pl.ds(pl.multiple_of(get_ref[base + c], SEG_ALIGN), SEG_ALIGN), :],
                buf_ref.at[nxt, c * SEG_ALIGN:(c + 1) * SEG_ALIGN, :], sem.at[nxt]).start()

    xb = x1b_ref[...]
    hg = jnp.dot(xb, wsg_ref[...], preferred_element_type=F32)
    hu = jnp.dot(xb, wsu_ref[...], preferred_element_type=F32)
    h = hg * (1.0 / (1.0 + jnp.exp(-hg))) * hu
    acc = acc + jnp.dot(h.astype(BF16), wsd_ref[...], preferred_element_type=F32)
    o_ref[...] = _layer_norm(alpha * x1_ref[...] + acc, g_ref[...], b_ref[...])

    @pl.when(b == pl.num_programs(0) - 1)
    def _():
        whole(nxt).wait()


def _combine(plan, y, sk, x1, x1b, wsg, wsu, wsd, g, b, alpha):
    t, d = x1.shape
    nb = t // MOE_BLOCK
    row = lambda i, *_: (i, 0)
    const = lambda i, *_: (0, 0)
    grid_spec = pltpu.PrefetchScalarGridSpec(
        num_scalar_prefetch=1,
        grid=(nb,),
        in_specs=[pl.BlockSpec(memory_space=pl.ANY), pl.BlockSpec((2 * TOP_K, MOE_BLOCK), lambda i, *_: (0, i)),
                  pl.BlockSpec((MOE_BLOCK, d), row), pl.BlockSpec((MOE_BLOCK, d), row),
                  pl.BlockSpec(wsg.shape, const), pl.BlockSpec(wsu.shape, const), pl.BlockSpec(wsd.shape, const),
                  pl.BlockSpec(g.shape, const), pl.BlockSpec(b.shape, const)],
        out_specs=pl.BlockSpec((MOE_BLOCK, d), row),
        scratch_shapes=[pltpu.VMEM((2, BLOCK_ROWS, d), BF16), pltpu.SemaphoreType.DMA((2,))],
    )
    return pl.pallas_call(
        functools.partial(_combine_kernel, alpha=alpha),
        grid_spec=grid_spec,
        out_shape=jax.ShapeDtypeStruct((t, d), F32),
        compiler_params=_cparams(("arbitrary",)),
        name="combine",
    )(plan["get"], y, sk, x1, x1b, wsg, wsu, wsd, g, b)


def _moe(x1, x1b, sk, cnt, wg, wu, wd, wsg, wsu, wsd, g, b, alpha):
    t = x1.shape[0]
    nb = t // MOE_BLOCK
    n_tiles = -(-(nb * BLOCK_ROWS) // FFN_TILE) + N_EXPERTS
    n_rows = n_tiles * FFN_TILE + 3 * BLOCK_ROWS
    plan = _moe_plan(cnt[:, 0, :N_EXPERTS], n_tiles, n_tiles * FFN_TILE)
    xs = _dispatch(plan, x1b, sk, n_rows)
    y = _ffn(plan, xs, wg, wu, wd, n_tiles)
    return _combine(plan, y, sk, x1, x1b, wsg, wsu, wsd, g, b, alpha)


def _rope_cs(pos, dim):
    inv = ROPE_THETA ** (-jnp.arange(0, dim, 2, dtype=F32) / dim)
    ang = pos.astype(F32)[:, None] * inv[None, :]
    return jnp.cos(ang), jnp.sin(ang)


def _tables(pos, reps):
    n = pos.shape[0]
    c32, s32 = _rope_cs(pos, DIFF_QK)
    c16, s16 = _rope_cs(pos, MLA_ROPE)
    one = lambda w: jnp.ones((n, w), F32)
    zero = lambda w: jnp.zeros((n, w), F32)
    cd = jnp.concatenate([c32] * 4, axis=1)
    sd = jnp.concatenate([-s32, s32] * 2, axis=1)
    cq = jnp.concatenate([one(MLA_NOPE), c16, c16, one(32)], axis=1)
    sq = jnp.concatenate([zero(MLA_NOPE), -s16, s16, zero(32)], axis=1)
    ck = jnp.concatenate([c16, c16, zero(96)], axis=1)
    sk = jnp.concatenate([-s16, s16, zero(96)], axis=1)
    return tuple(jnp.tile(a, (reps, 1)) for a in (cd, sd, cq, sq, ck, sk))


def kernel(x_prompt, x_sample, cache_diff_k, cache_diff_v, cache_mla_ckv, cache_mla_kpe, w_in, diff_lambda, diff_subln_g, mla_q_norm_g, mla_w_uq, mla_kv_norm_g, mla_w_ukv, w_out, ln1_g, ln1_b, w_router, b_router, w_exp_gate, w_exp_up, w_exp_down, w_sh_gate, w_sh_up, w_sh_down, ln2_g, ln2_b):
    depth = w_in.shape[0]
    assert depth == 1
    d_model = x_prompt.shape[-1]
    alpha = (2.0 * depth) ** 0.25
    past_len = cache_diff_k.shape[2]
    layer = 0
    lambda_init = 0.8 - 0.6 * math.exp(-0.3 * layer)

    w_in_b = jnp.pad(w_in[layer], ((0, 0), (0, IN_PAD - IN_WIDTH))).astype(BF16)
    wuq = jnp.pad(mla_w_uq[layer], ((0, 0), (0, 0), (0, LANES - MLA_NOPE - MLA_ROPE)))
    wuq = wuq.reshape(Q_LORA, HEAD_W).astype(BF16)
    wukv = mla_w_ukv[layer]
    wuk = jnp.pad(wukv[:, :, :MLA_NOPE], ((0, 0), (0, 0), (0, LANES - MLA_NOPE))).reshape(KV_LORA, HEAD_W).astype(BF16)
    wuv = wukv[:, :, MLA_NOPE:].reshape(KV_LORA, HEAD_W).astype(BF16)
    place = jnp.pad(jnp.eye(MLA_ROPE, dtype=F32), ((0, 0), (MLA_NOPE, LANES - MLA_NOPE - MLA_ROPE)))
    place = jnp.tile(place, (1, MLA_HEADS)).astype(BF16)
    gq = mla_q_norm_g[layer].reshape(1, Q_LORA)
    gkv = mla_kv_norm_g[layer].reshape(1, KV_LORA)
    gsub = diff_subln_g[layer].reshape(1, LANES)
    wo = w_out[layer].astype(BF16)
    g1, b1 = ln1_g[layer].reshape(1, d_model), ln1_b[layer].reshape(1, d_model)
    g2, b2 = ln2_g[layer].reshape(1, d_model), ln2_b[layer].reshape(1, d_model)
    wr_t = w_router[layer].T
    br = b_router[layer].reshape(N_EXPERTS, 1)
    wg, wu, wd = (w[layer].astype(BF16) for w in (w_exp_gate, w_exp_up, w_exp_down))
    wsg, wsu, wsd = (w[layer].astype(BF16) for w in (w_sh_gate, w_sh_up, w_sh_down))
    lp = diff_lambda[layer].astype(F32)
    lam = jnp.exp(jnp.sum(lp[0] * lp[1])) - jnp.exp(jnp.sum(lp[2] * lp[3])) + lambda_init
    lam = lam.reshape(1, 1)

    def group(x, pos, past, tm, tq, tk, tkm):
        b, s, _ = x.shape
        t = b * s
        tm = min(tm, t)
        xf = x.reshape(t, d_model)
        reps = max(1, tm // s)
        tables = _tables(pos, reps)
        n_pat = (s * reps) // tm
        qd, kd32, kd16, vd32, vd16, ckv, kpe, qm = _proj(xf, w_in_b, wuq, gq, gkv, tables, tm, n_pat)
        if past is None:
            k_d, v_d = kd16.reshape(b, s, HEAD_W), vd16.reshape(b, s, HEAD_W)
            ckv_all, kpe_all = ckv, kpe
            sk, q_pos0 = s, 0
        else:
            pk, pv, pc, pp = past
            sk = -(-(past_len + s) // tk) * tk
            padr = sk - past_len - s
            cat = lambda old, new: jnp.pad(jnp.concatenate([old, new], axis=1), ((0, 0), (0, padr), (0, 0)))
            k_d = cat(pk.reshape(b, past_len, HEAD_W).astype(BF16), kd16.reshape(b, s, HEAD_W))
            v_d = cat(pv.reshape(b, past_len, HEAD_W).astype(BF16), vd16.reshape(b, s, HEAD_W))
            ckv_all = cat(pc, ckv.reshape(b, s, KV_LORA)).reshape(b * sk, KV_LORA)
            kpe_all = cat(pp, kpe.reshape(b, s, MLA_ROPE)).reshape(b * sk, MLA_ROPE)
            q_pos0 = past_len
        k_m, v_m = _kvup(ckv_all, kpe_all, wuk, place, wuv, min(1024, ckv_all.shape[0]))
        o_d = _flash(qd.reshape(b, s, HEAD_W), k_d, v_d, lam, gsub, n_comp=2, tq=tq, tk=tk, tkm=tkm,
                     q_pos0=q_pos0, out_scale=1.0 - lambda_init)
        o_m = _flash(qm.reshape(b, s, HEAD_W), k_m.reshape(b, sk, HEAD_W), v_m.reshape(b, sk, HEAD_W), lam, gsub,
                     n_comp=1, tq=tq, tk=tk, tkm=tkm, q_pos0=q_pos0, out_scale=1.0)
        x1, x1b, slots, cnt = _mix(o_d.reshape(t, HEAD_W), o_m.reshape(t, HEAD_W), xf, wo, g1, b1, wr_t, br,
                                   tm, alpha)
        y = _moe(x1, x1b, slots, cnt, wg, wu, wd, wsg, wsu, wsd, g2, b2, alpha)
        rows = (kd32.reshape(1, b, s, DIFF_HEADS, LANES), vd32.reshape(1, b, s, DIFF_HEADS, LANES),
                ckv.reshape(1, b, s, KV_LORA), kpe.reshape(1, b, s, MLA_ROPE))
        return y.reshape(b, s, d_model), rows

    s_p = x_prompt.shape[1]
    s_s = x_sample.shape[1]
    pos_p = jnp.arange(s_p, dtype=jnp.int32)
    pos_s = past_len + jnp.arange(s_s, dtype=jnp.int32)
    y_p, r_p = group(x_prompt, pos_p, None, 512, 256, 512, 256)
    past = (cache_diff_k[layer], cache_diff_v[layer], cache_mla_ckv[layer], cache_mla_kpe[layer])
    y_s, r_s = group(x_sample, pos_s, past, 512, s_s, 512, 256)
    return (y_p, y_s) + r_p + r_s
```

```python
import functools
import math

import jax
import jax.numpy as jnp
from jax import lax
from jax.experimental import pallas as pl
from jax.experimental.pallas import tpu as pltpu

F32 = jnp.float32
BF16 = jnp.bfloat16

LANES = 128
VMEM_LIMIT = 52 * 1024 * 1024

CHUNK = 64
CHUNK_SHIFT = 6
ROPE_THETA = 10000.0
LN_EPS = 1e-5
RMS_EPS = 1e-6
LOG2E = 1.4426950408889634
NEG_BIG = -1e30
FLASH_SUB = 128

DIFF_HEADS = 4
DIFF_QK = 64
MLA_HEADS = 4
MLA_NOPE = 64
MLA_ROPE = 32
MLA_V = 128
Q_LORA = 384
KV_LORA = 256
N_EXPERTS = 64
N_GROUPS = 8
GROUP_SIZE = N_EXPERTS // N_GROUPS
TOPK_GROUPS = 4
TOP_K = 8
ROUTED_SCALE = 2.5
EXPERT_DIM = 256

MOE_BLOCK = 256
SEG_ALIGN = 16
ORD_SUB = 256
DISP_SUB = 1024
BLOCK_ROWS = -(-(TOP_K * MOE_BLOCK + N_EXPERTS * (SEG_ALIGN - 1)) // DISP_SUB) * DISP_SUB
BLOCK_CHUNKS = BLOCK_ROWS // SEG_ALIGN
FFN_TILE = 1024

DQ_W = DIFF_HEADS * 2 * DIFF_QK
HEAD_W = DIFF_HEADS * LANES
IN_WIDTH = 3 * DQ_W + Q_LORA + KV_LORA + MLA_ROPE
IN_PAD = 2304
OFF_DK, OFF_DV, OFF_CQ, OFF_CKV, OFF_KPE = 512, 1024, 1536, 1920, 2176


def _cparams(sem):
    return pltpu.CompilerParams(dimension_semantics=sem, vmem_limit_bytes=VMEM_LIMIT)


def _rms(x, g):
    return x * lax.rsqrt(jnp.mean(x * x, axis=-1, keepdims=True) + RMS_EPS) * g


def _layer_norm(x, g, b):
    mu = jnp.mean(x, axis=-1, keepdims=True)
    xc = x - mu
    var = jnp.mean(xc * xc, axis=-1, keepdims=True)
    return xc * lax.rsqrt(var + LN_EPS) * g + b


def _proj_kernel(x_ref, w_ref, wuq_ref, gq_ref, gkv_ref, cd_ref, sd_ref, cq_ref, sq_ref, ck_ref, sk_ref,
                 qd_ref, kd32_ref, kd16_ref, vd32_ref, vd16_ref, ckv_ref, kpe_ref, qm_ref,
                 *, scale_d, scale_m):
    tm = x_ref.shape[0]
    x = x_ref[...].astype(BF16)
    proj = jnp.dot(x, w_ref[...], preferred_element_type=F32)
    lane = lax.broadcasted_iota(jnp.int32, (tm, LANES), 1)

    first_d = (lane & 63) < 32
    cd = cd_ref[...]
    sd = sd_ref[...]

    def rope_d(blk):
        rot = jnp.where(first_d, pltpu.roll(blk, LANES - 32, 1), pltpu.roll(blk, 32, 1))
        return blk * cd + rot * sd

    for j in range(DIFF_HEADS):
        sl = slice(j * LANES, (j + 1) * LANES)
        qd_ref[:, sl] = (rope_d(proj[:, sl]) * scale_d).astype(BF16)
        kr = rope_d(proj[:, OFF_DK + j * LANES:OFF_DK + (j + 1) * LANES])
        kd32_ref[:, j, :] = kr
        kd16_ref[:, sl] = kr.astype(BF16)
        dv = proj[:, OFF_DV + j * LANES:OFF_DV + (j + 1) * LANES]
        vd32_ref[:, j, :] = dv
        vd16_ref[:, sl] = dv.astype(BF16)

    cqn = _rms(proj[:, OFF_CQ:OFF_CQ + Q_LORA], gq_ref[...])
    q = jnp.dot(cqn.astype(BF16), wuq_ref[...], preferred_element_type=F32)
    cq = cq_ref[...]
    sq = sq_ref[...]
    first_q = lane < (MLA_NOPE + MLA_ROPE // 2)
    for h in range(MLA_HEADS):
        sl = slice(h * LANES, (h + 1) * LANES)
        blk = q[:, sl]
        rot = jnp.where(first_q, pltpu.roll(blk, LANES - 16, 1), pltpu.roll(blk, 16, 1))
        qm_ref[:, sl] = ((blk * cq + rot * sq) * scale_m).astype(BF16)

    ckv_ref[...] = _rms(proj[:, OFF_CKV:OFF_CKV + KV_LORA], gkv_ref[...])

    kb = proj[:, OFF_KPE:OFF_KPE + LANES]
    rot = jnp.where(lane < 16, pltpu.roll(kb, LANES - 16, 1), pltpu.roll(kb, 16, 1))
    kpe_ref[...] = (kb * ck_ref[...] + rot * sk_ref[...])[:, :MLA_ROPE]


def _proj(x, w_in, wuq, gq, gkv, tables, tm, n_pat):
    t = x.shape[0]
    row = lambda i: (i, 0)
    const = lambda i: (0, 0)
    pat = lambda i: (i % n_pat, 0)
    tab_spec = pl.BlockSpec((tm, LANES), pat)
    out_w = lambda w, dt: jax.ShapeDtypeStruct((t, w), dt)
    cache_spec = pl.BlockSpec((tm, DIFF_HEADS, LANES), lambda i: (i, 0, 0))
    cache_shape = jax.ShapeDtypeStruct((t, DIFF_HEADS, LANES), F32)
    return pl.pallas_call(
        functools.partial(_proj_kernel, scale_d=LOG2E * DIFF_QK ** -0.5,
                          scale_m=LOG2E * (MLA_NOPE + MLA_ROPE) ** -0.5),
        grid=(t // tm,),
        in_specs=[pl.BlockSpec((tm, x.shape[1]), row),
                  pl.BlockSpec(w_in.shape, const), pl.BlockSpec(wuq.shape, const),
                  pl.BlockSpec(gq.shape, const), pl.BlockSpec(gkv.shape, const)] + [tab_spec] * 6,
        out_specs=[pl.BlockSpec((tm, HEAD_W), row), cache_spec, pl.BlockSpec((tm, HEAD_W), row), cache_spec,
                   pl.BlockSpec((tm, HEAD_W), row), pl.BlockSpec((tm, KV_LORA), row),
                   pl.BlockSpec((tm, MLA_ROPE), row), pl.BlockSpec((tm, HEAD_W), row)],
        out_shape=[out_w(HEAD_W, BF16), cache_shape, out_w(HEAD_W, BF16), cache_shape,
                   out_w(HEAD_W, BF16), out_w(KV_LORA, F32), out_w(MLA_ROPE, F32), out_w(HEAD_W, BF16)],
        compiler_params=_cparams(("parallel",)),
        name="proj",
    )(x, w_in, wuq, gq, gkv, *tables)


def _kvup_kernel(ckv_ref, kpe_ref, wuk_ref, place_ref, wuv_ref, k_ref, v_ref):
    c = ckv_ref[...].astype(BF16)
    k = jnp.dot(c, wuk_ref[...], preferred_element_type=F32)
    k = k + jnp.dot(kpe_ref[...].astype(BF16), place_ref[...], preferred_element_type=F32)
    k_ref[...] = k.astype(BF16)
    v_ref[...] = jnp.dot(c, wuv_ref[...], preferred_element_type=F32).astype(BF16)


def _kvup(ckv, kpe, wuk, place, wuv, tm):
    r = ckv.shape[0]
    row = lambda i: (i, 0)
    const = lambda i: (0, 0)
    return pl.pallas_call(
        _kvup_kernel,
        grid=(r // tm,),
        in_specs=[pl.BlockSpec((tm, KV_LORA), row), pl.BlockSpec((tm, MLA_ROPE), row),
                  pl.BlockSpec(wuk.shape, const), pl.BlockSpec(place.shape, const), pl.BlockSpec(wuv.shape, const)],
        out_specs=[pl.BlockSpec((tm, HEAD_W), row)] * 2,
        out_shape=[jax.ShapeDtypeStruct((r, HEAD_W), BF16)] * 2,
        compiler_params=_cparams(("parallel",)),
        name="kvup",
    )(ckv, kpe, wuk, place, wuv)


def _flash_kernel(lam_ref, q_ref, k_ref, v_ref, g_ref, o_ref, qs_ref, vx_ref, m_ref, acc_ref,
                  *, n_comp, tq, tk, tkm, sub, sk, q_pos0, out_scale, static_diag):
    qi = pl.program_id(2)
    rows = n_comp * tq

    @pl.when(qi == 0)
    def _():
        vx_ref[:, 0:LANES] = v_ref[...]
        vx_ref[:, LANES:2 * LANES] = jnp.ones((sk, LANES), BF16)

    q = q_ref[...]
    if n_comp == 2:
        lane = lax.broadcasted_iota(jnp.int32, (tq, LANES), 1)
        zero = jnp.zeros_like(q)
        qs_ref[0:tq, :] = jnp.where(lane < DIFF_QK, q, zero)
        qs_ref[tq:rows, :] = jnp.where(lane >= DIFF_QK, q, zero)
    else:
        qs_ref[...] = q
    m_ref[...] = jnp.full(m_ref.shape, NEG_BIG, F32)
    acc_ref[...] = jnp.zeros(acc_ref.shape, F32)

    q0 = q_pos0 + qi * tq
    lo_vis = jnp.minimum(((q0 >> CHUNK_SHIFT) + 1) << CHUNK_SHIFT, sk)
    hi_vis = jnp.minimum((((q0 + tq - 1) >> CHUNK_SHIFT) + 1) << CHUNK_SHIFT, sk)
    n_full = lo_vis // tk
    m_lo = n_full * (tk // tkm)
    m_hi = (hi_vis + tkm - 1) // tkm

    def sub_step(start, width, r0, masked):
        rs = slice(r0, r0 + sub)
        k = k_ref[pl.ds(start, width), :]
        vx = vx_ref[pl.ds(start, width), :]
        s = lax.dot_general(qs_ref[rs, :], k, (((1,), (1,)), ((), ())), preferred_element_type=F32)
        if masked:
            r = (lax.broadcasted_iota(jnp.int32, (sub, width), 0) + r0) & (tq - 1)
            c = lax.broadcasted_iota(jnp.int32, (sub, width), 1)
            ok = ((start + c) >> CHUNK_SHIFT) <= ((q0 + r) >> CHUNK_SHIFT)
            s = jnp.where(ok, s, NEG_BIG)
        m_prev = m_ref[rs, :]
        m_new = jnp.maximum(m_prev, jnp.max(s, axis=-1, keepdims=True))
        alpha = jnp.exp2(m_prev - m_new)
        p = jnp.exp2(s - jnp.concatenate([m_new] * (width // LANES), axis=1))
        pv = jnp.dot(p.astype(BF16), vx, preferred_element_type=F32)
        acc_ref[rs, :] = jnp.concatenate([alpha, alpha], axis=1) * acc_ref[rs, :] + pv
        m_ref[rs, :] = m_new

    def step(start, width, masked):
        for r0 in range(0, rows, sub):
            sub_step(start, width, r0, masked)

    def full_body(j, carry):
        step(pl.multiple_of(j * tk, tk), tk, False)
        return carry

    def masked_body(j, carry):
        step(pl.multiple_of(j * tkm, tkm), tkm, True)
        return carry

    lax.fori_loop(0, n_full, full_body, 0)
    if static_diag:
        for c in range(tq // tkm):
            start = pl.multiple_of(q0 + c * tkm, tkm)
            first_kc, last_kc = (c * tkm) // CHUNK, (c * tkm + tkm - 1) // CHUNK
            for r0 in range(0, rows, sub):
                first_qc, last_qc = (r0 % tq) // CHUNK, (r0 % tq + sub - 1) // CHUNK
                if first_kc > last_qc:
                    continue
                sub_step(start, tkm, r0, masked=last_kc > first_qc)
    else:
        lax.fori_loop(m_lo, m_hi, masked_body, 0)

    o = acc_ref[:, 0:LANES] / acc_ref[:, LANES:2 * LANES]
    if n_comp == 2:
        o = o[0:tq, :] - lam_ref[0, 0] * o[tq:rows, :]
        o = _rms(o, g_ref[...]) * out_scale
    o_ref[...] = o.astype(BF16)


def _flash(q, k, v, lam, g, *, n_comp, tq, tk, tkm, q_pos0, out_scale):
    b, sq, _ = q.shape
    sk = k.shape[1]
    rows = n_comp * tq
    sub = min(rows, FLASH_SUB)
    assert sq % tq == 0 and sk % tk == 0 and tk % tkm == 0 and tq & (tq - 1) == 0 and rows % sub == 0
    static_diag = (q_pos0 % tq == 0 and tq % tk == 0 and q_pos0 + sq <= sk and sub % CHUNK == 0
                   and tkm % CHUNK == 0)
    return pl.pallas_call(
        functools.partial(_flash_kernel, n_comp=n_comp, tq=tq, tk=tk, tkm=tkm, sub=sub, sk=sk, q_pos0=q_pos0,
                          out_scale=out_scale, static_diag=static_diag),
        grid=(b, DIFF_HEADS, sq // tq),
        in_specs=[pl.BlockSpec(memory_space=pltpu.SMEM),
                  pl.BlockSpec((None, tq, LANES), lambda bi, h, i: (bi, i, h)),
                  pl.BlockSpec((None, sk, LANES), lambda bi, h, i: (bi, 0, h)),
                  pl.BlockSpec((None, sk, LANES), lambda bi, h, i: (bi, 0, h)),
                  pl.BlockSpec((1, LANES), lambda bi, h, i: (0, 0))],
        out_specs=pl.BlockSpec((None, tq, LANES), lambda bi, h, i: (bi, i, h)),
        out_shape=jax.ShapeDtypeStruct((b, sq, HEAD_W), BF16),
        scratch_shapes=[pltpu.VMEM((rows, LANES), BF16), pltpu.VMEM((sk, 2 * LANES), BF16),
                        pltpu.VMEM((rows, LANES), F32), pltpu.VMEM((rows, 2 * LANES), F32)],
        compiler_params=_cparams(("parallel", "parallel", "arbitrary")),
        name="flash_diff" if n_comp == 2 else "flash_mla",
    )(lam, q, k, v, g)


def _sublane_max(x):
    return jnp.max(x, axis=0, keepdims=True)


def _sublane_min(x):
    return jnp.min(x, axis=0, keepdims=True)


def _route_t(scores, biased):
    tm = scores[0].shape[1]
    sub = lax.broadcasted_iota(jnp.int32, (GROUP_SIZE, tm), 0)
    neg_inf = jnp.float32(-jnp.inf)
    gs = jnp.zeros((N_GROUPS, tm), F32)
    for g in range(N_GROUPS):
        bg = biased[g]
        m1 = _sublane_max(bg)
        i1 = _sublane_min(jnp.where(bg == m1, sub, GROUP_SIZE))
        m2 = _sublane_max(jnp.where(sub == i1, neg_inf, bg))
        gs = jnp.where(sub == g, m1 + m2, gs)
    keep = jnp.zeros((N_GROUPS, tm), jnp.bool_)
    cur = gs
    for _ in range(TOPK_GROUPS):
        mx = _sublane_max(cur)
        fi = _sublane_min(jnp.where(cur == mx, sub, N_GROUPS))
        hit = sub == fi
        keep = jnp.logical_or(keep, hit)
        cur = jnp.where(hit, neg_inf, cur)
    keep_f = jnp.where(keep, 1.0, 0.0)
    cand = []
    for g in range(N_GROUPS):
        kg = _sublane_max(jnp.where(sub == g, keep_f, 0.0)) > 0.5
        cand.append(jnp.where(kg, biased[g], neg_inf))
    chosen = [jnp.zeros((GROUP_SIZE, tm), jnp.bool_) for _ in range(N_GROUPS)]
    picks = []
    for _ in range(TOP_K):
        mx = cand[0]
        for g in range(1, N_GROUPS):
            mx = jnp.maximum(mx, cand[g])
        mx = _sublane_max(mx)
        fi = jnp.where(cand[0] == mx, sub, N_EXPERTS)
        for g in range(1, N_GROUPS):
            fi = jnp.minimum(fi, jnp.where(cand[g] == mx, sub + g * GROUP_SIZE, N_EXPERTS))
        fi = _sublane_min(fi)
        picks.append(fi)
        for g in range(N_GROUPS):
            hit = (sub + g * GROUP_SIZE) == fi
            chosen[g] = jnp.logical_or(chosen[g], hit)
            cand[g] = jnp.where(hit, neg_inf, cand[g])
    w = [jnp.where(chosen[g], scores[g], 0.0) for g in range(N_GROUPS)]
    tot = w[0]
    for g in range(1, N_GROUPS):
        tot = tot + w[g]
    tot = jnp.sum(tot, axis=0, keepdims=True)
    return [wg / tot * ROUTED_SCALE for wg in w], chosen, picks


def _mix_kernel(od_ref, om_ref, x_ref, wo_ref, g_ref, b_ref, wr_ref, br_ref, before_ref, below_ref, eye_ref,
                x1_ref, x1b_ref, sk_ref, cnt_ref, *, alpha):
    mix = jnp.dot(od_ref[...], wo_ref[0:HEAD_W, :], preferred_element_type=F32)
    mix = mix + jnp.dot(om_ref[...], wo_ref[HEAD_W:2 * HEAD_W, :], preferred_element_type=F32)
    x1 = _layer_norm(alpha * x_ref[...] + mix, g_ref[...], b_ref[...])
    x1_ref[...] = x1
    x1b_ref[...] = x1.astype(BF16)
    logits = lax.dot_general(wr_ref[...], x1, (((1,), (1,)), ((), ())),
                             precision=lax.Precision.HIGHEST, preferred_element_type=F32)
    sc = 1.0 / (1.0 + jnp.exp(-logits))
    bi = sc + br_ref[...]
    scores = [sc[g * GROUP_SIZE:(g + 1) * GROUP_SIZE, :] for g in range(N_GROUPS)]
    biased = [bi[g * GROUP_SIZE:(g + 1) * GROUP_SIZE, :] for g in range(N_GROUPS)]
    gates, chosen, picks = _route_t(scores, biased)
    tm = x1.shape[0]
    ch = jnp.concatenate([jnp.where(c, 1.0, 0.0) for c in chosen], axis=0)
    gate_all = jnp.concatenate(gates, axis=0)
    sub = lax.broadcasted_iota(jnp.int32, (N_EXPERTS, MOE_BLOCK), 0)
    for blk in range(tm // MOE_BLOCK):
        ls = slice(blk * MOE_BLOCK, (blk + 1) * MOE_BLOCK)
        chb = ch[:, ls]
        rank = jnp.dot(chb.astype(BF16), before_ref[...], preferred_element_type=F32)
        cnt = jnp.sum(chb, axis=1, keepdims=True)
        seg = jnp.floor((cnt + (SEG_ALIGN - 1.0)) * (1.0 / SEG_ALIGN)) * SEG_ALIGN
        seg_b = jnp.broadcast_to(seg, (N_EXPERTS, LANES))
        off = jnp.dot(below_ref[...], seg_b, precision=lax.Precision.HIGHEST, preferred_element_type=F32)
        slot = jnp.concatenate([off] * (MOE_BLOCK // LANES), axis=1) + rank
        cnt_ref[blk] = jnp.sum(seg_b * eye_ref[...], axis=0, keepdims=True)
        gb = gate_all[:, ls]
        for k in range(TOP_K):
            hit = sub == picks[k][:, ls]
            sk_ref[k:k + 1, ls] = jnp.sum(jnp.where(hit, slot, 0.0), axis=0, keepdims=True)
            sk_ref[TOP_K + k:TOP_K + k + 1, ls] = jnp.sum(jnp.where(hit, gb, 0.0), axis=0, keepdims=True)


def _mix(od, om, x, wo, g, b, wr_t, br, tm, alpha):
    t = x.shape[0]
    d = x.shape[1]
    nblk = tm // MOE_BLOCK
    row = lambda i: (i, 0)
    const = lambda i: (0, 0)
    idx = jnp.arange(MOE_BLOCK)
    before = (idx[:, None] < idx[None, :]).astype(BF16)
    ide = jnp.arange(N_EXPERTS)
    below = (ide[None, :] < ide[:, None]).astype(F32)
    eye = (ide[:, None] == jnp.arange(LANES)[None, :]).astype(F32)
    return pl.pallas_call(
        functools.partial(_mix_kernel, alpha=alpha),
        grid=(t // tm,),
        in_specs=[pl.BlockSpec((tm, HEAD_W), row), pl.BlockSpec((tm, HEAD_W), row), pl.BlockSpec((tm, d), row),
                  pl.BlockSpec(wo.shape, const), pl.BlockSpec(g.shape, const), pl.BlockSpec(b.shape, const),
                  pl.BlockSpec(wr_t.shape, const), pl.BlockSpec(br.shape, const),
                  pl.BlockSpec(before.shape, const), pl.BlockSpec(below.shape, const), pl.BlockSpec(eye.shape, const)],
        out_specs=[pl.BlockSpec((tm, d), row), pl.BlockSpec((tm, d), row),
                   pl.BlockSpec((2 * TOP_K, tm), lambda i: (0, i)),
                   pl.BlockSpec((nblk, 1, LANES), lambda i: (i, 0, 0))],
        out_shape=[jax.ShapeDtypeStruct((t, d), F32), jax.ShapeDtypeStruct((t, d), BF16),
                   jax.ShapeDtypeStruct((2 * TOP_K, t), F32),
                   jax.ShapeDtypeStruct((t // MOE_BLOCK, 1, LANES), F32)],
        compiler_params=_cparams(("parallel",)),
        name="mix",
    )(od, om, x, wo, g, b, wr_t, br, before, below, eye)


def _moe_plan(cnt, n_tiles, trash_row):
    seg = cnt.astype(jnp.int32)
    nb = seg.shape[0]
    before_blocks = jnp.cumsum(seg, axis=0) - seg
    length = jnp.sum(seg, axis=0)
    padded = -(-length // FFN_TILE) * FFN_TILE
    ends = jnp.cumsum(padded)
    start = ends - padded
    dst = start[None, :] + before_blocks
    off = (jnp.cumsum(seg, axis=1) - seg) // SEG_ALIGN
    tot = jnp.sum(seg, axis=1) // SEG_ALIGN
    chunk = jnp.arange(BLOCK_CHUNKS, dtype=jnp.int32)
    base = dst - SEG_ALIGN * off
    step = jnp.concatenate([base[:, :1], base[:, 1:] - base[:, :-1]], axis=1)
    started = (off[:, None, :] <= chunk[None, :, None]).astype(jnp.int32)
    row = jnp.sum(started * step[:, None, :], axis=2) + SEG_ALIGN * chunk[None, :]
    valid = chunk[None, :] < tot[:, None]
    trash = trash_row + ((jnp.arange(nb, dtype=jnp.int32) % 3) * BLOCK_ROWS)[:, None] + SEG_ALIGN * chunk[None, :]
    first = trash_row + 2 * BLOCK_ROWS + SEG_ALIGN * chunk[None, :]
    put = jnp.concatenate([first, jnp.where(valid, row, trash)], axis=0).reshape(-1)
    get = jnp.concatenate([jnp.where(valid, row, 0), jnp.zeros((2, BLOCK_CHUNKS), jnp.int32)], axis=0).reshape(-1)
    n_used = ends[-1] // FFN_TILE
    tile_start = jnp.arange(n_tiles, dtype=jnp.int32) * FFN_TILE
    tile_e = jnp.sum((ends[None, :] <= tile_start[:, None]).astype(jnp.int32), axis=1)
    last_e = tile_e[jnp.maximum(n_used - 1, 0)]
    tile_e = jnp.where(jnp.arange(n_tiles) < n_used, tile_e, last_e)
    tail_start = start + length
    tail_nch = (padded - length) // SEG_ALIGN
    return dict(put=put, get=get, n_used=n_used.reshape(1), tile_e=tile_e, tail_start=tail_start,
                tail_nch=tail_nch, tail_tot=jnp.sum(tail_nch).reshape(1))


def _drain(copy, n):
    def body(i, c):
        copy.wait()
        return c
    lax.fori_loop(0, n, body, 0)


def _order_rows(sk, j0, vals):
    rows = lax.broadcasted_iota(jnp.int32, (ORD_SUB, MOE_BLOCK), 0).astype(F32).astype(BF16)
    out = jnp.zeros((ORD_SUB, MOE_BLOCK), BF16)
    for k in range(TOP_K):
        rel = (sk[k:k + 1, :] - j0).astype(BF16)
        out = jnp.where(rows == rel, vals[k], out)
    return out


def _dispatch_kernel(put_ref, tstart_ref, tnch_ref, ttot_ref, x_ref, sk_ref, xs_ref, buf_ref, zero_ref, sem, tail_sem):
    b = pl.program_id(0)
    last = pl.num_programs(0) - 1
    slot = lax.rem(b, 3)
    send = lax.rem(b + 2, 3)

    def whole(s):
        return pltpu.make_async_copy(buf_ref.at[s], xs_ref.at[pl.ds(0, BLOCK_ROWS), :], sem.at[s])

    @pl.when(b == 0)
    def _():
        buf_ref[...] = jnp.zeros(buf_ref.shape, BF16)

    @pl.when(b >= 2)
    def _():
        whole(slot).wait()

    x = x_ref[...]
    sk = sk_ref[0:TOP_K, :]
    ones = [jnp.ones((1, MOE_BLOCK), BF16)] * TOP_K
    base = b * BLOCK_CHUNKS
    for sub in range(BLOCK_ROWS // DISP_SUB):
        j0 = sub * DISP_SUB
        sel = jnp.concatenate([_order_rows(sk, float(j0 + i * ORD_SUB), ones) for i in range(DISP_SUB // ORD_SUB)],
                              axis=0)
        buf_ref[slot, j0:j0 + DISP_SUB, :] = jnp.dot(sel, x, preferred_element_type=F32).astype(BF16)
        for c in range(sub * (DISP_SUB // SEG_ALIGN), (sub + 1) * (DISP_SUB // SEG_ALIGN)):
            pltpu.make_async_copy(
                buf_ref.at[send, c * SEG_ALIGN:(c + 1) * SEG_ALIGN, :],
                xs_ref.at[pl.ds(pl.multiple_of(put_ref[base + c], SEG_ALIGN), SEG_ALIGN), :], sem.at[send]).start()

    @pl.when(b == last)
    def _():
        whole(send).wait()

        @pl.when(b >= 1)
        def _():
            whole(lax.rem(b + 1, 3)).wait()

        zero_ref[...] = jnp.zeros(zero_ref.shape, BF16)

        def tail_copy(dst_row):
            return pltpu.make_async_copy(zero_ref, xs_ref.at[pl.ds(pl.multiple_of(dst_row, SEG_ALIGN), SEG_ALIGN), :],
                                         tail_sem.at[0])

        def tail_expert(e, carry):
            def tail_chunk(c, carry2):
                tail_copy(tstart_ref[e] + c * SEG_ALIGN).start()
                return carry2
            lax.fori_loop(0, tnch_ref[e], tail_chunk, 0)
            return carry

        lax.fori_loop(0, N_EXPERTS, tail_expert, 0)
        _drain(tail_copy(0), ttot_ref[0])


def _dispatch(plan, x1b, sk, n_rows):
    t, d = x1b.shape
    nb = t // MOE_BLOCK
    grid_spec = pltpu.PrefetchScalarGridSpec(
        num_scalar_prefetch=4,
        grid=(nb + 1,),
        in_specs=[pl.BlockSpec((MOE_BLOCK, d), lambda b, *_: (jnp.minimum(b, nb - 1), 0)),
                  pl.BlockSpec((2 * TOP_K, MOE_BLOCK), lambda b, *_: (0, jnp.minimum(b, nb - 1)))],
        out_specs=pl.BlockSpec(memory_space=pl.ANY),
        scratch_shapes=[pltpu.VMEM((3, BLOCK_ROWS, d), BF16), pltpu.VMEM((SEG_ALIGN, d), BF16),
                        pltpu.SemaphoreType.DMA((3,)), pltpu.SemaphoreType.DMA((1,))],
    )
    return pl.pallas_call(
        _dispatch_kernel,
        grid_spec=grid_spec,
        out_shape=jax.ShapeDtypeStruct((n_rows, d), BF16),
        compiler_params=_cparams(("arbitrary",)),
        name="dispatch",
    )(plan["put"], plan["tail_start"], plan["tail_nch"], plan["tail_tot"], x1b, sk)


def _ffn_kernel(te_ref, nu_ref, xs_ref, wg_ref, wu_ref, wd_ref, y_ref):
    @pl.when(pl.program_id(0) < nu_ref[0])
    def _():
        x = xs_ref[...]
        hg = jnp.dot(x, wg_ref[...], preferred_element_type=F32)
        hu = jnp.dot(x, wu_ref[...], preferred_element_type=F32)
        h = hg * (1.0 / (1.0 + jnp.exp(-hg))) * hu
        y_ref[...] = jnp.dot(h.astype(BF16), wd_ref[...], preferred_element_type=F32).astype(BF16)


def _ffn(plan, xs, wg, wu, wd, n_tiles):
    d = xs.shape[1]
    used = lambda i, te, nu: (jnp.minimum(i, nu[0] - 1), 0)
    wsel = lambda i, te, nu: (te[i], 0, 0)
    grid_spec = pltpu.PrefetchScalarGridSpec(
        num_scalar_prefetch=2,
        grid=(n_tiles,),
        in_specs=[pl.BlockSpec((FFN_TILE, d), used),
                  pl.BlockSpec((None, d, EXPERT_DIM), wsel), pl.BlockSpec((None, d, EXPERT_DIM), wsel),
                  pl.BlockSpec((None, EXPERT_DIM, d), wsel)],
        out_specs=pl.BlockSpec((FFN_TILE, d), used),
    )
    return pl.pallas_call(
        _ffn_kernel,
        grid_spec=grid_spec,
        out_shape=jax.ShapeDtypeStruct((n_tiles * FFN_TILE, d), BF16),
        compiler_params=_cparams(("arbitrary",)),
        name="ffn",
    )(plan["tile_e"], plan["n_used"], xs, wg, wu, wd)


def _combine_kernel(get_ref, y_ref, sk_ref, x1_ref, x1b_ref, wsg_ref, wsu_ref, wsd_ref,
                    g_ref, b_ref, o_ref, buf_ref, sem, *, alpha):
    b = pl.program_id(0)
    slot = lax.rem(b, 3)
    nxt = lax.rem(b + 2, 3)

    def whole(s):
        return pltpu.make_async_copy(y_ref.at[pl.ds(0, BLOCK_ROWS), :], buf_ref.at[s], sem.at[s])

    @pl.when(b == 0)
    def _():
        def first(c, carry):
            s = c // BLOCK_CHUNKS
            pltpu.make_async_copy(
                y_ref.at[pl.ds(pl.multiple_of(get_ref[c], SEG_ALIGN), SEG_ALIGN), :],
                buf_ref.at[s, pl.ds(pl.multiple_of((c - s * BLOCK_CHUNKS) * SEG_ALIGN, SEG_ALIGN), SEG_ALIGN), :],
                sem.at[s]).start()
            return carry
        lax.fori_loop(0, 2 * BLOCK_CHUNKS, first, 0)

    whole(slot).wait()

    sk = sk_ref[0:TOP_K, :]
    gates = [sk_ref[TOP_K + k:TOP_K + k + 1, :].astype(BF16) for k in range(TOP_K)]
    base = (b + 2) * BLOCK_CHUNKS
    acc = None
    for sub in range(BLOCK_ROWS // DISP_SUB):
        j0 = sub * DISP_SUB
        w = jnp.concatenate([_order_rows(sk, float(j0 + i * ORD_SUB), gates) for i in range(DISP_SUB // ORD_SUB)],
                            axis=0)
        part = lax.dot_general(w, buf_ref[slot, j0:j0 + DISP_SUB, :], (((0,), (0,)), ((), ())),
                               preferred_element_type=F32)
        acc = part if acc is None else acc + part
        for c in range(sub * (DISP_SUB // SEG_ALIGN), (sub + 1) * (DISP_SUB // SEG_ALIGN)):
            pltpu.make_async_copy(
                y_ref.at[pl.ds(pl.multiple_of(get_ref[base + c], SEG_ALIGN), SEG_ALIGN), :],
                buf_ref.at[nxt, c * SEG_ALIGN:(c + 1) * SEG_ALIGN, :], sem.at[nxt]).start()

    xb = x1b_ref[...]
    hg = jnp.dot(xb, wsg_ref[...], preferred_element_type=F32)
    hu = jnp.dot(xb, wsu_ref[...], preferred_element_type=F32)
    h = hg * (1.0 / (1.0 + jnp.exp(-hg))) * hu
    acc = acc + jnp.dot(h.astype(BF16), wsd_ref[...], preferred_element_type=F32)
    o_ref[...] = _layer_norm(alpha * x1_ref[...] + acc, g_ref[...], b_ref[...])

    @pl.when(b == pl.num_programs(0) - 1)
    def _():
        whole(lax.rem(b + 1, 3)).wait()
        whole(nxt).wait()


def _combine(plan, y, sk, x1, x1b, wsg, wsu, wsd, g, b, alpha):
    t, d = x1.shape
    nb = t // MOE_BLOCK
    row = lambda i, *_: (i, 0)
    const = lambda i, *_: (0, 0)
    grid_spec = pltpu.PrefetchScalarGridSpec(
        num_scalar_prefetch=1,
        grid=(nb,),
        in_specs=[pl.BlockSpec(memory_space=pl.ANY), pl.BlockSpec((2 * TOP_K, MOE_BLOCK), lambda i, *_: (0, i)),
                  pl.BlockSpec((MOE_BLOCK, d), row), pl.BlockSpec((MOE_BLOCK, d), row),
                  pl.BlockSpec(wsg.shape, const), pl.BlockSpec(wsu.shape, const), pl.BlockSpec(wsd.shape, const),
                  pl.BlockSpec(g.shape, const), pl.BlockSpec(b.shape, const)],
        out_specs=pl.BlockSpec((MOE_BLOCK, d), row),
        scratch_shapes=[pltpu.VMEM((3, BLOCK_ROWS, d), BF16), pltpu.SemaphoreType.DMA((3,))],
    )
    return pl.pallas_call(
        functools.partial(_combine_kernel, alpha=alpha),
        grid_spec=grid_spec,
        out_shape=jax.ShapeDtypeStruct((t, d), F32),
        compiler_params=_cparams(("arbitrary",)),
        name="combine",
    )(plan["get"], y, sk, x1, x1b, wsg, wsu, wsd, g, b)


def _moe(x1, x1b, sk, cnt, wg, wu, wd, wsg, wsu, wsd, g, b, alpha):
    t = x1.shape[0]
    nb = t // MOE_BLOCK
    n_tiles = -(-(nb * BLOCK_ROWS) // FFN_TILE) + N_EXPERTS
    n_rows = n_tiles * FFN_TILE + 3 * BLOCK_ROWS
    plan = _moe_plan(cnt[:, 0, :N_EXPERTS], n_tiles, n_tiles * FFN_TILE)
    xs = _dispatch(plan, x1b, sk, n_rows)
    y = _ffn(plan, xs, wg, wu, wd, n_tiles)
    return _combine(plan, y, sk, x1, x1b, wsg, wsu, wsd, g, b, alpha)


def _rope_cs(pos, dim):
    inv = ROPE_THETA ** (-jnp.arange(0, dim, 2, dtype=F32) / dim)
    ang = pos.astype(F32)[:, None] * inv[None, :]
    return jnp.cos(ang), jnp.sin(ang)


def _tables(pos, reps):
    n = pos.shape[0]
    c32, s32 = _rope_cs(pos, DIFF_QK)
    c16, s16 = _rope_cs(pos, MLA_ROPE)
    one = lambda w: jnp.ones((n, w), F32)
    zero = lambda w: jnp.zeros((n, w), F32)
    cd = jnp.concatenate([c32] * 4, axis=1)
    sd = jnp.concatenate([-s32, s32] * 2, axis=1)
    cq = jnp.concatenate([one(MLA_NOPE), c16, c16, one(32)], axis=1)
    sq = jnp.concatenate([zero(MLA_NOPE), -s16, s16, zero(32)], axis=1)
    ck = jnp.concatenate([c16, c16, zero(96)], axis=1)
    sk = jnp.concatenate([-s16, s16, zero(96)], axis=1)
    return tuple(jnp.tile(a, (reps, 1)) for a in (cd, sd, cq, sq, ck, sk))


def kernel(x_prompt, x_sample, cache_diff_k, cache_diff_v, cache_mla_ckv, cache_mla_kpe, w_in, diff_lambda, diff_subln_g, mla_q_norm_g, mla_w_uq, mla_kv_norm_g, mla_w_ukv, w_out, ln1_g, ln1_b, w_router, b_router, w_exp_gate, w_exp_up, w_exp_down, w_sh_gate, w_sh_up, w_sh_down, ln2_g, ln2_b):
    depth = w_in.shape[0]
    assert depth == 1
    d_model = x_prompt.shape[-1]
    alpha = (2.0 * depth) ** 0.25
    past_len = cache_diff_k.shape[2]
    layer = 0
    lambda_init = 0.8 - 0.6 * math.exp(-0.3 * layer)

    w_in_b = jnp.pad(w_in[layer], ((0, 0), (0, IN_PAD - IN_WIDTH))).astype(BF16)
    wuq = jnp.pad(mla_w_uq[layer], ((0, 0), (0, 0), (0, LANES - MLA_NOPE - MLA_ROPE)))
    wuq = wuq.reshape(Q_LORA, HEAD_W).astype(BF16)
    wukv = mla_w_ukv[layer]
    wuk = jnp.pad(wukv[:, :, :MLA_NOPE], ((0, 0), (0, 0), (0, LANES - MLA_NOPE))).reshape(KV_LORA, HEAD_W).astype(BF16)
    wuv = wukv[:, :, MLA_NOPE:].reshape(KV_LORA, HEAD_W).astype(BF16)
    place = jnp.pad(jnp.eye(MLA_ROPE, dtype=F32), ((0, 0), (MLA_NOPE, LANES - MLA_NOPE - MLA_ROPE)))
    place = jnp.tile(place, (1, MLA_HEADS)).astype(BF16)
    gq = mla_q_norm_g[layer].reshape(1, Q_LORA)
    gkv = mla_kv_norm_g[layer].reshape(1, KV_LORA)
    gsub = diff_subln_g[layer].reshape(1, LANES)
    wo = w_out[layer].astype(BF16)
    g1, b1 = ln1_g[layer].reshape(1, d_model), ln1_b[layer].reshape(1, d_model)
    g2, b2 = ln2_g[layer].reshape(1, d_model), ln2_b[layer].reshape(1, d_model)
    wr_t = w_router[layer].T
    br = b_router[layer].reshape(N_EXPERTS, 1)
    wg, wu, wd = (w[layer].astype(BF16) for w in (w_exp_gate, w_exp_up, w_exp_down))
    wsg, wsu, wsd = (w[layer].astype(BF16) for w in (w_sh_gate, w_sh_up, w_sh_down))
    lp = diff_lambda[layer].astype(F32)
    lam = jnp.exp(jnp.sum(lp[0] * lp[1])) - jnp.exp(jnp.sum(lp[2] * lp[3])) + lambda_init
    lam = lam.reshape(1, 1)

    def group(x, pos, past, tm, tq, tk, tkm):
        b, s, _ = x.shape
        t = b * s
        tm = min(tm, t)
        xf = x.reshape(t, d_model)
        reps = max(1, tm // s)
        tables = _tables(pos, reps)
        n_pat = (s * reps) // tm
        qd, kd32, kd16, vd32, vd16, ckv, kpe, qm = _proj(xf, w_in_b, wuq, gq, gkv, tables, tm, n_pat)
        if past is None:
            k_d, v_d = kd16.reshape(b, s, HEAD_W), vd16.reshape(b, s, HEAD_W)
            ckv_all, kpe_all = ckv, kpe
            sk, q_pos0 = s, 0
        else:
            pk, pv, pc, pp = past
            sk = -(-(past_len + s) // tk) * tk
            padr = sk - past_len - s
            cat = lambda old, new: jnp.pad(jnp.concatenate([old, new], axis=1), ((0, 0), (0, padr), (0, 0)))
            k_d = cat(pk.reshape(b, past_len, HEAD_W).astype(BF16), kd16.reshape(b, s, HEAD_W))
            v_d = cat(pv.reshape(b, past_len, HEAD_W).astype(BF16), vd16.reshape(b, s, HEAD_W))
            ckv_all = cat(pc, ckv.reshape(b, s, KV_LORA)).reshape(b * sk, KV_LORA)
            kpe_all = cat(pp, kpe.reshape(b, s, MLA_ROPE)).reshape(b * sk, MLA_ROPE)
            q_pos0 = past_len
        k_m, v_m = _kvup(ckv_all, kpe_all, wuk, place, wuv, min(1024, ckv_all.shape[0]))
        o_d = _flash(qd.reshape(b, s, HEAD_W), k_d, v_d, lam, gsub, n_comp=2, tq=tq, tk=tk, tkm=tkm,
                     q_pos0=q_pos0, out_scale=1.0 - lambda_init)
        o_m = _flash(qm.reshape(b, s, HEAD_W), k_m.reshape(b, sk, HEAD_W), v_m.reshape(b, sk, HEAD_W), lam, gsub,
                     n_comp=1, tq=tq, tk=tk, tkm=tkm, q_pos0=q_pos0, out_scale=1.0)
        x1, x1b, slots, cnt = _mix(o_d.reshape(t, HEAD_W), o_m.reshape(t, HEAD_W), xf, wo, g1, b1, wr_t, br,
                                   tm, alpha)
        y = _moe(x1, x1b, slots, cnt, wg, wu, wd, wsg, wsu, wsd, g2, b2, alpha)
        rows = (kd32.reshape(1, b, s, DIFF_HEADS, LANES), vd32.reshape(1, b, s, DIFF_HEADS, LANES),
                ckv.reshape(1, b, s, KV_LORA), kpe.reshape(1, b, s, MLA_ROPE))
        return y.reshape(b, s, d_model), rows

    s_p = x_prompt.shape[1]
    s_s = x_sample.shape[1]
    pos_p = jnp.arange(s_p, dtype=jnp.int32)
    pos_s = past_len + jnp.arange(s_s, dtype=jnp.int32)
    y_p, r_p = group(x_prompt, pos_p, None, 512, 512, 512, 256)
    past = (cache_diff_k[layer], cache_diff_v[layer], cache_mla_ckv[layer], cache_mla_kpe[layer])
    y_s, r_s = group(x_sample, pos_s, past, 512, s_s, 512, 256)
    return (y_p, y_s) + r_p + r_s
```

```python
import functools
import math

import jax
import jax.numpy as jnp
from jax import lax
from jax.experimental import pallas as pl
from jax.experimental.pallas import tpu as pltpu

F32 = jnp.float32
BF16 = jnp.bfloat16

LANES = 128
VMEM_LIMIT = 52 * 1024 * 1024

CHUNK = 64
CHUNK_SHIFT = 6
ROPE_THETA = 10000.0
LN_EPS = 1e-5
RMS_EPS = 1e-6
LOG2E = 1.4426950408889634
NEG_BIG = -1e30
FLASH_SUB = 128

DIFF_HEADS = 4
DIFF_QK = 64
MLA_HEADS = 4
MLA_NOPE = 64
MLA_ROPE = 32
MLA_V = 128
Q_LORA = 384
KV_LORA = 256
N_EXPERTS = 64
N_GROUPS = 8
GROUP_SIZE = N_EXPERTS // N_GROUPS
TOPK_GROUPS = 4
TOP_K = 8
ROUTED_SCALE = 2.5
EXPERT_DIM = 256

MOE_BLOCK = 256
SEG_ALIGN = 16
ORD_SUB = 256
DISP_SUB = 1024
BLOCK_ROWS = -(-(TOP_K * MOE_BLOCK + N_EXPERTS * (SEG_ALIGN - 1)) // DISP_SUB) * DISP_SUB
BLOCK_CHUNKS = BLOCK_ROWS // SEG_ALIGN
FFN_TILE = 1024
FFN_TILE_SMALL = 256

DQ_W = DIFF_HEADS * 2 * DIFF_QK
HEAD_W = DIFF_HEADS * LANES
IN_WIDTH = 3 * DQ_W + Q_LORA + KV_LORA + MLA_ROPE
IN_PAD = 2304
OFF_DK, OFF_DV, OFF_CQ, OFF_CKV, OFF_KPE = 512, 1024, 1536, 1920, 2176


def _cparams(sem):
    return pltpu.CompilerParams(dimension_semantics=sem, vmem_limit_bytes=VMEM_LIMIT)


def _rms(x, g):
    return x * lax.rsqrt(jnp.mean(x * x, axis=-1, keepdims=True) + RMS_EPS) * g


def _layer_norm(x, g, b):
    mu = jnp.mean(x, axis=-1, keepdims=True)
    xc = x - mu
    var = jnp.mean(xc * xc, axis=-1, keepdims=True)
    return xc * lax.rsqrt(var + LN_EPS) * g + b


def _proj_kernel(x_ref, w_ref, wuq_ref, gq_ref, gkv_ref, cd_ref, sd_ref, cq_ref, sq_ref, ck_ref, sk_ref,
                 qd_ref, kd32_ref, kd16_ref, vd32_ref, vd16_ref, ckv_ref, kpe_ref, qm_ref,
                 *, scale_d, scale_m):
    tm = x_ref.shape[0]
    x = x_ref[...].astype(BF16)
    proj = jnp.dot(x, w_ref[...], preferred_element_type=F32)
    lane = lax.broadcasted_iota(jnp.int32, (tm, LANES), 1)

    first_d = (lane & 63) < 32
    cd = cd_ref[...]
    sd = sd_ref[...]

    def rope_d(blk):
        rot = jnp.where(first_d, pltpu.roll(blk, LANES - 32, 1), pltpu.roll(blk, 32, 1))
        return blk * cd + rot * sd

    for j in range(DIFF_HEADS):
        sl = slice(j * LANES, (j + 1) * LANES)
        qd_ref[:, sl] = (rope_d(proj[:, sl]) * scale_d).astype(BF16)
        kr = rope_d(proj[:, OFF_DK + j * LANES:OFF_DK + (j + 1) * LANES])
        kd32_ref[pl.ds(j, tm, stride=DIFF_HEADS), :] = kr
        kd16_ref[:, sl] = kr.astype(BF16)
        dv = proj[:, OFF_DV + j * LANES:OFF_DV + (j + 1) * LANES]
        vd32_ref[pl.ds(j, tm, stride=DIFF_HEADS), :] = dv
        vd16_ref[:, sl] = dv.astype(BF16)

    cqn = _rms(proj[:, OFF_CQ:OFF_CQ + Q_LORA], gq_ref[...])
    q = jnp.dot(cqn.astype(BF16), wuq_ref[...], preferred_element_type=F32)
    cq = cq_ref[...]
    sq = sq_ref[...]
    first_q = lane < (MLA_NOPE + MLA_ROPE // 2)
    for h in range(MLA_HEADS):
        sl = slice(h * LANES, (h + 1) * LANES)
        blk = q[:, sl]
        rot = jnp.where(first_q, pltpu.roll(blk, LANES - 16, 1), pltpu.roll(blk, 16, 1))
        qm_ref[:, sl] = ((blk * cq + rot * sq) * scale_m).astype(BF16)

    ckv_ref[...] = _rms(proj[:, OFF_CKV:OFF_CKV + KV_LORA], gkv_ref[...])

    kb = proj[:, OFF_KPE:OFF_KPE + LANES]
    rot = jnp.where(lane < 16, pltpu.roll(kb, LANES - 16, 1), pltpu.roll(kb, 16, 1))
    kpe_ref[...] = (kb * ck_ref[...] + rot * sk_ref[...])[:, :MLA_ROPE]


def _proj(x, w_in, wuq, gq, gkv, tables, tm, n_pat):
    t = x.shape[0]
    row = lambda i: (i, 0)
    const = lambda i: (0, 0)
    pat = lambda i: (i % n_pat, 0)
    tab_spec = pl.BlockSpec((tm, LANES), pat)
    out_w = lambda w, dt: jax.ShapeDtypeStruct((t, w), dt)
    cache_spec = pl.BlockSpec((tm * DIFF_HEADS, LANES), row)
    cache_shape = jax.ShapeDtypeStruct((t * DIFF_HEADS, LANES), F32)
    return pl.pallas_call(
        functools.partial(_proj_kernel, scale_d=LOG2E * DIFF_QK ** -0.5,
                          scale_m=LOG2E * (MLA_NOPE + MLA_ROPE) ** -0.5),
        grid=(t // tm,),
        in_specs=[pl.BlockSpec((tm, x.shape[1]), row),
                  pl.BlockSpec(w_in.shape, const), pl.BlockSpec(wuq.shape, const),
                  pl.BlockSpec(gq.shape, const), pl.BlockSpec(gkv.shape, const)] + [tab_spec] * 6,
        out_specs=[pl.BlockSpec((tm, HEAD_W), row), cache_spec, pl.BlockSpec((tm, HEAD_W), row), cache_spec,
                   pl.BlockSpec((tm, HEAD_W), row), pl.BlockSpec((tm, KV_LORA), row),
                   pl.BlockSpec((tm, MLA_ROPE), row), pl.BlockSpec((tm, HEAD_W), row)],
        out_shape=[out_w(HEAD_W, BF16), cache_shape, out_w(HEAD_W, BF16), cache_shape,
                   out_w(HEAD_W, BF16), out_w(KV_LORA, F32), out_w(MLA_ROPE, F32), out_w(HEAD_W, BF16)],
        compiler_params=_cparams(("parallel",)),
        name="proj",
    )(x, w_in, wuq, gq, gkv, *tables)


def _kvup_kernel(ckv_ref, kpe_ref, wuk_ref, place_ref, wuv_ref, k_ref, v_ref):
    c = ckv_ref[...].astype(BF16)
    k = jnp.dot(c, wuk_ref[...], preferred_element_type=F32)
    k = k + jnp.dot(kpe_ref[...].astype(BF16), place_ref[...], preferred_element_type=F32)
    k_ref[...] = k.astype(BF16)
    v_ref[...] = jnp.dot(c, wuv_ref[...], preferred_element_type=F32).astype(BF16)


def _kvup(ckv, kpe, wuk, place, wuv, tm):
    r = ckv.shape[0]
    row = lambda i: (i, 0)
    const = lambda i: (0, 0)
    return pl.pallas_call(
        _kvup_kernel,
        grid=(r // tm,),
        in_specs=[pl.BlockSpec((tm, KV_LORA), row), pl.BlockSpec((tm, MLA_ROPE), row),
                  pl.BlockSpec(wuk.shape, const), pl.BlockSpec(place.shape, const), pl.BlockSpec(wuv.shape, const)],
        out_specs=[pl.BlockSpec((tm, HEAD_W), row)] * 2,
        out_shape=[jax.ShapeDtypeStruct((r, HEAD_W), BF16)] * 2,
        compiler_params=_cparams(("parallel",)),
        name="kvup",
    )(ckv, kpe, wuk, place, wuv)


def _catcast_kernel(ck_ref, cv_ref, nk_ref, nv_ref, k_ref, v_ref, *, n_past, s_new, tr):
    j = pl.program_id(1)

    @pl.when(j < n_past)
    def _():
        for h in range(DIFF_HEADS):
            sl = slice(h * LANES, (h + 1) * LANES)
            k_ref[:, sl] = ck_ref[pl.ds(h, tr, stride=DIFF_HEADS), :].astype(BF16)
            v_ref[:, sl] = cv_ref[pl.ds(h, tr, stride=DIFF_HEADS), :].astype(BF16)

    @pl.when(j >= n_past)
    def _():
        k_ref[...] = jnp.zeros(k_ref.shape, BF16)
        v_ref[...] = jnp.zeros(v_ref.shape, BF16)
        k_ref[0:s_new, :] = nk_ref[...]
        v_ref[0:s_new, :] = nv_ref[...]


def _catcast(cache_k, cache_v, new_k, new_v, tr):
    b, p = cache_k.shape[:2]
    s = new_k.shape[1]
    assert p % tr == 0 and s <= tr
    n_past = p // tr
    past = pl.BlockSpec((None, tr * DIFF_HEADS, LANES), lambda bi, j: (bi, jnp.minimum(j, n_past - 1), 0))
    new = pl.BlockSpec((None, s, HEAD_W), lambda bi, j: (bi, 0, 0))
    out = pl.BlockSpec((None, tr, HEAD_W), lambda bi, j: (bi, j, 0))
    return pl.pallas_call(
        functools.partial(_catcast_kernel, n_past=n_past, s_new=s, tr=tr),
        grid=(b, n_past + 1),
        in_specs=[past, past, new, new],
        out_specs=[out, out],
        out_shape=[jax.ShapeDtypeStruct((b, p + tr, HEAD_W), BF16)] * 2,
        compiler_params=_cparams(("parallel", "arbitrary")),
        name="catcast",
    )(cache_k.reshape(b, p * DIFF_HEADS, LANES), cache_v.reshape(b, p * DIFF_HEADS, LANES), new_k, new_v)


def _kvup_past_kernel(pc_ref, pp_ref, nc_ref, np_ref, wuk_ref, place_ref, wuv_ref, k_ref, v_ref, c_ref, r_ref,
                      *, n_past, s_new):
    j = pl.program_id(1)

    @pl.when(j < n_past)
    def _():
        c_ref[...] = pc_ref[...]
        r_ref[...] = pp_ref[...]

    @pl.when(j >= n_past)
    def _():
        c_ref[...] = jnp.zeros(c_ref.shape, F32)
        r_ref[...] = jnp.zeros(r_ref.shape, F32)
        c_ref[0:s_new, :] = nc_ref[...]
        r_ref[0:s_new, :] = np_ref[...]

    _kvup_kernel(c_ref, r_ref, wuk_ref, place_ref, wuv_ref, k_ref, v_ref)


def _kvup_past(cache_c, cache_r, new_c, new_r, wuk, place, wuv, tr):
    b, p = cache_c.shape[:2]
    s = new_c.shape[1]
    assert p % tr == 0 and s <= tr
    n_past = p // tr
    clamp = lambda bi, j: (bi, jnp.minimum(j, n_past - 1), 0)
    first = lambda bi, j: (bi, 0, 0)
    const = lambda bi, j: (0, 0)
    out = pl.BlockSpec((None, tr, HEAD_W), lambda bi, j: (bi, j, 0))
    return pl.pallas_call(
        functools.partial(_kvup_past_kernel, n_past=n_past, s_new=s),
        grid=(b, n_past + 1),
        in_specs=[pl.BlockSpec((None, tr, KV_LORA), clamp), pl.BlockSpec((None, tr, MLA_ROPE), clamp),
                  pl.BlockSpec((None, s, KV_LORA), first), pl.BlockSpec((None, s, MLA_ROPE), first),
                  pl.BlockSpec(wuk.shape, const), pl.BlockSpec(place.shape, const), pl.BlockSpec(wuv.shape, const)],
        out_specs=[out, out],
        out_shape=[jax.ShapeDtypeStruct((b, p + tr, HEAD_W), BF16)] * 2,
        scratch_shapes=[pltpu.VMEM((tr, KV_LORA), F32), pltpu.VMEM((tr, MLA_ROPE), F32)],
        compiler_params=_cparams(("parallel", "arbitrary")),
        name="kvup_past",
    )(cache_c, cache_r, new_c, new_r, wuk, place, wuv)


def _flash_kernel(lam_ref, q_ref, k_ref, v_ref, g_ref, o_ref, qs_ref, vx_ref, m_ref, acc_ref,
                  *, n_comp, hps, tq, tk, tkm, sub, sk, q_pos0, out_scale, static_diag):
    qi = pl.program_id(2)
    rows = n_comp * tq

    @pl.when(qi == 0)
    def _():
        for h in range(hps):
            vx_ref[:, 2 * h * LANES:(2 * h + 1) * LANES] = v_ref[:, h * LANES:(h + 1) * LANES]
            vx_ref[:, (2 * h + 1) * LANES:(2 * h + 2) * LANES] = jnp.ones((sk, LANES), BF16)

    for h in range(hps):
        q = q_ref[:, h * LANES:(h + 1) * LANES]
        if n_comp == 2:
            lane = lax.broadcasted_iota(jnp.int32, (tq, LANES), 1)
            zero = jnp.zeros_like(q)
            qs_ref[h * rows:h * rows + tq, :] = jnp.where(lane < DIFF_QK, q, zero)
            qs_ref[h * rows + tq:(h + 1) * rows, :] = jnp.where(lane >= DIFF_QK, q, zero)
        else:
            qs_ref[h * rows:(h + 1) * rows, :] = q
    m_ref[...] = jnp.full(m_ref.shape, NEG_BIG, F32)
    acc_ref[...] = jnp.zeros(acc_ref.shape, F32)

    q0 = q_pos0 + qi * tq
    lo_vis = jnp.minimum(((q0 >> CHUNK_SHIFT) + 1) << CHUNK_SHIFT, sk)
    hi_vis = jnp.minimum((((q0 + tq - 1) >> CHUNK_SHIFT) + 1) << CHUNK_SHIFT, sk)
    n_full = lo_vis // tk
    m_lo = n_full * (tk // tkm)
    m_hi = (hi_vis + tkm - 1) // tkm

    def sub_step(start, width, h, r0, masked):
        rs = slice(h * rows + r0, h * rows + r0 + sub)
        k = k_ref[pl.ds(start, width), h * LANES:(h + 1) * LANES]
        vx = vx_ref[pl.ds(start, width), 2 * h * LANES:(2 * h + 2) * LANES]
        s = lax.dot_general(qs_ref[rs, :], k, (((1,), (1,)), ((), ())), preferred_element_type=F32)
        if masked:
            r = (lax.broadcasted_iota(jnp.int32, (sub, width), 0) + r0) & (tq - 1)
            c = lax.broadcasted_iota(jnp.int32, (sub, width), 1)
            ok = ((start + c) >> CHUNK_SHIFT) <= ((q0 + r) >> CHUNK_SHIFT)
            s = jnp.where(ok, s, NEG_BIG)
        m_prev = m_ref[rs, :]
        m_new = jnp.maximum(m_prev, jnp.max(s, axis=-1, keepdims=True))
        alpha = jnp.exp2(m_prev - m_new)
        p = jnp.exp2(s - jnp.concatenate([m_new] * (width // LANES), axis=1))
        pv = jnp.dot(p.astype(BF16), vx, preferred_element_type=F32)
        acc_ref[rs, :] = jnp.concatenate([alpha, alpha], axis=1) * acc_ref[rs, :] + pv
        m_ref[rs, :] = m_new

    def step(start, width, masked):
        for h in range(hps):
            for r0 in range(0, rows, sub):
                sub_step(start, width, h, r0, masked)

    def full_body(j, carry):
        step(pl.multiple_of(j * tk, tk), tk, False)
        return carry

    def masked_body(j, carry):
        step(pl.multiple_of(j * tkm, tkm), tkm, True)
        return carry

    lax.fori_loop(0, n_full, full_body, 0)
    if static_diag:
        for c in range(tq // tkm):
            start = pl.multiple_of(q0 + c * tkm, tkm)
            first_kc, last_kc = (c * tkm) // CHUNK, (c * tkm + tkm - 1) // CHUNK
            for h in range(hps):
                for r0 in range(0, rows, sub):
                    first_qc, last_qc = (r0 % tq) // CHUNK, (r0 % tq + sub - 1) // CHUNK
                    if first_kc > last_qc:
                        continue
                    sub_step(start, tkm, h, r0, masked=last_kc > first_qc)
    else:
        lax.fori_loop(m_lo, m_hi, masked_body, 0)

    for h in range(hps):
        hr = slice(h * rows, (h + 1) * rows)
        o = acc_ref[hr, 0:LANES] / acc_ref[hr, LANES:2 * LANES]
        if n_comp == 2:
            o = o[0:tq, :] - lam_ref[0, 0] * o[tq:rows, :]
            o = _rms(o, g_ref[...]) * out_scale
        o_ref[:, h * LANES:(h + 1) * LANES] = o.astype(BF16)


def _flash(q, k, v, lam, g, *, n_comp, hps, tq, tk, tkm, q_pos0, out_scale):
    b, sq, _ = q.shape
    sk = k.shape[1]
    rows = n_comp * tq
    sub = min(rows, FLASH_SUB)
    assert sq % tq == 0 and sk % tk == 0 and tk % tkm == 0 and tq & (tq - 1) == 0 and rows % sub == 0
    assert DIFF_HEADS % hps == 0
    static_diag = (q_pos0 % tq == 0 and tq % tk == 0 and q_pos0 + sq <= sk and sub % CHUNK == 0
                   and tkm % CHUNK == 0)
    hw = hps * LANES
    return pl.pallas_call(
        functools.partial(_flash_kernel, n_comp=n_comp, hps=hps, tq=tq, tk=tk, tkm=tkm, sub=sub, sk=sk,
                          q_pos0=q_pos0, out_scale=out_scale, static_diag=static_diag),
        grid=(b, DIFF_HEADS // hps, sq // tq),
        in_specs=[pl.BlockSpec(memory_space=pltpu.SMEM),
                  pl.BlockSpec((None, tq, hw), lambda bi, h, i: (bi, i, h)),
                  pl.BlockSpec((None, sk, hw), lambda bi, h, i: (bi, 0, h)),
                  pl.BlockSpec((None, sk, hw), lambda bi, h, i: (bi, 0, h)),
                  pl.BlockSpec((1, LANES), lambda bi, h, i: (0, 0))],
        out_specs=pl.BlockSpec((None, tq, hw), lambda bi, h, i: (bi, i, h)),
        out_shape=jax.ShapeDtypeStruct((b, sq, HEAD_W), BF16),
        scratch_shapes=[pltpu.VMEM((hps * rows, LANES), BF16), pltpu.VMEM((sk, 2 * hw), BF16),
                        pltpu.VMEM((hps * rows, LANES), F32), pltpu.VMEM((hps * rows, 2 * LANES), F32)],
        compiler_params=_cparams(("parallel", "parallel", "arbitrary")),
        name="flash_diff" if n_comp == 2 else "flash_mla",
    )(lam, q, k, v, g)


def _sublane_max(x):
    return jnp.max(x, axis=0, keepdims=True)


def _sublane_min(x):
    return jnp.min(x, axis=0, keepdims=True)


def _route_t(scores, biased):
    tm = scores[0].shape[1]
    sub = lax.broadcasted_iota(jnp.int32, (GROUP_SIZE, tm), 0)
    neg_inf = jnp.float32(-jnp.inf)
    gs = jnp.zeros((N_GROUPS, tm), F32)
    for g in range(N_GROUPS):
        bg = biased[g]
        m1 = _sublane_max(bg)
        i1 = _sublane_min(jnp.where(bg == m1, sub, GROUP_SIZE))
        m2 = _sublane_max(jnp.where(sub == i1, neg_inf, bg))
        gs = jnp.where(sub == g, m1 + m2, gs)
    keep = jnp.zeros((N_GROUPS, tm), jnp.bool_)
    cur = gs
    for _ in range(TOPK_GROUPS):
        mx = _sublane_max(cur)
        fi = _sublane_min(jnp.where(cur == mx, sub, N_GROUPS))
        hit = sub == fi
        keep = jnp.logical_or(keep, hit)
        cur = jnp.where(hit, neg_inf, cur)
    keep_f = jnp.where(keep, 1.0, 0.0)
    cand = []
    for g in range(N_GROUPS):
        kg = _sublane_max(jnp.where(sub == g, keep_f, 0.0)) > 0.5
        cand.append(jnp.where(kg, biased[g], neg_inf))
    chosen = [jnp.zeros((GROUP_SIZE, tm), jnp.bool_) for _ in range(N_GROUPS)]
    picks = []
    for _ in range(TOP_K):
        mx = cand[0]
        for g in range(1, N_GROUPS):
            mx = jnp.maximum(mx, cand[g])
        mx = _sublane_max(mx)
        fi = jnp.where(cand[0] == mx, sub, N_EXPERTS)
        for g in range(1, N_GROUPS):
            fi = jnp.minimum(fi, jnp.where(cand[g] == mx, sub + g * GROUP_SIZE, N_EXPERTS))
        fi = _sublane_min(fi)
        picks.append(fi)
        for g in range(N_GROUPS):
            hit = (sub + g * GROUP_SIZE) == fi
            chosen[g] = jnp.logical_or(chosen[g], hit)
            cand[g] = jnp.where(hit, neg_inf, cand[g])
    w = [jnp.where(chosen[g], scores[g], 0.0) for g in range(N_GROUPS)]
    tot = w[0]
    for g in range(1, N_GROUPS):
        tot = tot + w[g]
    tot = jnp.sum(tot, axis=0, keepdims=True)
    return [wg / tot * ROUTED_SCALE for wg in w], chosen, picks


def _mix_kernel(od_ref, om_ref, x_ref, wo_ref, g_ref, b_ref, wr_ref, wrl_ref, br_ref, before_ref, below_ref, eye_ref,
                x1_ref, x1b_ref, sk_ref, cnt_ref, *, alpha):
    mix = jnp.dot(od_ref[...], wo_ref[0:HEAD_W, :], preferred_element_type=F32)
    mix = mix + jnp.dot(om_ref[...], wo_ref[HEAD_W:2 * HEAD_W, :], preferred_element_type=F32)
    x1 = _layer_norm(alpha * x_ref[...] + mix, g_ref[...], b_ref[...])
    x_hi = x1.astype(BF16)
    x1_ref[...] = x1
    x1b_ref[...] = x_hi
    x_lo = (x1 - x_hi.astype(F32)).astype(BF16)
    nt = (((1,), (1,)), ((), ()))
    logits = (lax.dot_general(wr_ref[...], x_hi, nt, preferred_element_type=F32)
              + lax.dot_general(wr_ref[...], x_lo, nt, preferred_element_type=F32)
              + lax.dot_general(wrl_ref[...], x_hi, nt, preferred_element_type=F32))
    sc = 1.0 / (1.0 + jnp.exp(-logits))
    bi = sc + br_ref[...]
    scores = [sc[g * GROUP_SIZE:(g + 1) * GROUP_SIZE, :] for g in range(N_GROUPS)]
    biased = [bi[g * GROUP_SIZE:(g + 1) * GROUP_SIZE, :] for g in range(N_GROUPS)]
    gates, chosen, picks = _route_t(scores, biased)
    tm = x1.shape[0]
    ch = jnp.concatenate([jnp.where(c, 1.0, 0.0) for c in chosen], axis=0)
    gate_all = jnp.concatenate(gates, axis=0)
    sub = lax.broadcasted_iota(jnp.int32, (N_EXPERTS, MOE_BLOCK), 0)
    for blk in range(tm // MOE_BLOCK):
        ls = slice(blk * MOE_BLOCK, (blk + 1) * MOE_BLOCK)
        chb = ch[:, ls]
        rank = jnp.dot(chb.astype(BF16), before_ref[...], preferred_element_type=F32)
        cnt = jnp.sum(chb, axis=1, keepdims=True)
        seg = jnp.floor((cnt + (SEG_ALIGN - 1.0)) * (1.0 / SEG_ALIGN)) * SEG_ALIGN
        seg_b = jnp.broadcast_to(seg, (N_EXPERTS, LANES))
        off = jnp.dot(below_ref[...], seg_b, precision=lax.Precision.HIGHEST, preferred_element_type=F32)
        slot = jnp.concatenate([off] * (MOE_BLOCK // LANES), axis=1) + rank
        cnt_ref[blk] = jnp.sum(seg_b * eye_ref[...], axis=0, keepdims=True)
        gb = gate_all[:, ls]
        for k in range(TOP_K):
            hit = sub == picks[k][:, ls]
            sk_ref[k:k + 1, ls] = jnp.sum(jnp.where(hit, slot, 0.0), axis=0, keepdims=True)
            sk_ref[TOP_K + k:TOP_K + k + 1, ls] = jnp.sum(jnp.where(hit, gb, 0.0), axis=0, keepdims=True)


def _mix(od, om, x, wo, g, b, wr_hi, wr_lo, br, tm, alpha):
    t = x.shape[0]
    d = x.shape[1]
    nblk = tm // MOE_BLOCK
    row = lambda i: (i, 0)
    const = lambda i: (0, 0)
    idx = jnp.arange(MOE_BLOCK)
    before = (idx[:, None] < idx[None, :]).astype(BF16)
    ide = jnp.arange(N_EXPERTS)
    below = (ide[None, :] < ide[:, None]).astype(F32)
    eye = (ide[:, None] == jnp.arange(LANES)[None, :]).astype(F32)
    return pl.pallas_call(
        functools.partial(_mix_kernel, alpha=alpha),
        grid=(t // tm,),
        in_specs=[pl.BlockSpec((tm, HEAD_W), row), pl.BlockSpec((tm, HEAD_W), row), pl.BlockSpec((tm, d), row),
                  pl.BlockSpec(wo.shape, const), pl.BlockSpec(g.shape, const), pl.BlockSpec(b.shape, const),
                  pl.BlockSpec(wr_hi.shape, const), pl.BlockSpec(wr_lo.shape, const), pl.BlockSpec(br.shape, const),
                  pl.BlockSpec(before.shape, const), pl.BlockSpec(below.shape, const), pl.BlockSpec(eye.shape, const)],
        out_specs=[pl.BlockSpec((tm, d), row), pl.BlockSpec((tm, d), row),
                   pl.BlockSpec((2 * TOP_K, tm), lambda i: (0, i)),
                   pl.BlockSpec((nblk, 1, LANES), lambda i: (i, 0, 0))],
        out_shape=[jax.ShapeDtypeStruct((t, d), F32), jax.ShapeDtypeStruct((t, d), BF16),
                   jax.ShapeDtypeStruct((2 * TOP_K, t), F32),
                   jax.ShapeDtypeStruct((t // MOE_BLOCK, 1, LANES), F32)],
        compiler_params=_cparams(("parallel",)),
        name="mix",
    )(od, om, x, wo, g, b, wr_hi, wr_lo, br, before, below, eye)


def _moe_plan(cnt, n_tiles, tile, trash_row):
    seg = cnt.astype(jnp.int32)
    nb = seg.shape[0]
    before_blocks = jnp.cumsum(seg, axis=0) - seg
    length = jnp.sum(seg, axis=0)
    padded = -(-length // tile) * tile
    ends = jnp.cumsum(padded)
    start = ends - padded
    dst = start[None, :] + before_blocks
    off = (jnp.cumsum(seg, axis=1) - seg) // SEG_ALIGN
    tot = jnp.sum(seg, axis=1) // SEG_ALIGN
    chunk = jnp.arange(BLOCK_CHUNKS, dtype=jnp.int32)
    base = dst - SEG_ALIGN * off
    step = jnp.concatenate([base[:, :1], base[:, 1:] - base[:, :-1]], axis=1)
    started = (off[:, None, :] <= chunk[None, :, None]).astype(jnp.int32)
    row = jnp.sum(started * step[:, None, :], axis=2) + SEG_ALIGN * chunk[None, :]
    valid = chunk[None, :] < tot[:, None]
    trash = trash_row + ((jnp.arange(nb, dtype=jnp.int32) % 3) * BLOCK_ROWS)[:, None] + SEG_ALIGN * chunk[None, :]
    first = trash_row + 2 * BLOCK_ROWS + SEG_ALIGN * chunk[None, :]
    put = jnp.concatenate([first, jnp.where(valid, row, trash)], axis=0).reshape(-1)
    get = jnp.concatenate([jnp.where(valid, row, 0), jnp.zeros((2, BLOCK_CHUNKS), jnp.int32)], axis=0).reshape(-1)
    n_used = ends[-1] // tile
    tile_start = jnp.arange(n_tiles, dtype=jnp.int32) * tile
    tile_e = jnp.sum((ends[None, :] <= tile_start[:, None]).astype(jnp.int32), axis=1)
    last_e = tile_e[jnp.maximum(n_used - 1, 0)]
    tile_e = jnp.where(jnp.arange(n_tiles) < n_used, tile_e, last_e)
    tail_start = start + length
    tail_nch = (padded - length) // SEG_ALIGN
    return dict(put=put, get=get, n_used=n_used.reshape(1), tile_e=tile_e, tail_start=tail_start,
                tail_nch=tail_nch, tail_tot=jnp.sum(tail_nch).reshape(1))


def _drain(copy, n):
    def body(i, c):
        copy.wait()
        return c
    lax.fori_loop(0, n, body, 0)


def _order_rows(sk, j0, vals):
    rows = lax.broadcasted_iota(jnp.int32, (ORD_SUB, MOE_BLOCK), 0).astype(F32).astype(BF16)
    out = jnp.zeros((ORD_SUB, MOE_BLOCK), BF16)
    for k in range(TOP_K):
        rel = (sk[k:k + 1, :] - j0).astype(BF16)
        out = jnp.where(rows == rel, vals[k], out)
    return out


def _dispatch_kernel(put_ref, tstart_ref, tnch_ref, ttot_ref, x_ref, sk_ref, xs_ref, buf_ref, zero_ref, sem, tail_sem):
    b = pl.program_id(0)
    last = pl.num_programs(0) - 1
    slot = lax.rem(b, 3)
    send = lax.rem(b + 2, 3)

    def whole(s):
        return pltpu.make_async_copy(buf_ref.at[s], xs_ref.at[pl.ds(0, BLOCK_ROWS), :], sem.at[s])

    @pl.when(b == 0)
    def _():
        buf_ref[...] = jnp.zeros(buf_ref.shape, BF16)

    @pl.when(b >= 2)
    def _():
        whole(slot).wait()

    x = x_ref[...]
    sk = sk_ref[0:TOP_K, :]
    ones = [jnp.ones((1, MOE_BLOCK), BF16)] * TOP_K
    base = b * BLOCK_CHUNKS
    for sub in range(BLOCK_ROWS // DISP_SUB):
        j0 = sub * DISP_SUB
        sel = jnp.concatenate([_order_rows(sk, float(j0 + i * ORD_SUB), ones) for i in range(DISP_SUB // ORD_SUB)],
                              axis=0)
        buf_ref[slot, j0:j0 + DISP_SUB, :] = jnp.dot(sel, x, preferred_element_type=F32).astype(BF16)
        for c in range(sub * (DISP_SUB // SEG_ALIGN), (sub + 1) * (DISP_SUB // SEG_ALIGN)):
            pltpu.make_async_copy(
                buf_ref.at[send, c * SEG_ALIGN:(c + 1) * SEG_ALIGN, :],
                xs_ref.at[pl.ds(pl.multiple_of(put_ref[base + c], SEG_ALIGN), SEG_ALIGN), :], sem.at[send]).start()

    @pl.when(b == last)
    def _():
        whole(send).wait()

        @pl.when(b >= 1)
        def _():
            whole(lax.rem(b + 1, 3)).wait()

        zero_ref[...] = jnp.zeros(zero_ref.shape, BF16)

        def tail_copy(dst_row):
            return pltpu.make_async_copy(zero_ref, xs_ref.at[pl.ds(pl.multiple_of(dst_row, SEG_ALIGN), SEG_ALIGN), :],
                                         tail_sem.at[0])

        def tail_expert(e, carry):
            def tail_chunk(c, carry2):
                tail_copy(tstart_ref[e] + c * SEG_ALIGN).start()
                return carry2
            lax.fori_loop(0, tnch_ref[e], tail_chunk, 0)
            return carry

        lax.fori_loop(0, N_EXPERTS, tail_expert, 0)
        _drain(tail_copy(0), ttot_ref[0])


def _dispatch(plan, x1b, sk, n_rows):
    t, d = x1b.shape
    nb = t // MOE_BLOCK
    grid_spec = pltpu.PrefetchScalarGridSpec(
        num_scalar_prefetch=4,
        grid=(nb + 1,),
        in_specs=[pl.BlockSpec((MOE_BLOCK, d), lambda b, *_: (jnp.minimum(b, nb - 1), 0)),
                  pl.BlockSpec((2 * TOP_K, MOE_BLOCK), lambda b, *_: (0, jnp.minimum(b, nb - 1)))],
        out_specs=pl.BlockSpec(memory_space=pl.ANY),
        scratch_shapes=[pltpu.VMEM((3, BLOCK_ROWS, d), BF16), pltpu.VMEM((SEG_ALIGN, d), BF16),
                        pltpu.SemaphoreType.DMA((3,)), pltpu.SemaphoreType.DMA((1,))],
    )
    return pl.pallas_call(
        _dispatch_kernel,
        grid_spec=grid_spec,
        out_shape=jax.ShapeDtypeStruct((n_rows, d), BF16),
        compiler_params=_cparams(("arbitrary",)),
        name="dispatch",
    )(plan["put"], plan["tail_start"], plan["tail_nch"], plan["tail_tot"], x1b, sk)


def _ffn_kernel(te_ref, nu_ref, xs_ref, wg_ref, wu_ref, wd_ref, y_ref):
    @pl.when(pl.program_id(0) < nu_ref[0])
    def _():
        x = xs_ref[...]
        hg = jnp.dot(x, wg_ref[...], preferred_element_type=F32)
        hu = jnp.dot(x, wu_ref[...], preferred_element_type=F32)
        h = hg * (1.0 / (1.0 + jnp.exp(-hg))) * hu
        y_ref[...] = jnp.dot(h.astype(BF16), wd_ref[...], preferred_element_type=F32).astype(BF16)


def _ffn(plan, xs, wg, wu, wd, n_tiles, tile):
    d = xs.shape[1]
    used = lambda i, te, nu: (jnp.minimum(i, nu[0] - 1), 0)
    wsel = lambda i, te, nu: (te[i], 0, 0)
    grid_spec = pltpu.PrefetchScalarGridSpec(
        num_scalar_prefetch=2,
        grid=(n_tiles,),
        in_specs=[pl.BlockSpec((tile, d), used),
                  pl.BlockSpec((None, d, EXPERT_DIM), wsel), pl.BlockSpec((None, d, EXPERT_DIM), wsel),
                  pl.BlockSpec((None, EXPERT_DIM, d), wsel)],
        out_specs=pl.BlockSpec((tile, d), used),
    )
    return pl.pallas_call(
        _ffn_kernel,
        grid_spec=grid_spec,
        out_shape=jax.ShapeDtypeStruct((n_tiles * tile, d), BF16),
        compiler_params=_cparams(("arbitrary",)),
        name="ffn",
    )(plan["tile_e"], plan["n_used"], xs, wg, wu, wd)


def _combine_kernel(get_ref, y_ref, sk_ref, x1_ref, x1b_ref, wsg_ref, wsu_ref, wsd_ref,
                    g_ref, b_ref, o_ref, buf_ref, sem, *, alpha):
    b = pl.program_id(0)
    slot = lax.rem(b, 3)
    nxt = lax.rem(b + 2, 3)

    def whole(s):
        return pltpu.make_async_copy(y_ref.at[pl.ds(0, BLOCK_ROWS), :], buf_ref.at[s], sem.at[s])

    @pl.when(b == 0)
    def _():
        def first(c, carry):
            s = c // BLOCK_CHUNKS
            pltpu.make_async_copy(
                y_ref.at[pl.ds(pl.multiple_of(get_ref[c], SEG_ALIGN), SEG_ALIGN), :],
                buf_ref.at[s, pl.ds(pl.multiple_of((c - s * BLOCK_CHUNKS) * SEG_ALIGN, SEG_ALIGN), SEG_ALIGN), :],
                sem.at[s]).start()
            return carry
        lax.fori_loop(0, 2 * BLOCK_CHUNKS, first, 0)

    whole(slot).wait()

    sk = sk_ref[0:TOP_K, :]
    gates = [sk_ref[TOP_K + k:TOP_K + k + 1, :].astype(BF16) for k in range(TOP_K)]
    base = (b + 2) * BLOCK_CHUNKS
    acc = None
    for sub in range(BLOCK_ROWS // DISP_SUB):
        j0 = sub * DISP_SUB
        w = jnp.concatenate([_order_rows(sk, float(j0 + i * ORD_SUB), gates) for i in range(DISP_SUB // ORD_SUB)],
                            axis=0)
        part = lax.dot_general(w, buf_ref[slot, j0:j0 + DISP_SUB, :], (((0,), (0,)), ((), ())),
                               preferred_element_type=F32)
        acc = part if acc is None else acc + part
        for c in range(sub * (DISP_SUB // SEG_ALIGN), (sub + 1) * (DISP_SUB // SEG_ALIGN)):
            pltpu.make_async_copy(
                y_ref.at[pl.ds(pl.multiple_of(get_ref[base + c], SEG_ALIGN), SEG_ALIGN), :],
                buf_ref.at[nxt, c * SEG_ALIGN:(c + 1) * SEG_ALIGN, :], sem.at[nxt]).start()

    xb = x1b_ref[...]
    hg = jnp.dot(xb, wsg_ref[...], preferred_element_type=F32)
    hu = jnp.dot(xb, wsu_ref[...], preferred_element_type=F32)
    h = hg * (1.0 / (1.0 + jnp.exp(-hg))) * hu
    acc = acc + jnp.dot(h.astype(BF16), wsd_ref[...], preferred_element_type=F32)
    o_ref[...] = _layer_norm(alpha * x1_ref[...] + acc, g_ref[...], b_ref[...])

    @pl.when(b == pl.num_programs(0) - 1)
    def _():
        whole(lax.rem(b + 1, 3)).wait()
        whole(nxt).wait()


def _combine(plan, y, sk, x1, x1b, wsg, wsu, wsd, g, b, alpha):
    t, d = x1.shape
    nb = t // MOE_BLOCK
    row = lambda i, *_: (i, 0)
    const = lambda i, *_: (0, 0)
    grid_spec = pltpu.PrefetchScalarGridSpec(
        num_scalar_prefetch=1,
        grid=(nb,),
        in_specs=[pl.BlockSpec(memory_space=pl.ANY), pl.BlockSpec((2 * TOP_K, MOE_BLOCK), lambda i, *_: (0, i)),
                  pl.BlockSpec((MOE_BLOCK, d), row), pl.BlockSpec((MOE_BLOCK, d), row),
                  pl.BlockSpec(wsg.shape, const), pl.BlockSpec(wsu.shape, const), pl.BlockSpec(wsd.shape, const),
                  pl.BlockSpec(g.shape, const), pl.BlockSpec(b.shape, const)],
        out_specs=pl.BlockSpec((MOE_BLOCK, d), row),
        scratch_shapes=[pltpu.VMEM((3, BLOCK_ROWS, d), BF16), pltpu.SemaphoreType.DMA((3,))],
    )
    return pl.pallas_call(
        functools.partial(_combine_kernel, alpha=alpha),
        grid_spec=grid_spec,
        out_shape=jax.ShapeDtypeStruct((t, d), F32),
        compiler_params=_cparams(("arbitrary",)),
        name="combine",
    )(plan["get"], y, sk, x1, x1b, wsg, wsu, wsd, g, b)


def _moe(x1, x1b, sk, cnt, wg, wu, wd, wsg, wsu, wsd, g, b, alpha):
    t = x1.shape[0]
    nb = t // MOE_BLOCK
    tile = FFN_TILE if (TOP_K * t) // N_EXPERTS >= FFN_TILE else FFN_TILE_SMALL
    n_tiles = -(-(nb * BLOCK_ROWS) // tile) + N_EXPERTS
    n_rows = n_tiles * tile + 3 * BLOCK_ROWS
    plan = _moe_plan(cnt[:, 0, :N_EXPERTS], n_tiles, tile, n_tiles * tile)
    xs = _dispatch(plan, x1b, sk, n_rows)
    y = _ffn(plan, xs, wg, wu, wd, n_tiles, tile)
    return _combine(plan, y, sk, x1, x1b, wsg, wsu, wsd, g, b, alpha)


def _rope_cs(pos, dim):
    inv = ROPE_THETA ** (-jnp.arange(0, dim, 2, dtype=F32) / dim)
    ang = pos.astype(F32)[:, None] * inv[None, :]
    return jnp.cos(ang), jnp.sin(ang)


def _tables(pos, reps):
    n = pos.shape[0]
    c32, s32 = _rope_cs(pos, DIFF_QK)
    c16, s16 = _rope_cs(pos, MLA_ROPE)
    one = lambda w: jnp.ones((n, w), F32)
    zero = lambda w: jnp.zeros((n, w), F32)
    cd = jnp.concatenate([c32] * 4, axis=1)
    sd = jnp.concatenate([-s32, s32] * 2, axis=1)
    cq = jnp.concatenate([one(MLA_NOPE), c16, c16, one(32)], axis=1)
    sq = jnp.concatenate([zero(MLA_NOPE), -s16, s16, zero(32)], axis=1)
    ck = jnp.concatenate([c16, c16, zero(96)], axis=1)
    sk = jnp.concatenate([-s16, s16, zero(96)], axis=1)
    return tuple(jnp.tile(a, (reps, 1)) for a in (cd, sd, cq, sq, ck, sk))


def kernel(x_prompt, x_sample, cache_diff_k, cache_diff_v, cache_mla_ckv, cache_mla_kpe, w_in, diff_lambda, diff_subln_g, mla_q_norm_g, mla_w_uq, mla_kv_norm_g, mla_w_ukv, w_out, ln1_g, ln1_b, w_router, b_router, w_exp_gate, w_exp_up, w_exp_down, w_sh_gate, w_sh_up, w_sh_down, ln2_g, ln2_b):
    depth = w_in.shape[0]
    assert depth == 1
    d_model = x_prompt.shape[-1]
    alpha = (2.0 * depth) ** 0.25
    past_len = cache_diff_k.shape[2]
    layer = 0
    lambda_init = 0.8 - 0.6 * math.exp(-0.3 * layer)

    w_in_b = jnp.pad(w_in[layer], ((0, 0), (0, IN_PAD - IN_WIDTH))).astype(BF16)
    wuq = jnp.pad(mla_w_uq[layer], ((0, 0), (0, 0), (0, LANES - MLA_NOPE - MLA_ROPE)))
    wuq = wuq.reshape(Q_LORA, HEAD_W).astype(BF16)
    wukv = mla_w_ukv[layer]
    wuk = jnp.pad(wukv[:, :, :MLA_NOPE], ((0, 0), (0, 0), (0, LANES - MLA_NOPE))).reshape(KV_LORA, HEAD_W).astype(BF16)
    wuv = wukv[:, :, MLA_NOPE:].reshape(KV_LORA, HEAD_W).astype(BF16)
    place = jnp.pad(jnp.eye(MLA_ROPE, dtype=F32), ((0, 0), (MLA_NOPE, LANES - MLA_NOPE - MLA_ROPE)))
    place = jnp.tile(place, (1, MLA_HEADS)).astype(BF16)
    gq = mla_q_norm_g[layer].reshape(1, Q_LORA)
    gkv = mla_kv_norm_g[layer].reshape(1, KV_LORA)
    gsub = diff_subln_g[layer].reshape(1, LANES)
    wo = w_out[layer].astype(BF16)
    g1, b1 = ln1_g[layer].reshape(1, d_model), ln1_b[layer].reshape(1, d_model)
    g2, b2 = ln2_g[layer].reshape(1, d_model), ln2_b[layer].reshape(1, d_model)
    wr_t = w_router[layer].T
    wr_hi = wr_t.astype(BF16)
    wr_lo = (wr_t - wr_hi.astype(F32)).astype(BF16)
    br = b_router[layer].reshape(N_EXPERTS, 1)
    wg, wu, wd = (w[layer].astype(BF16) for w in (w_exp_gate, w_exp_up, w_exp_down))
    wsg, wsu, wsd = (w[layer].astype(BF16) for w in (w_sh_gate, w_sh_up, w_sh_down))
    lp = diff_lambda[layer].astype(F32)
    lam = jnp.exp(jnp.sum(lp[0] * lp[1])) - jnp.exp(jnp.sum(lp[2] * lp[3])) + lambda_init
    lam = lam.reshape(1, 1)

    def group(x, pos, past, tm, tq, tk, tkm, hps):
        b, s, _ = x.shape
        t = b * s
        tm = min(tm, t)
        xf = x.reshape(t, d_model)
        reps = max(1, tm // s)
        tables = _tables(pos, reps)
        n_pat = (s * reps) // tm
        qd, kd32, kd16, vd32, vd16, ckv, kpe, qm = _proj(xf, w_in_b, wuq, gq, gkv, tables, tm, n_pat)
        if past is None:
            k_d, v_d = kd16.reshape(b, s, HEAD_W), vd16.reshape(b, s, HEAD_W)
            k_m, v_m = _kvup(ckv, kpe, wuk, place, wuv, min(1024, t))
            k_m, v_m = k_m.reshape(b, s, HEAD_W), v_m.reshape(b, s, HEAD_W)
            q_pos0 = 0
        else:
            pk, pv, pc, pp = past
            k_d, v_d = _catcast(pk, pv, kd16.reshape(b, s, HEAD_W), vd16.reshape(b, s, HEAD_W), tk)
            k_m, v_m = _kvup_past(pc, pp, ckv.reshape(b, s, KV_LORA), kpe.reshape(b, s, MLA_ROPE),
                                  wuk, place, wuv, tk)
            q_pos0 = past_len
        o_d = _flash(qd.reshape(b, s, HEAD_W), k_d, v_d, lam, gsub, n_comp=2, hps=hps, tq=tq, tk=tk, tkm=tkm,
                     q_pos0=q_pos0, out_scale=1.0 - lambda_init)
        o_m = _flash(qm.reshape(b, s, HEAD_W), k_m, v_m, lam, gsub,
                     n_comp=1, hps=hps, tq=tq, tk=tk, tkm=tkm, q_pos0=q_pos0, out_scale=1.0)
        x1, x1b, slots, cnt = _mix(o_d.reshape(t, HEAD_W), o_m.reshape(t, HEAD_W), xf, wo, g1, b1, wr_hi, wr_lo, br,
                                   tm, alpha)
        y = _moe(x1, x1b, slots, cnt, wg, wu, wd, wsg, wsu, wsd, g2, b2, alpha)
        rows = (kd32.reshape(1, b, s, DIFF_HEADS, LANES), vd32.reshape(1, b, s, DIFF_HEADS, LANES),
                ckv.reshape(1, b, s, KV_LORA), kpe.reshape(1, b, s, MLA_ROPE))
        return y.reshape(b, s, d_model), rows

    s_p = x_prompt.shape[1]
    s_s = x_sample.shape[1]
    pos_p = jnp.arange(s_p, dtype=jnp.int32)
    pos_s = past_len + jnp.arange(s_s, dtype=jnp.int32)
    y_p, r_p = group(x_prompt, pos_p, None, 512, 512, 512, 256, 1)
    past = (cache_diff_k[layer], cache_diff_v[layer], cache_mla_ckv[layer], cache_mla_kpe[layer])
    y_s, r_s = group(x_sample, pos_s, past, 512, s_s, 512, 256, DIFF_HEADS)
    return (y_p, y_s) + r_p + r_s
```

```python
import functools
import math

import jax
import jax.numpy as jnp
from jax import lax
from jax.experimental import pallas as pl
from jax.experimental.pallas import tpu as pltpu

F32 = jnp.float32
BF16 = jnp.bfloat16

LANES = 128
VMEM_LIMIT = 52 * 1024 * 1024

CHUNK = 64
CHUNK_SHIFT = 6
ROPE_THETA = 10000.0
LN_EPS = 1e-5
RMS_EPS = 1e-6
LOG2E = 1.4426950408889634
NEG_BIG = -1e30
FLASH_SUB = 256

DIFF_HEADS = 4
DIFF_QK = 64
MLA_HEADS = 4
MLA_NOPE = 64
MLA_ROPE = 32
MLA_V = 128
Q_LORA = 384
KV_LORA = 256
N_EXPERTS = 64
N_GROUPS = 8
GROUP_SIZE = N_EXPERTS // N_GROUPS
TOPK_GROUPS = 4
TOP_K = 8
ROUTED_SCALE = 2.5
EXPERT_DIM = 256

MOE_BLOCK = 256
SEG_ALIGN = 16
ORD_SUB = 256
DISP_SUB = 1024
BLOCK_ROWS = -(-(TOP_K * MOE_BLOCK + N_EXPERTS * (SEG_ALIGN - 1)) // DISP_SUB) * DISP_SUB
BLOCK_CHUNKS = BLOCK_ROWS // SEG_ALIGN
FFN_TILE = 1024
FFN_TILE_SMALL = 256

DQ_W = DIFF_HEADS * 2 * DIFF_QK
HEAD_W = DIFF_HEADS * LANES
IN_WIDTH = 3 * DQ_W + Q_LORA + KV_LORA + MLA_ROPE
IN_PAD = 2304
OFF_DK, OFF_DV, OFF_CQ, OFF_CKV, OFF_KPE = 512, 1024, 1536, 1920, 2176


def _cparams(sem):
    return pltpu.CompilerParams(dimension_semantics=sem, vmem_limit_bytes=VMEM_LIMIT)


def _rms(x, g):
    return x * lax.rsqrt(jnp.mean(x * x, axis=-1, keepdims=True) + RMS_EPS) * g


def _layer_norm(x, g, b):
    mu = jnp.mean(x, axis=-1, keepdims=True)
    xc = x - mu
    var = jnp.mean(xc * xc, axis=-1, keepdims=True)
    return xc * lax.rsqrt(var + LN_EPS) * g + b


def _proj_kernel(x_ref, w_ref, wuq_ref, gq_ref, gkv_ref, cd_ref, sd_ref, cq_ref, sq_ref, ck_ref, sk_ref,
                 qd_ref, kd32_ref, kd16_ref, vd32_ref, vd16_ref, ckv_ref, kpe_ref, qm_ref,
                 *, scale_d, scale_m):
    tm = x_ref.shape[0]
    x = x_ref[...].astype(BF16)
    proj = jnp.dot(x, w_ref[...], preferred_element_type=F32)
    lane = lax.broadcasted_iota(jnp.int32, (tm, LANES), 1)

    first_d = (lane & 63) < 32
    cd = cd_ref[...]
    sd = sd_ref[...]

    def rope_d(blk):
        rot = jnp.where(first_d, pltpu.roll(blk, LANES - 32, 1), pltpu.roll(blk, 32, 1))
        return blk * cd + rot * sd

    for j in range(DIFF_HEADS):
        sl = slice(j * LANES, (j + 1) * LANES)
        qd_ref[:, sl] = (rope_d(proj[:, sl]) * scale_d).astype(BF16)
        kr = rope_d(proj[:, OFF_DK + j * LANES:OFF_DK + (j + 1) * LANES])
        kd32_ref[pl.ds(j, tm, stride=DIFF_HEADS), :] = kr
        kd16_ref[:, sl] = kr.astype(BF16)
        dv = proj[:, OFF_DV + j * LANES:OFF_DV + (j + 1) * LANES]
        vd32_ref[pl.ds(j, tm, stride=DIFF_HEADS), :] = dv
        vd16_ref[:, sl] = dv.astype(BF16)

    cqn = _rms(proj[:, OFF_CQ:OFF_CQ + Q_LORA], gq_ref[...])
    q = jnp.dot(cqn.astype(BF16), wuq_ref[...], preferred_element_type=F32)
    cq = cq_ref[...]
    sq = sq_ref[...]
    first_q = lane < (MLA_NOPE + MLA_ROPE // 2)
    for h in range(MLA_HEADS):
        sl = slice(h * LANES, (h + 1) * LANES)
        blk = q[:, sl]
        rot = jnp.where(first_q, pltpu.roll(blk, LANES - 16, 1), pltpu.roll(blk, 16, 1))
        qm_ref[:, sl] = ((blk * cq + rot * sq) * scale_m).astype(BF16)

    ckv_ref[...] = _rms(proj[:, OFF_CKV:OFF_CKV + KV_LORA], gkv_ref[...])

    kb = proj[:, OFF_KPE:OFF_KPE + LANES]
    rot = jnp.where(lane < 16, pltpu.roll(kb, LANES - 16, 1), pltpu.roll(kb, 16, 1))
    kpe_ref[...] = (kb * ck_ref[...] + rot * sk_ref[...])[:, :MLA_ROPE]


def _proj(x, w_in, wuq, gq, gkv, tables, tm, n_pat):
    t = x.shape[0]
    row = lambda i: (i, 0)
    const = lambda i: (0, 0)
    pat = lambda i: (i % n_pat, 0)
    tab_spec = pl.BlockSpec((tm, LANES), pat)
    out_w = lambda w, dt: jax.ShapeDtypeStruct((t, w), dt)
    cache_spec = pl.BlockSpec((tm * DIFF_HEADS, LANES), row)
    cache_shape = jax.ShapeDtypeStruct((t * DIFF_HEADS, LANES), F32)
    return pl.pallas_call(
        functools.partial(_proj_kernel, scale_d=LOG2E * DIFF_QK ** -0.5,
                          scale_m=LOG2E * (MLA_NOPE + MLA_ROPE) ** -0.5),
        grid=(t // tm,),
        in_specs=[pl.BlockSpec((tm, x.shape[1]), row),
                  pl.BlockSpec(w_in.shape, const), pl.BlockSpec(wuq.shape, const),
                  pl.BlockSpec(gq.shape, const), pl.BlockSpec(gkv.shape, const)] + [tab_spec] * 6,
        out_specs=[pl.BlockSpec((tm, HEAD_W), row), cache_spec, pl.BlockSpec((tm, HEAD_W), row), cache_spec,
                   pl.BlockSpec((tm, HEAD_W), row), pl.BlockSpec((tm, KV_LORA), row),
                   pl.BlockSpec((tm, MLA_ROPE), row), pl.BlockSpec((tm, HEAD_W), row)],
        out_shape=[out_w(HEAD_W, BF16), cache_shape, out_w(HEAD_W, BF16), cache_shape,
                   out_w(HEAD_W, BF16), out_w(KV_LORA, F32), out_w(MLA_ROPE, F32), out_w(HEAD_W, BF16)],
        compiler_params=_cparams(("parallel",)),
        name="proj",
    )(x, w_in, wuq, gq, gkv, *tables)


def _kvup_kernel(ckv_ref, kpe_ref, wuk_ref, place_ref, wuv_ref, k_ref, v_ref):
    c = ckv_ref[...].astype(BF16)
    k = jnp.dot(c, wuk_ref[...], preferred_element_type=F32)
    k = k + jnp.dot(kpe_ref[...].astype(BF16), place_ref[...], preferred_element_type=F32)
    k_ref[...] = k.astype(BF16)
    v_ref[...] = jnp.dot(c, wuv_ref[...], preferred_element_type=F32).astype(BF16)


def _kvup(ckv, kpe, wuk, place, wuv, tm):
    r = ckv.shape[0]
    row = lambda i: (i, 0)
    const = lambda i: (0, 0)
    return pl.pallas_call(
        _kvup_kernel,
        grid=(r // tm,),
        in_specs=[pl.BlockSpec((tm, KV_LORA), row), pl.BlockSpec((tm, MLA_ROPE), row),
                  pl.BlockSpec(wuk.shape, const), pl.BlockSpec(place.shape, const), pl.BlockSpec(wuv.shape, const)],
        out_specs=[pl.BlockSpec((tm, HEAD_W), row)] * 2,
        out_shape=[jax.ShapeDtypeStruct((r, HEAD_W), BF16)] * 2,
        compiler_params=_cparams(("parallel",)),
        name="kvup",
    )(ckv, kpe, wuk, place, wuv)


def _catcast_kernel(ck_ref, cv_ref, nk_ref, nv_ref, k_ref, v_ref, *, n_past, s_new, tr):
    j = pl.program_id(1)

    @pl.when(j < n_past)
    def _():
        for h in range(DIFF_HEADS):
            sl = slice(h * LANES, (h + 1) * LANES)
            k_ref[:, sl] = ck_ref[pl.ds(h, tr, stride=DIFF_HEADS), :].astype(BF16)
            v_ref[:, sl] = cv_ref[pl.ds(h, tr, stride=DIFF_HEADS), :].astype(BF16)

    @pl.when(j >= n_past)
    def _():
        k_ref[...] = jnp.zeros(k_ref.shape, BF16)
        v_ref[...] = jnp.zeros(v_ref.shape, BF16)
        k_ref[0:s_new, :] = nk_ref[...]
        v_ref[0:s_new, :] = nv_ref[...]


def _catcast(cache_k, cache_v, new_k, new_v, tr):
    b, p = cache_k.shape[:2]
    s = new_k.shape[1]
    assert p % tr == 0 and s <= tr
    n_past = p // tr
    past = pl.BlockSpec((None, tr * DIFF_HEADS, LANES), lambda bi, j: (bi, jnp.minimum(j, n_past - 1), 0))
    new = pl.BlockSpec((None, s, HEAD_W), lambda bi, j: (bi, 0, 0))
    out = pl.BlockSpec((None, tr, HEAD_W), lambda bi, j: (bi, j, 0))
    return pl.pallas_call(
        functools.partial(_catcast_kernel, n_past=n_past, s_new=s, tr=tr),
        grid=(b, n_past + 1),
        in_specs=[past, past, new, new],
        out_specs=[out, out],
        out_shape=[jax.ShapeDtypeStruct((b, p + tr, HEAD_W), BF16)] * 2,
        compiler_params=_cparams(("parallel", "arbitrary")),
        name="catcast",
    )(cache_k.reshape(b, p * DIFF_HEADS, LANES), cache_v.reshape(b, p * DIFF_HEADS, LANES), new_k, new_v)


def _kvup_past_kernel(pc_ref, pp_ref, nc_ref, np_ref, wuk_ref, place_ref, wuv_ref, k_ref, v_ref, c_ref, r_ref,
                      *, n_past, s_new):
    j = pl.program_id(1)

    @pl.when(j < n_past)
    def _():
        c_ref[...] = pc_ref[...]
        r_ref[...] = pp_ref[...]

    @pl.when(j >= n_past)
    def _():
        c_ref[...] = jnp.zeros(c_ref.shape, F32)
        r_ref[...] = jnp.zeros(r_ref.shape, F32)
        c_ref[0:s_new, :] = nc_ref[...]
        r_ref[0:s_new, :] = np_ref[...]

    _kvup_kernel(c_ref, r_ref, wuk_ref, place_ref, wuv_ref, k_ref, v_ref)


def _kvup_past(cache_c, cache_r, new_c, new_r, wuk, place, wuv, tr):
    b, p = cache_c.shape[:2]
    s = new_c.shape[1]
    assert p % tr == 0 and s <= tr
    n_past = p // tr
    clamp = lambda bi, j: (bi, jnp.minimum(j, n_past - 1), 0)
    first = lambda bi, j: (bi, 0, 0)
    const = lambda bi, j: (0, 0)
    out = pl.BlockSpec((None, tr, HEAD_W), lambda bi, j: (bi, j, 0))
    return pl.pallas_call(
        functools.partial(_kvup_past_kernel, n_past=n_past, s_new=s),
        grid=(b, n_past + 1),
        in_specs=[pl.BlockSpec((None, tr, KV_LORA), clamp), pl.BlockSpec((None, tr, MLA_ROPE), clamp),
                  pl.BlockSpec((None, s, KV_LORA), first), pl.BlockSpec((None, s, MLA_ROPE), first),
                  pl.BlockSpec(wuk.shape, const), pl.BlockSpec(place.shape, const), pl.BlockSpec(wuv.shape, const)],
        out_specs=[out, out],
        out_shape=[jax.ShapeDtypeStruct((b, p + tr, HEAD_W), BF16)] * 2,
        scratch_shapes=[pltpu.VMEM((tr, KV_LORA), F32), pltpu.VMEM((tr, MLA_ROPE), F32)],
        compiler_params=_cparams(("parallel", "arbitrary")),
        name="kvup_past",
    )(cache_c, cache_r, new_c, new_r, wuk, place, wuv)


def _flash_kernel(lam_ref, q_ref, k_ref, v_ref, g_ref, o_ref, qs_ref, vx_ref, m_ref, acc_ref,
                  *, n_comp, hps, tq, tk, tkm, sub, sk, q_pos0, out_scale, tiles):
    rows = n_comp * tq
    blocks = [(h, r0) for h in range(hps) for r0 in range(0, rows, sub)]

    @pl.when(pl.program_id(2) == 0)
    def _():
        for h in range(hps):
            vx_ref[:, 2 * h * LANES:(2 * h + 1) * LANES] = v_ref[:, h * LANES:(h + 1) * LANES]
            vx_ref[:, (2 * h + 1) * LANES:(2 * h + 2) * LANES] = jnp.ones((sk, LANES), BF16)

    def absorb(s, start, h, r0):
        rs = slice(h * rows + r0, h * rows + r0 + sub)
        width = s.shape[1]
        vx = vx_ref[pl.ds(start, width), 2 * h * LANES:(2 * h + 2) * LANES]
        m_prev = m_ref[rs, :]
        m_new = jnp.maximum(m_prev, jnp.max(s, axis=-1, keepdims=True))
        alpha = jnp.exp2(m_prev - m_new)
        p = jnp.exp2(s - jnp.concatenate([m_new] * (width // LANES), axis=1))
        pv = jnp.dot(p.astype(BF16), vx, preferred_element_type=F32)
        acc_ref[rs, :] = jnp.concatenate([alpha, alpha], axis=1) * acc_ref[rs, :] + pv
        m_ref[rs, :] = m_new

    def scores(start, width, h, r0):
        rs = slice(h * rows + r0, h * rows + r0 + sub)
        k = k_ref[pl.ds(start, width), h * LANES:(h + 1) * LANES]
        return lax.dot_general(qs_ref[rs, :], k, (((1,), (1,)), ((), ())), preferred_element_type=F32)

    def one_tile(qt, q0, static):
        for h in range(hps):
            q = q_ref[qt * tq:(qt + 1) * tq, h * LANES:(h + 1) * LANES]
            if n_comp == 2:
                lane = lax.broadcasted_iota(jnp.int32, (tq, LANES), 1)
                zero = jnp.zeros_like(q)
                qs_ref[h * rows:h * rows + tq, :] = jnp.where(lane < DIFF_QK, q, zero)
                qs_ref[h * rows + tq:(h + 1) * rows, :] = jnp.where(lane >= DIFF_QK, q, zero)
            else:
                qs_ref[h * rows:(h + 1) * rows, :] = q
        m_ref[...] = jnp.full(m_ref.shape, NEG_BIG, F32)
        acc_ref[...] = jnp.zeros(acc_ref.shape, F32)

        if static:
            for j in range(q0 // tk):
                for h, r0 in blocks:
                    absorb(scores(j * tk, tk, h, r0), j * tk, h, r0)
            qc = lax.broadcasted_iota(jnp.int32, (sub, sub), 0) >> CHUNK_SHIFT
            kc = lax.broadcasted_iota(jnp.int32, (sub, sub), 1) >> CHUNK_SHIFT
            for h, r0 in blocks:
                a = r0 % tq
                s = scores(q0, a + sub, h, r0)
                last = jnp.where(kc <= qc, s[:, a:a + sub], NEG_BIG)
                absorb(jnp.concatenate([s[:, 0:a], last], axis=1) if a else last, q0, h, r0)
        else:
            lo_vis = jnp.minimum(((q0 >> CHUNK_SHIFT) + 1) << CHUNK_SHIFT, sk)
            hi_vis = jnp.minimum((((q0 + tq - 1) >> CHUNK_SHIFT) + 1) << CHUNK_SHIFT, sk)
            n_full = lo_vis // tk

            def masked_scores(start, width, h, r0):
                r = (lax.broadcasted_iota(jnp.int32, (sub, width), 0) + r0) & (tq - 1)
                c = lax.broadcasted_iota(jnp.int32, (sub, width), 1)
                ok = ((start + c) >> CHUNK_SHIFT) <= ((q0 + r) >> CHUNK_SHIFT)
                return jnp.where(ok, scores(start, width, h, r0), NEG_BIG)

            def full_body(j, carry):
                start = pl.multiple_of(j * tk, tk)
                for h, r0 in blocks:
                    absorb(scores(start, tk, h, r0), start, h, r0)
                return carry

            def masked_body(j, carry):
                start = pl.multiple_of(j * tkm, tkm)
                for h, r0 in blocks:
                    absorb(masked_scores(start, tkm, h, r0), start, h, r0)
                return carry

            lax.fori_loop(0, n_full, full_body, 0)
            lax.fori_loop(n_full * (tk // tkm), (hi_vis + tkm - 1) // tkm, masked_body, 0)

        for h in range(hps):
            hr = slice(h * rows, (h + 1) * rows)
            o = acc_ref[hr, 0:LANES] / acc_ref[hr, LANES:2 * LANES]
            if n_comp == 2:
                o = o[0:tq, :] - lam_ref[0, 0] * o[tq:rows, :]
                o = _rms(o, g_ref[...]) * out_scale
            o_ref[qt * tq:(qt + 1) * tq, h * LANES:(h + 1) * LANES] = o.astype(BF16)

    if tiles > 1:
        for qt in range(tiles):
            one_tile(qt, q_pos0 + qt * tq, True)
    else:
        one_tile(0, q_pos0 + pl.program_id(2) * tq, False)


def _flash(q, k, v, lam, g, *, n_comp, hps, tq, tk, tkm, q_pos0, out_scale):
    b, sq, _ = q.shape
    sk = k.shape[1]
    rows = n_comp * tq
    sub = min(rows, FLASH_SUB)
    assert sq % tq == 0 and sk % tk == 0 and tk % tkm == 0 and tq & (tq - 1) == 0 and rows % sub == 0
    assert DIFF_HEADS % hps == 0
    aligned = (q_pos0 % tq == 0 and tq == tk and q_pos0 + sq <= sk and sub % CHUNK == 0 and sub % LANES == 0
               and sq > tq)
    tiles = sq // tq if aligned else 1
    hw = hps * LANES
    return pl.pallas_call(
        functools.partial(_flash_kernel, n_comp=n_comp, hps=hps, tq=tq, tk=tk, tkm=tkm, sub=sub, sk=sk,
                          q_pos0=q_pos0, out_scale=out_scale, tiles=tiles),
        grid=(b, DIFF_HEADS // hps, sq // (tq * tiles)),
        in_specs=[pl.BlockSpec(memory_space=pltpu.SMEM),
                  pl.BlockSpec((None, tq * tiles, hw), lambda bi, h, i: (bi, i, h)),
                  pl.BlockSpec((None, sk, hw), lambda bi, h, i: (bi, 0, h)),
                  pl.BlockSpec((None, sk, hw), lambda bi, h, i: (bi, 0, h)),
                  pl.BlockSpec((1, LANES), lambda bi, h, i: (0, 0))],
        out_specs=pl.BlockSpec((None, tq * tiles, hw), lambda bi, h, i: (bi, i, h)),
        out_shape=jax.ShapeDtypeStruct((b, sq, HEAD_W), BF16),
        scratch_shapes=[pltpu.VMEM((hps * rows, LANES), BF16), pltpu.VMEM((sk, 2 * hw), BF16),
                        pltpu.VMEM((hps * rows, LANES), F32), pltpu.VMEM((hps * rows, 2 * LANES), F32)],
        compiler_params=_cparams(("parallel", "parallel", "arbitrary")),
        name="flash_diff" if n_comp == 2 else "flash_mla",
    )(lam, q, k, v, g)


def _sublane_max(x):
    return jnp.max(x, axis=0, keepdims=True)


def _sublane_min(x):
    return jnp.min(x, axis=0, keepdims=True)


def _route_t(scores, biased):
    tm = scores[0].shape[1]
    sub = lax.broadcasted_iota(jnp.int32, (GROUP_SIZE, tm), 0)
    neg_inf = jnp.float32(-jnp.inf)
    gs = jnp.zeros((N_GROUPS, tm), F32)
    for g in range(N_GROUPS):
        bg = biased[g]
        m1 = _sublane_max(bg)
        i1 = _sublane_min(jnp.where(bg == m1, sub, GROUP_SIZE))
        m2 = _sublane_max(jnp.where(sub == i1, neg_inf, bg))
        gs = jnp.where(sub == g, m1 + m2, gs)
    keep = jnp.zeros((N_GROUPS, tm), jnp.bool_)
    cur = gs
    for _ in range(TOPK_GROUPS):
        mx = _sublane_max(cur)
        fi = _sublane_min(jnp.where(cur == mx, sub, N_GROUPS))
        hit = sub == fi
        keep = jnp.logical_or(keep, hit)
        cur = jnp.where(hit, neg_inf, cur)
    keep_f = jnp.where(keep, 1.0, 0.0)
    cand = []
    for g in range(N_GROUPS):
        kg = _sublane_max(jnp.where(sub == g, keep_f, 0.0)) > 0.5
        cand.append(jnp.where(kg, biased[g], neg_inf))
    chosen = [jnp.zeros((GROUP_SIZE, tm), jnp.bool_) for _ in range(N_GROUPS)]
    picks = []
    for _ in range(TOP_K):
        mx = cand[0]
        for g in range(1, N_GROUPS):
            mx = jnp.maximum(mx, cand[g])
        mx = _sublane_max(mx)
        fi = jnp.where(cand[0] == mx, sub, N_EXPERTS)
        for g in range(1, N_GROUPS):
            fi = jnp.minimum(fi, jnp.where(cand[g] == mx, sub + g * GROUP_SIZE, N_EXPERTS))
        fi = _sublane_min(fi)
        picks.append(fi)
        for g in range(N_GROUPS):
            hit = (sub + g * GROUP_SIZE) == fi
            chosen[g] = jnp.logical_or(chosen[g], hit)
            cand[g] = jnp.where(hit, neg_inf, cand[g])
    w = [jnp.where(chosen[g], scores[g], 0.0) for g in range(N_GROUPS)]
    tot = w[0]
    for g in range(1, N_GROUPS):
        tot = tot + w[g]
    tot = jnp.sum(tot, axis=0, keepdims=True)
    return [wg / tot * ROUTED_SCALE for wg in w], chosen, picks


def _mix_kernel(od_ref, om_ref, x_ref, wo_ref, g_ref, b_ref, wr_ref, wrl_ref, br_ref, before_ref, below_ref, eye_ref,
                x1_ref, x1b_ref, sk_ref, cnt_ref, *, alpha):
    mix = jnp.dot(od_ref[...], wo_ref[0:HEAD_W, :], preferred_element_type=F32)
    mix = mix + jnp.dot(om_ref[...], wo_ref[HEAD_W:2 * HEAD_W, :], preferred_element_type=F32)
    x1 = _layer_norm(alpha * x_ref[...] + mix, g_ref[...], b_ref[...])
    x_hi = x1.astype(BF16)
    x1_ref[...] = x1
    x1b_ref[...] = x_hi
    x_lo = (x1 - x_hi.astype(F32)).astype(BF16)
    nt = (((1,), (1,)), ((), ()))
    logits = (lax.dot_general(wr_ref[...], x_hi, nt, preferred_element_type=F32)
              + lax.dot_general(wr_ref[...], x_lo, nt, preferred_element_type=F32)
              + lax.dot_general(wrl_ref[...], x_hi, nt, preferred_element_type=F32))
    sc = 1.0 / (1.0 + jnp.exp(-logits))
    bi = sc + br_ref[...]
    scores = [sc[g * GROUP_SIZE:(g + 1) * GROUP_SIZE, :] for g in range(N_GROUPS)]
    biased = [bi[g * GROUP_SIZE:(g + 1) * GROUP_SIZE, :] for g in range(N_GROUPS)]
    gates, chosen, picks = _route_t(scores, biased)
    tm = x1.shape[0]
    ch = jnp.concatenate([jnp.where(c, 1.0, 0.0) for c in chosen], axis=0)
    gate_all = jnp.concatenate(gates, axis=0)
    sub = lax.broadcasted_iota(jnp.int32, (N_EXPERTS, MOE_BLOCK), 0)
    for blk in range(tm // MOE_BLOCK):
        ls = slice(blk * MOE_BLOCK, (blk + 1) * MOE_BLOCK)
        chb = ch[:, ls]
        rank = jnp.dot(chb.astype(BF16), before_ref[...], preferred_element_type=F32)
        cnt = jnp.sum(chb, axis=1, keepdims=True)
        seg = jnp.floor((cnt + (SEG_ALIGN - 1.0)) * (1.0 / SEG_ALIGN)) * SEG_ALIGN
        seg_b = jnp.broadcast_to(seg, (N_EXPERTS, LANES))
        off = jnp.dot(below_ref[...], seg_b, precision=lax.Precision.HIGHEST, preferred_element_type=F32)
        slot = jnp.concatenate([off] * (MOE_BLOCK // LANES), axis=1) + rank
        cnt_ref[blk] = jnp.sum(seg_b * eye_ref[...], axis=0, keepdims=True)
        gb = gate_all[:, ls]
        for k in range(TOP_K):
            hit = sub == picks[k][:, ls]
            sk_ref[k:k + 1, ls] = jnp.sum(jnp.where(hit, slot, 0.0), axis=0, keepdims=True)
            sk_ref[TOP_K + k:TOP_K + k + 1, ls] = jnp.sum(jnp.where(hit, gb, 0.0), axis=0, keepdims=True)


def _mix(od, om, x, wo, g, b, wr_hi, wr_lo, br, tm, alpha):
    t = x.shape[0]
    d = x.shape[1]
    nblk = tm // MOE_BLOCK
    row = lambda i: (i, 0)
    const = lambda i: (0, 0)
    idx = jnp.arange(MOE_BLOCK)
    before = (idx[:, None] < idx[None, :]).astype(BF16)
    ide = jnp.arange(N_EXPERTS)
    below = (ide[None, :] < ide[:, None]).astype(F32)
    eye = (ide[:, None] == jnp.arange(LANES)[None, :]).astype(F32)
    return pl.pallas_call(
        functools.partial(_mix_kernel, alpha=alpha),
        grid=(t // tm,),
        in_specs=[pl.BlockSpec((tm, HEAD_W), row), pl.BlockSpec((tm, HEAD_W), row), pl.BlockSpec((tm, d), row),
                  pl.BlockSpec(wo.shape, const), pl.BlockSpec(g.shape, const), pl.BlockSpec(b.shape, const),
                  pl.BlockSpec(wr_hi.shape, const), pl.BlockSpec(wr_lo.shape, const), pl.BlockSpec(br.shape, const),
                  pl.BlockSpec(before.shape, const), pl.BlockSpec(below.shape, const), pl.BlockSpec(eye.shape, const)],
        out_specs=[pl.BlockSpec((tm, d), row), pl.BlockSpec((tm, d), row),
                   pl.BlockSpec((2 * TOP_K, tm), lambda i: (0, i)),
                   pl.BlockSpec((nblk, 1, LANES), lambda i: (i, 0, 0))],
        out_shape=[jax.ShapeDtypeStruct((t, d), F32), jax.ShapeDtypeStruct((t, d), BF16),
                   jax.ShapeDtypeStruct((2 * TOP_K, t), F32),
                   jax.ShapeDtypeStruct((t // MOE_BLOCK, 1, LANES), F32)],
        compiler_params=_cparams(("parallel",)),
        name="mix",
    )(od, om, x, wo, g, b, wr_hi, wr_lo, br, before, below, eye)


def _moe_plan(cnt, n_tiles, tile, trash_row):
    seg = cnt.astype(jnp.int32)
    nb = seg.shape[0]
    before_blocks = jnp.cumsum(seg, axis=0) - seg
    length = jnp.sum(seg, axis=0)
    padded = -(-length // tile) * tile
    ends = jnp.cumsum(padded)
    start = ends - padded
    dst = start[None, :] + before_blocks
    off = (jnp.cumsum(seg, axis=1) - seg) // SEG_ALIGN
    tot = jnp.sum(seg, axis=1) // SEG_ALIGN
    chunk = jnp.arange(BLOCK_CHUNKS, dtype=jnp.int32)
    base = dst - SEG_ALIGN * off
    step = jnp.concatenate([base[:, :1], base[:, 1:] - base[:, :-1]], axis=1)
    started = (off[:, None, :] <= chunk[None, :, None]).astype(jnp.int32)
    row = jnp.sum(started * step[:, None, :], axis=2) + SEG_ALIGN * chunk[None, :]
    valid = chunk[None, :] < tot[:, None]
    trash = trash_row + ((jnp.arange(nb, dtype=jnp.int32) % 3) * BLOCK_ROWS)[:, None] + SEG_ALIGN * chunk[None, :]
    first = trash_row + 2 * BLOCK_ROWS + SEG_ALIGN * chunk[None, :]
    put = jnp.concatenate([first, jnp.where(valid, row, trash)], axis=0).reshape(-1)
    get = jnp.concatenate([jnp.where(valid, row, 0), jnp.zeros((2, BLOCK_CHUNKS), jnp.int32)], axis=0).reshape(-1)
    n_used = ends[-1] // tile
    tile_start = jnp.arange(n_tiles, dtype=jnp.int32) * tile
    tile_e = jnp.sum((ends[None, :] <= tile_start[:, None]).astype(jnp.int32), axis=1)
    last_e = tile_e[jnp.maximum(n_used - 1, 0)]
    tile_e = jnp.where(jnp.arange(n_tiles) < n_used, tile_e, last_e)
    tail_start = start + length
    tail_nch = (padded - length) // SEG_ALIGN
    return dict(put=put, get=get, n_used=n_used.reshape(1), tile_e=tile_e, tail_start=tail_start,
                tail_nch=tail_nch, tail_tot=jnp.sum(tail_nch).reshape(1))


def _drain(copy, n):
    def body(i, c):
        copy.wait()
        return c
    lax.fori_loop(0, n, body, 0)


def _order_rows(sk, j0, vals):
    rows = lax.broadcasted_iota(jnp.int32, (ORD_SUB, MOE_BLOCK), 0).astype(F32).astype(BF16)
    out = jnp.zeros((ORD_SUB, MOE_BLOCK), BF16)
    for k in range(TOP_K):
        rel = (sk[k:k + 1, :] - j0).astype(BF16)
        out = jnp.where(rows == rel, vals[k], out)
    return out


def _dispatch_kernel(put_ref, tstart_ref, tnch_ref, ttot_ref, x_ref, sk_ref, xs_ref, buf_ref, zero_ref, sem, tail_sem):
    b = pl.program_id(0)
    last = pl.num_programs(0) - 1
    slot = lax.rem(b, 3)
    send = lax.rem(b + 2, 3)

    def whole(s):
        return pltpu.make_async_copy(buf_ref.at[s], xs_ref.at[pl.ds(0, BLOCK_ROWS), :], sem.at[s])

    @pl.when(b == 0)
    def _():
        buf_ref[...] = jnp.zeros(buf_ref.shape, BF16)

    @pl.when(b >= 2)
    def _():
        whole(slot).wait()

    x = x_ref[...]
    sk = sk_ref[0:TOP_K, :]
    ones = [jnp.ones((1, MOE_BLOCK), BF16)] * TOP_K
    base = b * BLOCK_CHUNKS
    for sub in range(BLOCK_ROWS // DISP_SUB):
        j0 = sub * DISP_SUB
        sel = jnp.concatenate([_order_rows(sk, float(j0 + i * ORD_SUB), ones) for i in range(DISP_SUB // ORD_SUB)],
                              axis=0)
        buf_ref[slot, j0:j0 + DISP_SUB, :] = jnp.dot(sel, x, preferred_element_type=F32).astype(BF16)
        for c in range(sub * (DISP_SUB // SEG_ALIGN), (sub + 1) * (DISP_SUB // SEG_ALIGN)):
            pltpu.make_async_copy(
                buf_ref.at[send, c * SEG_ALIGN:(c + 1) * SEG_ALIGN, :],
                xs_ref.at[pl.ds(pl.multiple_of(put_ref[base + c], SEG_ALIGN), SEG_ALIGN), :], sem.at[send]).start()

    @pl.when(b == last)
    def _():
        whole(send).wait()

        @pl.when(b >= 1)
        def _():
            whole(lax.rem(b + 1, 3)).wait()

        zero_ref[...] = jnp.zeros(zero_ref.shape, BF16)

        def tail_copy(dst_row):
            return pltpu.make_async_copy(zero_ref, xs_ref.at[pl.ds(pl.multiple_of(dst_row, SEG_ALIGN), SEG_ALIGN), :],
                                         tail_sem.at[0])

        def tail_expert(e, carry):
            def tail_chunk(c, carry2):
                tail_copy(tstart_ref[e] + c * SEG_ALIGN).start()
                return carry2
            lax.fori_loop(0, tnch_ref[e], tail_chunk, 0)
            return carry

        lax.fori_loop(0, N_EXPERTS, tail_expert, 0)
        _drain(tail_copy(0), ttot_ref[0])


def _dispatch(plan, x1b, sk, n_rows):
    t, d = x1b.shape
    nb = t // MOE_BLOCK
    grid_spec = pltpu.PrefetchScalarGridSpec(
        num_scalar_prefetch=4,
        grid=(nb + 1,),
        in_specs=[pl.BlockSpec((MOE_BLOCK, d), lambda b, *_: (jnp.minimum(b, nb - 1), 0)),
                  pl.BlockSpec((2 * TOP_K, MOE_BLOCK), lambda b, *_: (0, jnp.minimum(b, nb - 1)))],
        out_specs=pl.BlockSpec(memory_space=pl.ANY),
        scratch_shapes=[pltpu.VMEM((3, BLOCK_ROWS, d), BF16), pltpu.VMEM((SEG_ALIGN, d), BF16),
                        pltpu.SemaphoreType.DMA((3,)), pltpu.SemaphoreType.DMA((1,))],
    )
    return pl.pallas_call(
        _dispatch_kernel,
        grid_spec=grid_spec,
        out_shape=jax.ShapeDtypeStruct((n_rows, d), BF16),
        compiler_params=_cparams(("arbitrary",)),
        name="dispatch",
    )(plan["put"], plan["tail_start"], plan["tail_nch"], plan["tail_tot"], x1b, sk)


def _ffn_kernel(te_ref, nu_ref, xs_ref, wg_ref, wu_ref, wd_ref, y_ref):
    @pl.when(pl.program_id(0) < nu_ref[0])
    def _():
        x = xs_ref[...]
        hg = jnp.dot(x, wg_ref[...], preferred_element_type=F32)
        hu = jnp.dot(x, wu_ref[...], preferred_element_type=F32)
        h = hg * (1.0 / (1.0 + jnp.exp(-hg))) * hu
        y_ref[...] = jnp.dot(h.astype(BF16), wd_ref[...], preferred_element_type=F32).astype(BF16)


def _ffn(plan, xs, wg, wu, wd, n_tiles, tile):
    d = xs.shape[1]
    used = lambda i, te, nu: (jnp.minimum(i, nu[0] - 1), 0)
    wsel = lambda i, te, nu: (te[i], 0, 0)
    grid_spec = pltpu.PrefetchScalarGridSpec(
        num_scalar_prefetch=2,
        grid=(n_tiles,),
        in_specs=[pl.BlockSpec((tile, d), used),
                  pl.BlockSpec((None, d, EXPERT_DIM), wsel), pl.BlockSpec((None, d, EXPERT_DIM), wsel),
                  pl.BlockSpec((None, EXPERT_DIM, d), wsel)],
        out_specs=pl.BlockSpec((tile, d), used),
    )
    return pl.pallas_call(
        _ffn_kernel,
        grid_spec=grid_spec,
        out_shape=jax.ShapeDtypeStruct((n_tiles * tile, d), BF16),
        compiler_params=_cparams(("arbitrary",)),
        name="ffn",
    )(plan["tile_e"], plan["n_used"], xs, wg, wu, wd)


def _combine_kernel(get_ref, y_ref, sk_ref, x1_ref, x1b_ref, wsg_ref, wsu_ref, wsd_ref,
                    g_ref, b_ref, o_ref, buf_ref, sem, *, alpha):
    b = pl.program_id(0)
    slot = lax.rem(b, 3)
    nxt = lax.rem(b + 2, 3)

    def whole(s):
        return pltpu.make_async_copy(y_ref.at[pl.ds(0, BLOCK_ROWS), :], buf_ref.at[s], sem.at[s])

    @pl.when(b == 0)
    def _():
        def first(c, carry):
            s = c // BLOCK_CHUNKS
            pltpu.make_async_copy(
                y_ref.at[pl.ds(pl.multiple_of(get_ref[c], SEG_ALIGN), SEG_ALIGN), :],
                buf_ref.at[s, pl.ds(pl.multiple_of((c - s * BLOCK_CHUNKS) * SEG_ALIGN, SEG_ALIGN), SEG_ALIGN), :],
                sem.at[s]).start()
            return carry
        lax.fori_loop(0, 2 * BLOCK_CHUNKS, first, 0)

    whole(slot).wait()

    sk = sk_ref[0:TOP_K, :]
    gates = [sk_ref[TOP_K + k:TOP_K + k + 1, :].astype(BF16) for k in range(TOP_K)]
    base = (b + 2) * BLOCK_CHUNKS
    acc = None
    for sub in range(BLOCK_ROWS // DISP_SUB):
        j0 = sub * DISP_SUB
        w = jnp.concatenate([_order_rows(sk, float(j0 + i * ORD_SUB), gates) for i in range(DISP_SUB // ORD_SUB)],
                            axis=0)
        part = lax.dot_general(w, buf_ref[slot, j0:j0 + DISP_SUB, :], (((0,), (0,)), ((), ())),
                               preferred_element_type=F32)
        acc = part if acc is None else acc + part
        for c in range(sub * (DISP_SUB // SEG_ALIGN), (sub + 1) * (DISP_SUB // SEG_ALIGN)):
            pltpu.async_copy(
                y_ref.at[pl.ds(pl.multiple_of(get_ref[base + c], SEG_ALIGN), SEG_ALIGN), :],
                buf_ref.at[nxt, c * SEG_ALIGN:(c + 1) * SEG_ALIGN, :], sem.at[nxt], priority=c % 2)

    xb = x1b_ref[...]
    hg = jnp.dot(xb, wsg_ref[...], preferred_element_type=F32)
    hu = jnp.dot(xb, wsu_ref[...], preferred_element_type=F32)
    h = hg * (1.0 / (1.0 + jnp.exp(-hg))) * hu
    acc = acc + jnp.dot(h.astype(BF16), wsd_ref[...], preferred_element_type=F32)
    o_ref[...] = _layer_norm(alpha * x1_ref[...] + acc, g_ref[...], b_ref[...])

    @pl.when(b == pl.num_programs(0) - 1)
    def _():
        whole(lax.rem(b + 1, 3)).wait()
        whole(nxt).wait()


def _combine(plan, y, sk, x1, x1b, wsg, wsu, wsd, g, b, alpha):
    t, d = x1.shape
    nb = t // MOE_BLOCK
    row = lambda i, *_: (i, 0)
    const = lambda i, *_: (0, 0)
    grid_spec = pltpu.PrefetchScalarGridSpec(
        num_scalar_prefetch=1,
        grid=(nb,),
        in_specs=[pl.BlockSpec(memory_space=pl.ANY), pl.BlockSpec((2 * TOP_K, MOE_BLOCK), lambda i, *_: (0, i)),
                  pl.BlockSpec((MOE_BLOCK, d), row), pl.BlockSpec((MOE_BLOCK, d), row),
                  pl.BlockSpec(wsg.shape, const), pl.BlockSpec(wsu.shape, const), pl.BlockSpec(wsd.shape, const),
                  pl.BlockSpec(g.shape, const), pl.BlockSpec(b.shape, const)],
        out_specs=pl.BlockSpec((MOE_BLOCK, d), row),
        scratch_shapes=[pltpu.VMEM((3, BLOCK_ROWS, d), BF16), pltpu.SemaphoreType.DMA((3,))],
    )
    return pl.pallas_call(
        functools.partial(_combine_kernel, alpha=alpha),
        grid_spec=grid_spec,
        out_shape=jax.ShapeDtypeStruct((t, d), F32),
        compiler_params=_cparams(("arbitrary",)),
        name="combine",
    )(plan["get"], y, sk, x1, x1b, wsg, wsu, wsd, g, b)


def _moe(x1, x1b, sk, cnt, wg, wu, wd, wsg, wsu, wsd, g, b, alpha):
    t = x1.shape[0]
    nb = t // MOE_BLOCK
    tile = FFN_TILE if (TOP_K * t) // N_EXPERTS >= FFN_TILE else FFN_TILE_SMALL
    n_tiles = -(-(nb * BLOCK_ROWS) // tile) + N_EXPERTS
    n_rows = n_tiles * tile + 3 * BLOCK_ROWS
    plan = _moe_plan(cnt[:, 0, :N_EXPERTS], n_tiles, tile, n_tiles * tile)
    xs = _dispatch(plan, x1b, sk, n_rows)
    y = _ffn(plan, xs, wg, wu, wd, n_tiles, tile)
    return _combine(plan, y, sk, x1, x1b, wsg, wsu, wsd, g, b, alpha)


def _rope_cs(pos, dim):
    inv = ROPE_THETA ** (-jnp.arange(0, dim, 2, dtype=F32) / dim)
    ang = pos.astype(F32)[:, None] * inv[None, :]
    return jnp.cos(ang), jnp.sin(ang)


def _tables(pos, reps):
    n = pos.shape[0]
    c32, s32 = _rope_cs(pos, DIFF_QK)
    c16, s16 = _rope_cs(pos, MLA_ROPE)
    one = lambda w: jnp.ones((n, w), F32)
    zero = lambda w: jnp.zeros((n, w), F32)
    cd = jnp.concatenate([c32] * 4, axis=1)
    sd = jnp.concatenate([-s32, s32] * 2, axis=1)
    cq = jnp.concatenate([one(MLA_NOPE), c16, c16, one(32)], axis=1)
    sq = jnp.concatenate([zero(MLA_NOPE), -s16, s16, zero(32)], axis=1)
    ck = jnp.concatenate([c16, c16, zero(96)], axis=1)
    sk = jnp.concatenate([-s16, s16, zero(96)], axis=1)
    return tuple(jnp.tile(a, (reps, 1)) for a in (cd, sd, cq, sq, ck, sk))


def kernel(x_prompt, x_sample, cache_diff_k, cache_diff_v, cache_mla_ckv, cache_mla_kpe, w_in, diff_lambda, diff_subln_g, mla_q_norm_g, mla_w_uq, mla_kv_norm_g, mla_w_ukv, w_out, ln1_g, ln1_b, w_router, b_router, w_exp_gate, w_exp_up, w_exp_down, w_sh_gate, w_sh_up, w_sh_down, ln2_g, ln2_b):
    depth = w_in.shape[0]
    assert depth == 1
    d_model = x_prompt.shape[-1]
    alpha = (2.0 * depth) ** 0.25
    past_len = cache_diff_k.shape[2]
    layer = 0
    lambda_init = 0.8 - 0.6 * math.exp(-0.3 * layer)

    w_in_b = jnp.pad(w_in[layer], ((0, 0), (0, IN_PAD - IN_WIDTH))).astype(BF16)
    wuq = jnp.pad(mla_w_uq[layer], ((0, 0), (0, 0), (0, LANES - MLA_NOPE - MLA_ROPE)))
    wuq = wuq.reshape(Q_LORA, HEAD_W).astype(BF16)
    wukv = mla_w_ukv[layer]
    wuk = jnp.pad(wukv[:, :, :MLA_NOPE], ((0, 0), (0, 0), (0, LANES - MLA_NOPE))).reshape(KV_LORA, HEAD_W).astype(BF16)
    wuv = wukv[:, :, MLA_NOPE:].reshape(KV_LORA, HEAD_W).astype(BF16)
    place = jnp.pad(jnp.eye(MLA_ROPE, dtype=F32), ((0, 0), (MLA_NOPE, LANES - MLA_NOPE - MLA_ROPE)))
    place = jnp.tile(place, (1, MLA_HEADS)).astype(BF16)
    gq = mla_q_norm_g[layer].reshape(1, Q_LORA)
    gkv = mla_kv_norm_g[layer].reshape(1, KV_LORA)
    gsub = diff_subln_g[layer].reshape(1, LANES)
    wo = w_out[layer].astype(BF16)
    g1, b1 = ln1_g[layer].reshape(1, d_model), ln1_b[layer].reshape(1, d_model)
    g2, b2 = ln2_g[layer].reshape(1, d_model), ln2_b[layer].reshape(1, d_model)
    wr_t = w_router[layer].T
    wr_hi = wr_t.astype(BF16)
    wr_lo = (wr_t - wr_hi.astype(F32)).astype(BF16)
    br = b_router[layer].reshape(N_EXPERTS, 1)
    wg, wu, wd = (w[layer].astype(BF16) for w in (w_exp_gate, w_exp_up, w_exp_down))
    wsg, wsu, wsd = (w[layer].astype(BF16) for w in (w_sh_gate, w_sh_up, w_sh_down))
    lp = diff_lambda[layer].astype(F32)
    lam = jnp.exp(jnp.sum(lp[0] * lp[1])) - jnp.exp(jnp.sum(lp[2] * lp[3])) + lambda_init
    lam = lam.reshape(1, 1)

    def group(x, pos, past, tm, tq, tk, tkm, hps):
        b, s, _ = x.shape
        t = b * s
        tm = min(tm, t)
        xf = x.reshape(t, d_model)
        reps = max(1, tm // s)
        tables = _tables(pos, reps)
        n_pat = (s * reps) // tm
        qd, kd32, kd16, vd32, vd16, ckv, kpe, qm = _proj(xf, w_in_b, wuq, gq, gkv, tables, tm, n_pat)
        if past is None:
            k_d, v_d = kd16.reshape(b, s, HEAD_W), vd16.reshape(b, s, HEAD_W)
            k_m, v_m = _kvup(ckv, kpe, wuk, place, wuv, min(1024, t))
            k_m, v_m = k_m.reshape(b, s, HEAD_W), v_m.reshape(b, s, HEAD_W)
            q_pos0 = 0
        else:
            pk, pv, pc, pp = past
            k_d, v_d = _catcast(pk, pv, kd16.reshape(b, s, HEAD_W), vd16.reshape(b, s, HEAD_W), tk)
            k_m, v_m = _kvup_past(pc, pp, ckv.reshape(b, s, KV_LORA), kpe.reshape(b, s, MLA_ROPE),
                                  wuk, place, wuv, tk)
            q_pos0 = past_len
        o_d = _flash(qd.reshape(b, s, HEAD_W), k_d, v_d, lam, gsub, n_comp=2, hps=hps, tq=tq, tk=tk, tkm=tkm,
                     q_pos0=q_pos0, out_scale=1.0 - lambda_init)
        o_m = _flash(qm.reshape(b, s, HEAD_W), k_m, v_m, lam, gsub,
                     n_comp=1, hps=hps, tq=tq, tk=tk, tkm=tkm, q_pos0=q_pos0, out_scale=1.0)
        x1, x1b, slots, cnt = _mix(o_d.reshape(t, HEAD_W), o_m.reshape(t, HEAD_W), xf, wo, g1, b1, wr_hi, wr_lo, br,
                                   tm, alpha)
        y = _moe(x1, x1b, slots, cnt, wg, wu, wd, wsg, wsu, wsd, g2, b2, alpha)
        rows = (kd32.reshape(1, b, s, DIFF_HEADS, LANES), vd32.reshape(1, b, s, DIFF_HEADS, LANES),
                ckv.reshape(1, b, s, KV_LORA), kpe.reshape(1, b, s, MLA_ROPE))
        return y.reshape(b, s, d_model), rows

    s_p = x_prompt.shape[1]
    s_s = x_sample.shape[1]
    pos_p = jnp.arange(s_p, dtype=jnp.int32)
    pos_s = past_len + jnp.arange(s_s, dtype=jnp.int32)
    y_p, r_p = group(x_prompt, pos_p, None, 512, 256, 256, 256, DIFF_HEADS)
    past = (cache_diff_k[layer], cache_diff_v[layer], cache_mla_ckv[layer], cache_mla_kpe[layer])
    y_s, r_s = group(x_sample, pos_s, past, 512, s_s, 512, 256, DIFF_HEADS)
    return (y_p, y_s) + r_p + r_s
```

```python
import functools
import math

import jax
import jax.numpy as jnp
from jax import lax
from jax.experimental import pallas as pl
from jax.experimental.pallas import tpu as pltpu

F32 = jnp.float32
BF16 = jnp.bfloat16

LANES = 128
VMEM_LIMIT = 52 * 1024 * 1024

CHUNK = 64
CHUNK_SHIFT = 6
ROPE_THETA = 10000.0
LN_EPS = 1e-5
RMS_EPS = 1e-6
LOG2E = 1.4426950408889634
NEG_BIG = -1e30
FLASH_SUB = 256

DIFF_HEADS = 4
DIFF_QK = 64
MLA_HEADS = 4
MLA_NOPE = 64
MLA_ROPE = 32
MLA_V = 128
Q_LORA = 384
KV_LORA = 256
N_EXPERTS = 64
N_GROUPS = 8
GROUP_SIZE = N_EXPERTS // N_GROUPS
TOPK_GROUPS = 4
TOP_K = 8
ROUTED_SCALE = 2.5
EXPERT_DIM = 256

MOE_BLOCK = 256
SEG_ALIGN = 16
ORD_SUB = 256
DISP_SUB = 1024
BLOCK_ROWS = -(-(TOP_K * MOE_BLOCK + N_EXPERTS * (SEG_ALIGN - 1)) // DISP_SUB) * DISP_SUB
BLOCK_CHUNKS = BLOCK_ROWS // SEG_ALIGN
FFN_TILE = 1024
FFN_TILE_SMALL = 256

DQ_W = DIFF_HEADS * 2 * DIFF_QK
HEAD_W = DIFF_HEADS * LANES
IN_WIDTH = 3 * DQ_W + Q_LORA + KV_LORA + MLA_ROPE
IN_PAD = 2304
OFF_ROT = IN_PAD
IN_ALL = IN_PAD + 2 * DQ_W
OFF_DK, OFF_DV, OFF_CQ, OFF_CKV, OFF_KPE = 512, 1024, 1536, 1920, 2176


def _cparams(sem):
    return pltpu.CompilerParams(dimension_semantics=sem, vmem_limit_bytes=VMEM_LIMIT)


def _rms(x, g):
    return x * lax.rsqrt(jnp.mean(x * x, axis=-1, keepdims=True) + RMS_EPS) * g


def _layer_norm(x, g, b):
    mu = jnp.mean(x, axis=-1, keepdims=True)
    xc = x - mu
    var = jnp.mean(xc * xc, axis=-1, keepdims=True)
    return xc * lax.rsqrt(var + LN_EPS) * g + b


def _proj_kernel(x_ref, w_ref, wuq_ref, gq_ref, gkv_ref, cd_ref, sd_ref, cq_ref, sq_ref, ck_ref, sk_ref,
                 qd_ref, kd32_ref, kd16_ref, vd32_ref, vd16_ref, ckv_ref, kpe_ref, qm_ref,
                 *, scale_d, scale_m):
    tm = x_ref.shape[0]
    x = x_ref[...].astype(BF16)
    proj = jnp.dot(x, w_ref[...], preferred_element_type=F32)
    lane = lax.broadcasted_iota(jnp.int32, (tm, LANES), 1)

    cd = cd_ref[...]
    sd = sd_ref[...]

    def rope_d(off, j):
        blk = proj[:, off + j * LANES:off + (j + 1) * LANES]
        rot = proj[:, OFF_ROT + off + j * LANES:OFF_ROT + off + (j + 1) * LANES]
        return blk * cd + rot * sd

    for j in range(DIFF_HEADS):
        sl = slice(j * LANES, (j + 1) * LANES)
        qd_ref[:, sl] = (rope_d(0, j) * scale_d).astype(BF16)
        kr = rope_d(OFF_DK, j)
        kd32_ref[pl.ds(j, tm, stride=DIFF_HEADS), :] = kr
        kd16_ref[:, sl] = kr.astype(BF16)
        dv = proj[:, OFF_DV + j * LANES:OFF_DV + (j + 1) * LANES]
        vd32_ref[pl.ds(j, tm, stride=DIFF_HEADS), :] = dv
        vd16_ref[:, sl] = dv.astype(BF16)

    cqn = _rms(proj[:, OFF_CQ:OFF_CQ + Q_LORA], gq_ref[...])
    q = jnp.dot(cqn.astype(BF16), wuq_ref[...], preferred_element_type=F32)
    cq = cq_ref[...]
    sq = sq_ref[...]
    first_q = lane < (MLA_NOPE + MLA_ROPE // 2)
    for h in range(MLA_HEADS):
        sl = slice(h * LANES, (h + 1) * LANES)
        blk = q[:, sl]
        rot = jnp.where(first_q, pltpu.roll(blk, LANES - 16, 1), pltpu.roll(blk, 16, 1))
        qm_ref[:, sl] = ((blk * cq + rot * sq) * scale_m).astype(BF16)

    ckv_ref[...] = _rms(proj[:, OFF_CKV:OFF_CKV + KV_LORA], gkv_ref[...])

    kb = proj[:, OFF_KPE:OFF_KPE + LANES]
    rot = jnp.where(lane < 16, pltpu.roll(kb, LANES - 16, 1), pltpu.roll(kb, 16, 1))
    kpe_ref[...] = (kb * ck_ref[...] + rot * sk_ref[...])[:, :MLA_ROPE]


def _proj(x, w_in, wuq, gq, gkv, tables, tm, n_pat):
    t = x.shape[0]
    row = lambda i: (i, 0)
    const = lambda i: (0, 0)
    pat = lambda i: (i % n_pat, 0)
    tab_spec = pl.BlockSpec((tm, LANES), pat)
    out_w = lambda w, dt: jax.ShapeDtypeStruct((t, w), dt)
    cache_spec = pl.BlockSpec((tm * DIFF_HEADS, LANES), row)
    cache_shape = jax.ShapeDtypeStruct((t * DIFF_HEADS, LANES), F32)
    return pl.pallas_call(
        functools.partial(_proj_kernel, scale_d=LOG2E * DIFF_QK ** -0.5,
                          scale_m=LOG2E * (MLA_NOPE + MLA_ROPE) ** -0.5),
        grid=(t // tm,),
        in_specs=[pl.BlockSpec((tm, x.shape[1]), row),
                  pl.BlockSpec(w_in.shape, const), pl.BlockSpec(wuq.shape, const),
                  pl.BlockSpec(gq.shape, const), pl.BlockSpec(gkv.shape, const)] + [tab_spec] * 6,
        out_specs=[pl.BlockSpec((tm, HEAD_W), row), cache_spec, pl.BlockSpec((tm, HEAD_W), row), cache_spec,
                   pl.BlockSpec((tm, HEAD_W), row), pl.BlockSpec((tm, KV_LORA), row),
                   pl.BlockSpec((tm, MLA_ROPE), row), pl.BlockSpec((tm, HEAD_W), row)],
        out_shape=[out_w(HEAD_W, BF16), cache_shape, out_w(HEAD_W, BF16), cache_shape,
                   out_w(HEAD_W, BF16), out_w(KV_LORA, F32), out_w(MLA_ROPE, F32), out_w(HEAD_W, BF16)],
        compiler_params=_cparams(("parallel",)),
        name="proj",
    )(x, w_in, wuq, gq, gkv, *tables)


def _kvup_kernel(ckv_ref, kpe_ref, wuk_ref, place_ref, wuv_ref, k_ref, v_ref):
    c = ckv_ref[...].astype(BF16)
    k = jnp.dot(c, wuk_ref[...], preferred_element_type=F32)
    k = k + jnp.dot(kpe_ref[...].astype(BF16), place_ref[...], preferred_element_type=F32)
    k_ref[...] = k.astype(BF16)
    v_ref[...] = jnp.dot(c, wuv_ref[...], preferred_element_type=F32).astype(BF16)


def _kvup(ckv, kpe, wuk, place, wuv, tm):
    r = ckv.shape[0]
    row = lambda i: (i, 0)
    const = lambda i: (0, 0)
    return pl.pallas_call(
        _kvup_kernel,
        grid=(r // tm,),
        in_specs=[pl.BlockSpec((tm, KV_LORA), row), pl.BlockSpec((tm, MLA_ROPE), row),
                  pl.BlockSpec(wuk.shape, const), pl.BlockSpec(place.shape, const), pl.BlockSpec(wuv.shape, const)],
        out_specs=[pl.BlockSpec((tm, HEAD_W), row)] * 2,
        out_shape=[jax.ShapeDtypeStruct((r, HEAD_W), BF16)] * 2,
        compiler_params=_cparams(("parallel",)),
        name="kvup",
    )(ckv, kpe, wuk, place, wuv)


def _catcast_kernel(ck_ref, cv_ref, nk_ref, nv_ref, k_ref, v_ref, *, n_past, s_new, tr):
    j = pl.program_id(1)

    @pl.when(j < n_past)
    def _():
        for h in range(DIFF_HEADS):
            sl = slice(h * LANES, (h + 1) * LANES)
            k_ref[:, sl] = ck_ref[pl.ds(h, tr, stride=DIFF_HEADS), :].astype(BF16)
            v_ref[:, sl] = cv_ref[pl.ds(h, tr, stride=DIFF_HEADS), :].astype(BF16)

    @pl.when(j >= n_past)
    def _():
        k_ref[...] = jnp.zeros(k_ref.shape, BF16)
        v_ref[...] = jnp.zeros(v_ref.shape, BF16)
        k_ref[0:s_new, :] = nk_ref[...]
        v_ref[0:s_new, :] = nv_ref[...]


def _catcast(cache_k, cache_v, new_k, new_v, tr):
    b, p = cache_k.shape[:2]
    s = new_k.shape[1]
    assert p % tr == 0 and s <= tr
    n_past = p // tr
    past = pl.BlockSpec((None, tr * DIFF_HEADS, LANES), lambda bi, j: (bi, jnp.minimum(j, n_past - 1), 0))
    new = pl.BlockSpec((None, s, HEAD_W), lambda bi, j: (bi, 0, 0))
    out = pl.BlockSpec((None, tr, HEAD_W), lambda bi, j: (bi, j, 0))
    return pl.pallas_call(
        functools.partial(_catcast_kernel, n_past=n_past, s_new=s, tr=tr),
        grid=(b, n_past + 1),
        in_specs=[past, past, new, new],
        out_specs=[out, out],
        out_shape=[jax.ShapeDtypeStruct((b, p + tr, HEAD_W), BF16)] * 2,
        compiler_params=_cparams(("parallel", "arbitrary")),
        name="catcast",
    )(cache_k.reshape(b, p * DIFF_HEADS, LANES), cache_v.reshape(b, p * DIFF_HEADS, LANES), new_k, new_v)


def _kvup_past_kernel(pc_ref, pp_ref, nc_ref, np_ref, wuk_ref, place_ref, wuv_ref, k_ref, v_ref, c_ref, r_ref,
                      *, n_past, s_new):
    j = pl.program_id(1)

    @pl.when(j < n_past)
    def _():
        c_ref[...] = pc_ref[...]
        r_ref[...] = pp_ref[...]

    @pl.when(j >= n_past)
    def _():
        c_ref[...] = jnp.zeros(c_ref.shape, F32)
        r_ref[...] = jnp.zeros(r_ref.shape, F32)
        c_ref[0:s_new, :] = nc_ref[...]
        r_ref[0:s_new, :] = np_ref[...]

    _kvup_kernel(c_ref, r_ref, wuk_ref, place_ref, wuv_ref, k_ref, v_ref)


def _kvup_past(cache_c, cache_r, new_c, new_r, wuk, place, wuv, tr):
    b, p = cache_c.shape[:2]
    s = new_c.shape[1]
    assert p % tr == 0 and s <= tr
    n_past = p // tr
    clamp = lambda bi, j: (bi, jnp.minimum(j, n_past - 1), 0)
    first = lambda bi, j: (bi, 0, 0)
    const = lambda bi, j: (0, 0)
    out = pl.BlockSpec((None, tr, HEAD_W), lambda bi, j: (bi, j, 0))
    return pl.pallas_call(
        functools.partial(_kvup_past_kernel, n_past=n_past, s_new=s),
        grid=(b, n_past + 1),
        in_specs=[pl.BlockSpec((None, tr, KV_LORA), clamp), pl.BlockSpec((None, tr, MLA_ROPE), clamp),
                  pl.BlockSpec((None, s, KV_LORA), first), pl.BlockSpec((None, s, MLA_ROPE), first),
                  pl.BlockSpec(wuk.shape, const), pl.BlockSpec(place.shape, const), pl.BlockSpec(wuv.shape, const)],
        out_specs=[out, out],
        out_shape=[jax.ShapeDtypeStruct((b, p + tr, HEAD_W), BF16)] * 2,
        scratch_shapes=[pltpu.VMEM((tr, KV_LORA), F32), pltpu.VMEM((tr, MLA_ROPE), F32)],
        compiler_params=_cparams(("parallel", "arbitrary")),
        name="kvup_past",
    )(cache_c, cache_r, new_c, new_r, wuk, place, wuv)


def _flash_kernel(lam_ref, q_ref, k_ref, v_ref, g_ref, o_ref, qs_ref, vx_ref, m_ref, acc_ref,
                  *, n_comp, hps, tq, tk, tkm, sub, sk, q_pos0, out_scale, tiles, static, aligned):
    rows = n_comp * tq
    blocks = [(h, r0) for h in range(hps) for r0 in range(0, rows, sub)]

    @pl.when(pl.program_id(2) == 0)
    def _():
        for h in range(hps):
            vx_ref[:, 2 * h * LANES:(2 * h + 1) * LANES] = v_ref[:, h * LANES:(h + 1) * LANES]
            vx_ref[:, (2 * h + 1) * LANES:(2 * h + 2) * LANES] = jnp.ones((sk, LANES), BF16)

    def absorb(s, start, h, r0):
        rs = slice(h * rows + r0, h * rows + r0 + sub)
        width = s.shape[1]
        vx = vx_ref[pl.ds(start, width), 2 * h * LANES:(2 * h + 2) * LANES]
        m_prev = m_ref[rs, :]
        m_new = jnp.maximum(m_prev, jnp.max(s, axis=-1, keepdims=True))
        alpha = jnp.exp2(m_prev - m_new)
        p = jnp.exp2(s - jnp.concatenate([m_new] * (width // LANES), axis=1))
        pv = jnp.dot(p.astype(BF16), vx, preferred_element_type=F32)
        acc_ref[rs, :] = jnp.concatenate([alpha, alpha], axis=1) * acc_ref[rs, :] + pv
        m_ref[rs, :] = m_new

    def scores(start, width, h, r0):
        rs = slice(h * rows + r0, h * rows + r0 + sub)
        k = k_ref[pl.ds(start, width), h * LANES:(h + 1) * LANES]
        return lax.dot_general(qs_ref[rs, :], k, (((1,), (1,)), ((), ())), preferred_element_type=F32)

    def masked_scores(start, width, h, r0, q0):
        r = (lax.broadcasted_iota(jnp.int32, (sub, width), 0) + r0) & (tq - 1)
        c = lax.broadcasted_iota(jnp.int32, (sub, width), 1)
        ok = ((start + c) >> CHUNK_SHIFT) <= ((q0 + r) >> CHUNK_SHIFT)
        return jnp.where(ok, scores(start, width, h, r0), NEG_BIG)

    def one_tile(qt, q0):
        for h in range(hps):
            q = q_ref[qt * tq:(qt + 1) * tq, h * LANES:(h + 1) * LANES]
            if n_comp == 2:
                lane = lax.broadcasted_iota(jnp.int32, (tq, LANES), 1)
                zero = jnp.zeros_like(q)
                qs_ref[h * rows:h * rows + tq, :] = jnp.where(lane < DIFF_QK, q, zero)
                qs_ref[h * rows + tq:(h + 1) * rows, :] = jnp.where(lane >= DIFF_QK, q, zero)
            else:
                qs_ref[h * rows:(h + 1) * rows, :] = q
        m_ref[...] = jnp.full(m_ref.shape, NEG_BIG, F32)
        acc_ref[...] = jnp.zeros(acc_ref.shape, F32)

        if aligned:
            for j in range(q0 // tk):
                for h, r0 in blocks:
                    absorb(scores(j * tk, tk, h, r0), j * tk, h, r0)
            qc = lax.broadcasted_iota(jnp.int32, (sub, sub), 0) >> CHUNK_SHIFT
            kc = lax.broadcasted_iota(jnp.int32, (sub, sub), 1) >> CHUNK_SHIFT
            for h, r0 in blocks:
                a = r0 % tq
                s = scores(q0, a + sub, h, r0)
                last = jnp.where(kc <= qc, s[:, a:a + sub], NEG_BIG)
                absorb(jnp.concatenate([s[:, 0:a], last], axis=1) if a else last, q0, h, r0)
        elif static:
            lo_vis = min((q0 // CHUNK + 1) * CHUNK, sk)
            hi_vis = min(((q0 + tq - 1) // CHUNK + 1) * CHUNK, sk)
            for j in range(lo_vis // tk):
                for h, r0 in blocks:
                    absorb(scores(j * tk, tk, h, r0), j * tk, h, r0)
            start = lo_vis // tk * tk
            while start < hi_vis:
                width = min(tkm, -(-(hi_vis - start) // LANES) * LANES)
                for h, r0 in blocks:
                    absorb(masked_scores(start, width, h, r0, q0), start, h, r0)
                start += width
        else:
            lo_vis = jnp.minimum(((q0 >> CHUNK_SHIFT) + 1) << CHUNK_SHIFT, sk)
            hi_vis = jnp.minimum((((q0 + tq - 1) >> CHUNK_SHIFT) + 1) << CHUNK_SHIFT, sk)
            n_full = lo_vis // tk

            def full_body(j, carry):
                start = pl.multiple_of(j * tk, tk)
                for h, r0 in blocks:
                    absorb(scores(start, tk, h, r0), start, h, r0)
                return carry

            def masked_body(j, carry):
                start = pl.multiple_of(j * tkm, tkm)
                for h, r0 in blocks:
                    absorb(masked_scores(start, tkm, h, r0, q0), start, h, r0)
                return carry

            lax.fori_loop(0, n_full, full_body, 0)
            lax.fori_loop(n_full * (tk // tkm), (hi_vis + tkm - 1) // tkm, masked_body, 0)

        for h in range(hps):
            hr = slice(h * rows, (h + 1) * rows)
            o = acc_ref[hr, 0:LANES] / acc_ref[hr, LANES:2 * LANES]
            if n_comp == 2:
                o = o[0:tq, :] - lam_ref[0, 0] * o[tq:rows, :]
                o = _rms(o, g_ref[...]) * out_scale
            o_ref[qt * tq:(qt + 1) * tq, h * LANES:(h + 1) * LANES] = o.astype(BF16)

    if static:
        for qt in range(tiles):
            one_tile(qt, q_pos0 + qt * tq)
    else:
        one_tile(0, q_pos0 + pl.program_id(2) * tq)


def _flash(q, k, v, lam, g, *, n_comp, hps, tq, tk, tkm, q_pos0, out_scale):
    b, sq, _ = q.shape
    sk = k.shape[1]
    rows = n_comp * tq
    sub = min(rows, FLASH_SUB)
    assert sq % tq == 0 and sk % tk == 0 and tk % tkm == 0 and tq & (tq - 1) == 0 and rows % sub == 0
    assert DIFF_HEADS % hps == 0
    aligned = q_pos0 % tq == 0 and tq == tk and q_pos0 + sq <= sk and sub % CHUNK == 0 and sub % LANES == 0
    static = aligned or sq == tq
    tiles = sq // tq if static else 1
    hw = hps * LANES
    return pl.pallas_call(
        functools.partial(_flash_kernel, n_comp=n_comp, hps=hps, tq=tq, tk=tk, tkm=tkm, sub=sub, sk=sk,
                          q_pos0=q_pos0, out_scale=out_scale, tiles=tiles, static=static, aligned=aligned),
        grid=(b, DIFF_HEADS // hps, sq // (tq * tiles)),
        in_specs=[pl.BlockSpec(memory_space=pltpu.SMEM),
                  pl.BlockSpec((None, tq * tiles, hw), lambda bi, h, i: (bi, i, h)),
                  pl.BlockSpec((None, sk, hw), lambda bi, h, i: (bi, 0, h)),
                  pl.BlockSpec((None, sk, hw), lambda bi, h, i: (bi, 0, h)),
                  pl.BlockSpec((1, LANES), lambda bi, h, i: (0, 0))],
        out_specs=pl.BlockSpec((None, tq * tiles, hw), lambda bi, h, i: (bi, i, h)),
        out_shape=jax.ShapeDtypeStruct((b, sq, HEAD_W), BF16),
        scratch_shapes=[pltpu.VMEM((hps * rows, LANES), BF16), pltpu.VMEM((sk, 2 * hw), BF16),
                        pltpu.VMEM((hps * rows, LANES), F32), pltpu.VMEM((hps * rows, 2 * LANES), F32)],
        compiler_params=_cparams(("parallel", "parallel", "arbitrary")),
        name="flash_diff" if n_comp == 2 else "flash_mla",
    )(lam, q, k, v, g)


def _sublane_max(x):
    return jnp.max(x, axis=0, keepdims=True)


def _sublane_min(x):
    return jnp.min(x, axis=0, keepdims=True)


def _route_t(scores, biased):
    tm = scores[0].shape[1]
    sub = lax.broadcasted_iota(jnp.int32, (GROUP_SIZE, tm), 0)
    neg_inf = jnp.float32(-jnp.inf)
    gs = jnp.zeros((N_GROUPS, tm), F32)
    for g in range(N_GROUPS):
        bg = biased[g]
        m1 = _sublane_max(bg)
        i1 = _sublane_min(jnp.where(bg == m1, sub, GROUP_SIZE))
        m2 = _sublane_max(jnp.where(sub == i1, neg_inf, bg))
        gs = jnp.where(sub == g, m1 + m2, gs)
    keep = jnp.zeros((N_GROUPS, tm), jnp.bool_)
    cur = gs
    for _ in range(TOPK_GROUPS):
        mx = _sublane_max(cur)
        fi = _sublane_min(jnp.where(cur == mx, sub, N_GROUPS))
        hit = sub == fi
        keep = jnp.logical_or(keep, hit)
        cur = jnp.where(hit, neg_inf, cur)
    keep_f = jnp.where(keep, 1.0, 0.0)
    cand = []
    for g in range(N_GROUPS):
        kg = _sublane_max(jnp.where(sub == g, keep_f, 0.0)) > 0.5
        cand.append(jnp.where(kg, biased[g], neg_inf))
    chosen = [jnp.zeros((GROUP_SIZE, tm), jnp.bool_) for _ in range(N_GROUPS)]
    picks = []
    for _ in range(TOP_K):
        mx = cand[0]
        for g in range(1, N_GROUPS):
            mx = jnp.maximum(mx, cand[g])
        mx = _sublane_max(mx)
        fi = jnp.where(cand[0] == mx, sub, N_EXPERTS)
        for g in range(1, N_GROUPS):
            fi = jnp.minimum(fi, jnp.where(cand[g] == mx, sub + g * GROUP_SIZE, N_EXPERTS))
        fi = _sublane_min(fi)
        picks.append(fi)
        for g in range(N_GROUPS):
            hit = (sub + g * GROUP_SIZE) == fi
            chosen[g] = jnp.logical_or(chosen[g], hit)
            cand[g] = jnp.where(hit, neg_inf, cand[g])
    w = [jnp.where(chosen[g], scores[g], 0.0) for g in range(N_GROUPS)]
    tot = w[0]
    for g in range(1, N_GROUPS):
        tot = tot + w[g]
    tot = jnp.sum(tot, axis=0, keepdims=True)
    return [wg / tot * ROUTED_SCALE for wg in w], chosen, picks


def _mix_kernel(od_ref, om_ref, x_ref, wo_ref, g_ref, b_ref, wr_ref, wrl_ref, br_ref, before_ref, below_ref, eye_ref,
                x1_ref, x1b_ref, sk_ref, cnt_ref, *, alpha):
    mix = jnp.dot(od_ref[...], wo_ref[0:HEAD_W, :], preferred_element_type=F32)
    mix = mix + jnp.dot(om_ref[...], wo_ref[HEAD_W:2 * HEAD_W, :], preferred_element_type=F32)
    x1 = _layer_norm(alpha * x_ref[...] + mix, g_ref[...], b_ref[...])
    x_hi = x1.astype(BF16)
    x1_ref[...] = x1
    x1b_ref[...] = x_hi
    x_lo = (x1 - x_hi.astype(F32)).astype(BF16)
    nt = (((1,), (1,)), ((), ()))
    logits = (lax.dot_general(wr_ref[...], x_hi, nt, preferred_element_type=F32)
              + lax.dot_general(wr_ref[...], x_lo, nt, preferred_element_type=F32)
              + lax.dot_general(wrl_ref[...], x_hi, nt, preferred_element_type=F32))
    sc = 1.0 / (1.0 + jnp.exp(-logits))
    bi = sc + br_ref[...]
    scores = [sc[g * GROUP_SIZE:(g + 1) * GROUP_SIZE, :] for g in range(N_GROUPS)]
    biased = [bi[g * GROUP_SIZE:(g + 1) * GROUP_SIZE, :] for g in range(N_GROUPS)]
    gates, chosen, picks = _route_t(scores, biased)
    tm = x1.shape[0]
    ch = jnp.concatenate([jnp.where(c, 1.0, 0.0) for c in chosen], axis=0)
    gate_all = jnp.concatenate(gates, axis=0)
    sub = lax.broadcasted_iota(jnp.int32, (N_EXPERTS, MOE_BLOCK), 0)
    for blk in range(tm // MOE_BLOCK):
        ls = slice(blk * MOE_BLOCK, (blk + 1) * MOE_BLOCK)
        chb = ch[:, ls]
        rank = jnp.dot(chb.astype(BF16), before_ref[...], preferred_element_type=F32)
        cnt = jnp.sum(chb, axis=1, keepdims=True)
        seg = jnp.floor((cnt + (SEG_ALIGN - 1.0)) * (1.0 / SEG_ALIGN)) * SEG_ALIGN
        seg_b = jnp.broadcast_to(seg, (N_EXPERTS, LANES))
        off = jnp.dot(below_ref[...], seg_b, precision=lax.Precision.HIGHEST, preferred_element_type=F32)
        slot = jnp.concatenate([off] * (MOE_BLOCK // LANES), axis=1) + rank
        cnt_ref[blk] = jnp.sum(seg_b * eye_ref[...], axis=0, keepdims=True)
        gb = gate_all[:, ls]
        for k in range(TOP_K):
            hit = sub == picks[k][:, ls]
            sk_ref[k:k + 1, ls] = jnp.sum(jnp.where(hit, slot, 0.0), axis=0, keepdims=True)
            sk_ref[TOP_K + k:TOP_K + k + 1, ls] = jnp.sum(jnp.where(hit, gb, 0.0), axis=0, keepdims=True)


def _mix(od, om, x, wo, g, b, wr_hi, wr_lo, br, tm, alpha):
    t = x.shape[0]
    d = x.shape[1]
    nblk = tm // MOE_BLOCK
    row = lambda i: (i, 0)
    const = lambda i: (0, 0)
    idx = jnp.arange(MOE_BLOCK)
    before = (idx[:, None] < idx[None, :]).astype(BF16)
    ide = jnp.arange(N_EXPERTS)
    below = (ide[None, :] < ide[:, None]).astype(F32)
    eye = (ide[:, None] == jnp.arange(LANES)[None, :]).astype(F32)
    return pl.pallas_call(
        functools.partial(_mix_kernel, alpha=alpha),
        grid=(t // tm,),
        in_specs=[pl.BlockSpec((tm, HEAD_W), row), pl.BlockSpec((tm, HEAD_W), row), pl.BlockSpec((tm, d), row),
                  pl.BlockSpec(wo.shape, const), pl.BlockSpec(g.shape, const), pl.BlockSpec(b.shape, const),
                  pl.BlockSpec(wr_hi.shape, const), pl.BlockSpec(wr_lo.shape, const), pl.BlockSpec(br.shape, const),
                  pl.BlockSpec(before.shape, const), pl.BlockSpec(below.shape, const), pl.BlockSpec(eye.shape, const)],
        out_specs=[pl.BlockSpec((tm, d), row), pl.BlockSpec((tm, d), row),
                   pl.BlockSpec((2 * TOP_K, tm), lambda i: (0, i)),
                   pl.BlockSpec((nblk, 1, LANES), lambda i: (i, 0, 0))],
        out_shape=[jax.ShapeDtypeStruct((t, d), F32), jax.ShapeDtypeStruct((t, d), BF16),
                   jax.ShapeDtypeStruct((2 * TOP_K, t), F32),
                   jax.ShapeDtypeStruct((t // MOE_BLOCK, 1, LANES), F32)],
        compiler_params=_cparams(("parallel",)),
        name="mix",
    )(od, om, x, wo, g, b, wr_hi, wr_lo, br, before, below, eye)


def _moe_plan(cnt, n_tiles, tile, trash_row):
    seg = cnt.astype(jnp.int32)
    nb = seg.shape[0]
    before_blocks = jnp.cumsum(seg, axis=0) - seg
    length = jnp.sum(seg, axis=0)
    padded = -(-length // tile) * tile
    ends = jnp.cumsum(padded)
    start = ends - padded
    dst = start[None, :] + before_blocks
    off = (jnp.cumsum(seg, axis=1) - seg) // SEG_ALIGN
    tot = jnp.sum(seg, axis=1) // SEG_ALIGN
    chunk = jnp.arange(BLOCK_CHUNKS, dtype=jnp.int32)
    base = dst - SEG_ALIGN * off
    step = jnp.concatenate([base[:, :1], base[:, 1:] - base[:, :-1]], axis=1)
    started = (off[:, None, :] <= chunk[None, :, None]).astype(jnp.int32)
    row = jnp.sum(started * step[:, None, :], axis=2) + SEG_ALIGN * chunk[None, :]
    valid = chunk[None, :] < tot[:, None]
    trash = trash_row + ((jnp.arange(nb, dtype=jnp.int32) % 3) * BLOCK_ROWS)[:, None] + SEG_ALIGN * chunk[None, :]
    first = trash_row + 2 * BLOCK_ROWS + SEG_ALIGN * chunk[None, :]
    put = jnp.concatenate([first, jnp.where(valid, row, trash)], axis=0).reshape(-1)
    get = jnp.concatenate([jnp.where(valid, row, 0), jnp.zeros((2, BLOCK_CHUNKS), jnp.int32)], axis=0).reshape(-1)
    n_used = ends[-1] // tile
    tile_start = jnp.arange(n_tiles, dtype=jnp.int32) * tile
    tile_e = jnp.sum((ends[None, :] <= tile_start[:, None]).astype(jnp.int32), axis=1)
    last_e = tile_e[jnp.maximum(n_used - 1, 0)]
    tile_e = jnp.where(jnp.arange(n_tiles) < n_used, tile_e, last_e)
    tail_start = start + length
    tail_nch = (padded - length) // SEG_ALIGN
    return dict(put=put, get=get, n_used=n_used.reshape(1), tile_e=tile_e, tail_start=tail_start,
                tail_nch=tail_nch, tail_tot=jnp.sum(tail_nch).reshape(1))


def _drain(copy, n):
    def body(i, c):
        copy.wait()
        return c
    lax.fori_loop(0, n, body, 0)


def _order_rows(sk, j0, vals):
    rows = lax.broadcasted_iota(jnp.int32, (ORD_SUB, MOE_BLOCK), 0).astype(F32).astype(BF16)
    out = jnp.zeros((ORD_SUB, MOE_BLOCK), BF16)
    for k in range(TOP_K):
        rel = (sk[k:k + 1, :] - j0).astype(BF16)
        out = jnp.where(rows == rel, vals[k], out)
    return out


def _dispatch_kernel(put_ref, tstart_ref, tnch_ref, ttot_ref, x_ref, sk_ref, xs_ref, buf_ref, zero_ref, sem, tail_sem):
    b = pl.program_id(0)
    last = pl.num_programs(0) - 1
    slot = lax.rem(b, 3)
    send = lax.rem(b + 2, 3)

    def whole(s):
        return pltpu.make_async_copy(buf_ref.at[s], xs_ref.at[pl.ds(0, BLOCK_ROWS), :], sem.at[s])

    @pl.when(b == 0)
    def _():
        buf_ref[...] = jnp.zeros(buf_ref.shape, BF16)

    @pl.when(b >= 2)
    def _():
        whole(slot).wait()

    x = x_ref[...]
    sk = sk_ref[0:TOP_K, :]
    ones = [jnp.ones((1, MOE_BLOCK), BF16)] * TOP_K
    base = b * BLOCK_CHUNKS
    for sub in range(BLOCK_ROWS // DISP_SUB):
        j0 = sub * DISP_SUB
        sel = jnp.concatenate([_order_rows(sk, float(j0 + i * ORD_SUB), ones) for i in range(DISP_SUB // ORD_SUB)],
                              axis=0)
        buf_ref[slot, j0:j0 + DISP_SUB, :] = jnp.dot(sel, x, preferred_element_type=F32).astype(BF16)
        for c in range(sub * (DISP_SUB // SEG_ALIGN), (sub + 1) * (DISP_SUB // SEG_ALIGN)):
            pltpu.make_async_copy(
                buf_ref.at[send, c * SEG_ALIGN:(c + 1) * SEG_ALIGN, :],
                xs_ref.at[pl.ds(pl.multiple_of(put_ref[base + c], SEG_ALIGN), SEG_ALIGN), :], sem.at[send]).start()

    @pl.when(b == last)
    def _():
        whole(send).wait()

        @pl.when(b >= 1)
        def _():
            whole(lax.rem(b + 1, 3)).wait()

        zero_ref[...] = jnp.zeros(zero_ref.shape, BF16)

        def tail_copy(dst_row):
            return pltpu.make_async_copy(zero_ref, xs_ref.at[pl.ds(pl.multiple_of(dst_row, SEG_ALIGN), SEG_ALIGN), :],
                                         tail_sem.at[0])

        def tail_expert(e, carry):
            def tail_chunk(c, carry2):
                tail_copy(tstart_ref[e] + c * SEG_ALIGN).start()
                return carry2
            lax.fori_loop(0, tnch_ref[e], tail_chunk, 0)
            return carry

        lax.fori_loop(0, N_EXPERTS, tail_expert, 0)
        _drain(tail_copy(0), ttot_ref[0])


def _dispatch(plan, x1b, sk, n_rows):
    t, d = x1b.shape
    nb = t // MOE_BLOCK
    grid_spec = pltpu.PrefetchScalarGridSpec(
        num_scalar_prefetch=4,
        grid=(nb + 1,),
        in_specs=[pl.BlockSpec((MOE_BLOCK, d), lambda b, *_: (jnp.minimum(b, nb - 1), 0)),
                  pl.BlockSpec((2 * TOP_K, MOE_BLOCK), lambda b, *_: (0, jnp.minimum(b, nb - 1)))],
        out_specs=pl.BlockSpec(memory_space=pl.ANY),
        scratch_shapes=[pltpu.VMEM((3, BLOCK_ROWS, d), BF16), pltpu.VMEM((SEG_ALIGN, d), BF16),
                        pltpu.SemaphoreType.DMA((3,)), pltpu.SemaphoreType.DMA((1,))],
    )
    return pl.pallas_call(
        _dispatch_kernel,
        grid_spec=grid_spec,
        out_shape=jax.ShapeDtypeStruct((n_rows, d), BF16),
        compiler_params=_cparams(("arbitrary",)),
        name="dispatch",
    )(plan["put"], plan["tail_start"], plan["tail_nch"], plan["tail_tot"], x1b, sk)


def _ffn_kernel(te_ref, nu_ref, xs_ref, wg_ref, wu_ref, wd_ref, y_ref):
    @pl.when(pl.program_id(0) < nu_ref[0])
    def _():
        x = xs_ref[...]
        hg = jnp.dot(x, wg_ref[...], preferred_element_type=F32)
        hu = jnp.dot(x, wu_ref[...], preferred_element_type=F32)
        h = hg * (1.0 / (1.0 + jnp.exp(-hg))) * hu
        y_ref[...] = jnp.dot(h.astype(BF16), wd_ref[...], preferred_element_type=F32).astype(BF16)


def _ffn(plan, xs, wg, wu, wd, n_tiles, tile):
    d = xs.shape[1]
    used = lambda i, te, nu: (jnp.minimum(i, nu[0] - 1), 0)
    wsel = lambda i, te, nu: (te[i], 0, 0)
    grid_spec = pltpu.PrefetchScalarGridSpec(
        num_scalar_prefetch=2,
        grid=(n_tiles,),
        in_specs=[pl.BlockSpec((tile, d), used),
                  pl.BlockSpec((None, d, EXPERT_DIM), wsel), pl.BlockSpec((None, d, EXPERT_DIM), wsel),
                  pl.BlockSpec((None, EXPERT_DIM, d), wsel)],
        out_specs=pl.BlockSpec((tile, d), used),
    )
    return pl.pallas_call(
        _ffn_kernel,
        grid_spec=grid_spec,
        out_shape=jax.ShapeDtypeStruct((n_tiles * tile, d), BF16),
        compiler_params=_cparams(("arbitrary",)),
        name="ffn",
    )(plan["tile_e"], plan["n_used"], xs, wg, wu, wd)


def _combine_kernel(get_ref, y_ref, sk_ref, x1_ref, x1b_ref, wsg_ref, wsu_ref, wsd_ref,
                    g_ref, b_ref, o_ref, buf_ref, sem, *, alpha):
    b = pl.program_id(0)
    slot = lax.rem(b, 3)
    nxt = lax.rem(b + 2, 3)

    def whole(s):
        return pltpu.make_async_copy(y_ref.at[pl.ds(0, BLOCK_ROWS), :], buf_ref.at[s], sem.at[s])

    @pl.when(b == 0)
    def _():
        def first(c, carry):
            s = c // BLOCK_CHUNKS
            pltpu.make_async_copy(
                y_ref.at[pl.ds(pl.multiple_of(get_ref[c], SEG_ALIGN), SEG_ALIGN), :],
                buf_ref.at[s, pl.ds(pl.multiple_of((c - s * BLOCK_CHUNKS) * SEG_ALIGN, SEG_ALIGN), SEG_ALIGN), :],
                sem.at[s]).start()
            return carry
        lax.fori_loop(0, 2 * BLOCK_CHUNKS, first, 0)

    whole(slot).wait()

    sk = sk_ref[0:TOP_K, :]
    gates = [sk_ref[TOP_K + k:TOP_K + k + 1, :].astype(BF16) for k in range(TOP_K)]
    base = (b + 2) * BLOCK_CHUNKS
    acc = None
    for sub in range(BLOCK_ROWS // DISP_SUB):
        j0 = sub * DISP_SUB
        w = jnp.concatenate([_order_rows(sk, float(j0 + i * ORD_SUB), gates) for i in range(DISP_SUB // ORD_SUB)],
                            axis=0)
        part = lax.dot_general(w, buf_ref[slot, j0:j0 + DISP_SUB, :], (((0,), (0,)), ((), ())),
                               preferred_element_type=F32)
        acc = part if acc is None else acc + part
        for c in range(sub * (DISP_SUB // SEG_ALIGN), (sub + 1) * (DISP_SUB // SEG_ALIGN)):
            pltpu.async_copy(
                y_ref.at[pl.ds(pl.multiple_of(get_ref[base + c], SEG_ALIGN), SEG_ALIGN), :],
                buf_ref.at[nxt, c * SEG_ALIGN:(c + 1) * SEG_ALIGN, :], sem.at[nxt], priority=c % 2)

    xb = x1b_ref[...]
    hg = jnp.dot(xb, wsg_ref[...], preferred_element_type=F32)
    hu = jnp.dot(xb, wsu_ref[...], preferred_element_type=F32)
    h = hg * (1.0 / (1.0 + jnp.exp(-hg))) * hu
    acc = acc + jnp.dot(h.astype(BF16), wsd_ref[...], preferred_element_type=F32)
    o_ref[...] = _layer_norm(alpha * x1_ref[...] + acc, g_ref[...], b_ref[...])

    @pl.when(b == pl.num_programs(0) - 1)
    def _():
        whole(lax.rem(b + 1, 3)).wait()
        whole(nxt).wait()


def _combine(plan, y, sk, x1, x1b, wsg, wsu, wsd, g, b, alpha):
    t, d = x1.shape
    nb = t // MOE_BLOCK
    row = lambda i, *_: (i, 0)
    const = lambda i, *_: (0, 0)
    grid_spec = pltpu.PrefetchScalarGridSpec(
        num_scalar_prefetch=1,
        grid=(nb,),
        in_specs=[pl.BlockSpec(memory_space=pl.ANY), pl.BlockSpec((2 * TOP_K, MOE_BLOCK), lambda i, *_: (0, i)),
                  pl.BlockSpec((MOE_BLOCK, d), row), pl.BlockSpec((MOE_BLOCK, d), row),
                  pl.BlockSpec(wsg.shape, const), pl.BlockSpec(wsu.shape, const), pl.BlockSpec(wsd.shape, const),
                  pl.BlockSpec(g.shape, const), pl.BlockSpec(b.shape, const)],
        out_specs=pl.BlockSpec((MOE_BLOCK, d), row),
        scratch_shapes=[pltpu.VMEM((3, BLOCK_ROWS, d), BF16), pltpu.SemaphoreType.DMA((3,))],
    )
    return pl.pallas_call(
        functools.partial(_combine_kernel, alpha=alpha),
        grid_spec=grid_spec,
        out_shape=jax.ShapeDtypeStruct((t, d), F32),
        compiler_params=_cparams(("arbitrary",)),
        name="combine",
    )(plan["get"], y, sk, x1, x1b, wsg, wsu, wsd, g, b)


def _moe(x1, x1b, sk, cnt, wg, wu, wd, wsg, wsu, wsd, g, b, alpha):
    t = x1.shape[0]
    nb = t // MOE_BLOCK
    tile = FFN_TILE if (TOP_K * t) // N_EXPERTS >= FFN_TILE else FFN_TILE_SMALL
    n_tiles = -(-(nb * BLOCK_ROWS) // tile) + N_EXPERTS
    n_rows = n_tiles * tile + 3 * BLOCK_ROWS
    plan = _moe_plan(cnt[:, 0, :N_EXPERTS], n_tiles, tile, n_tiles * tile)
    xs = _dispatch(plan, x1b, sk, n_rows)
    y = _ffn(plan, xs, wg, wu, wd, n_tiles, tile)
    return _combine(plan, y, sk, x1, x1b, wsg, wsu, wsd, g, b, alpha)


def _rope_cs(pos, dim):
    inv = ROPE_THETA ** (-jnp.arange(0, dim, 2, dtype=F32) / dim)
    ang = pos.astype(F32)[:, None] * inv[None, :]
    return jnp.cos(ang), jnp.sin(ang)


def _tables(pos, reps):
    n = pos.shape[0]
    c32, s32 = _rope_cs(pos, DIFF_QK)
    c16, s16 = _rope_cs(pos, MLA_ROPE)
    one = lambda w: jnp.ones((n, w), F32)
    zero = lambda w: jnp.zeros((n, w), F32)
    cd = jnp.concatenate([c32] * 4, axis=1)
    sd = jnp.concatenate([-s32, s32] * 2, axis=1)
    cq = jnp.concatenate([one(MLA_NOPE), c16, c16, one(32)], axis=1)
    sq = jnp.concatenate([zero(MLA_NOPE), -s16, s16, zero(32)], axis=1)
    ck = jnp.concatenate([c16, c16, zero(96)], axis=1)
    sk = jnp.concatenate([-s16, s16, zero(96)], axis=1)
    return tuple(jnp.tile(a, (reps, 1)) for a in (cd, sd, cq, sq, ck, sk))


def kernel(x_prompt, x_sample, cache_diff_k, cache_diff_v, cache_mla_ckv, cache_mla_kpe, w_in, diff_lambda, diff_subln_g, mla_q_norm_g, mla_w_uq, mla_kv_norm_g, mla_w_ukv, w_out, ln1_g, ln1_b, w_router, b_router, w_exp_gate, w_exp_up, w_exp_down, w_sh_gate, w_sh_up, w_sh_down, ln2_g, ln2_b):
    depth = w_in.shape[0]
    assert depth == 1
    d_model = x_prompt.shape[-1]
    alpha = (2.0 * depth) ** 0.25
    past_len = cache_diff_k.shape[2]
    layer = 0
    lambda_init = 0.8 - 0.6 * math.exp(-0.3 * layer)

    col = jnp.arange(2 * DQ_W)
    partner = jnp.where((col % DIFF_QK) < DIFF_QK // 2, col + DIFF_QK // 2, col - DIFF_QK // 2)
    w_in_b = jnp.concatenate([jnp.pad(w_in[layer], ((0, 0), (0, IN_PAD - IN_WIDTH))), w_in[layer][:, partner]],
                             axis=1).astype(BF16)
    wuq = jnp.pad(mla_w_uq[layer], ((0, 0), (0, 0), (0, LANES - MLA_NOPE - MLA_ROPE)))
    wuq = wuq.reshape(Q_LORA, HEAD_W).astype(BF16)
    wukv = mla_w_ukv[layer]
    wuk = jnp.pad(wukv[:, :, :MLA_NOPE], ((0, 0), (0, 0), (0, LANES - MLA_NOPE))).reshape(KV_LORA, HEAD_W).astype(BF16)
    wuv = wukv[:, :, MLA_NOPE:].reshape(KV_LORA, HEAD_W).astype(BF16)
    place = jnp.pad(jnp.eye(MLA_ROPE, dtype=F32), ((0, 0), (MLA_NOPE, LANES - MLA_NOPE - MLA_ROPE)))
    place = jnp.tile(place, (1, MLA_HEADS)).astype(BF16)
    gq = mla_q_norm_g[layer].reshape(1, Q_LORA)
    gkv = mla_kv_norm_g[layer].reshape(1, KV_LORA)
    gsub = diff_subln_g[layer].reshape(1, LANES)
    wo = w_out[layer].astype(BF16)
    g1, b1 = ln1_g[layer].reshape(1, d_model), ln1_b[layer].reshape(1, d_model)
    g2, b2 = ln2_g[layer].reshape(1, d_model), ln2_b[layer].reshape(1, d_model)
    wr_t = w_router[layer].T
    wr_hi = wr_t.astype(BF16)
    wr_lo = (wr_t - wr_hi.astype(F32)).astype(BF16)
    br = b_router[layer].reshape(N_EXPERTS, 1)
    wg, wu, wd = (w[layer].astype(BF16) for w in (w_exp_gate, w_exp_up, w_exp_down))
    wsg, wsu, wsd = (w[layer].astype(BF16) for w in (w_sh_gate, w_sh_up, w_sh_down))
    lp = diff_lambda[layer].astype(F32)
    lam = jnp.exp(jnp.sum(lp[0] * lp[1])) - jnp.exp(jnp.sum(lp[2] * lp[3])) + lambda_init
    lam = lam.reshape(1, 1)

    def group(x, pos, past, tm, tq, tk, tkm, hps):
        b, s, _ = x.shape
        t = b * s
        tm = min(tm, t)
        xf = x.reshape(t, d_model)
        reps = max(1, tm // s)
        tables = _tables(pos, reps)
        n_pat = (s * reps) // tm
        qd, kd32, kd16, vd32, vd16, ckv, kpe, qm = _proj(xf, w_in_b, wuq, gq, gkv, tables, tm, n_pat)
        if past is None:
            k_d, v_d = kd16.reshape(b, s, HEAD_W), vd16.reshape(b, s, HEAD_W)
            k_m, v_m = _kvup(ckv, kpe, wuk, place, wuv, min(1024, t))
            k_m, v_m = k_m.reshape(b, s, HEAD_W), v_m.reshape(b, s, HEAD_W)
            q_pos0 = 0
        else:
            pk, pv, pc, pp = past
            k_d, v_d = _catcast(pk, pv, kd16.reshape(b, s, HEAD_W), vd16.reshape(b, s, HEAD_W), tk)
            k_m, v_m = _kvup_past(pc, pp, ckv.reshape(b, s, KV_LORA), kpe.reshape(b, s, MLA_ROPE),
                                  wuk, place, wuv, tk)
            q_pos0 = past_len
        o_d = _flash(qd.reshape(b, s, HEAD_W), k_d, v_d, lam, gsub, n_comp=2, hps=hps, tq=tq, tk=tk, tkm=tkm,
                     q_pos0=q_pos0, out_scale=1.0 - lambda_init)
        o_m = _flash(qm.reshape(b, s, HEAD_W), k_m, v_m, lam, gsub,
                     n_comp=1, hps=hps, tq=tq, tk=tk, tkm=tkm, q_pos0=q_pos0, out_scale=1.0)
        x1, x1b, slots, cnt = _mix(o_d.reshape(t, HEAD_W), o_m.reshape(t, HEAD_W), xf, wo, g1, b1, wr_hi, wr_lo, br,
                                   tm, alpha)
        y = _moe(x1, x1b, slots, cnt, wg, wu, wd, wsg, wsu, wsd, g2, b2, alpha)
        rows = (kd32.reshape(1, b, s, DIFF_HEADS, LANES), vd32.reshape(1, b, s, DIFF_HEADS, LANES),
                ckv.reshape(1, b, s, KV_LORA), kpe.reshape(1, b, s, MLA_ROPE))
        return y.reshape(b, s, d_model), rows

    s_p = x_prompt.shape[1]
    s_s = x_sample.shape[1]
    pos_p = jnp.arange(s_p, dtype=jnp.int32)
    pos_s = past_len + jnp.arange(s_s, dtype=jnp.int32)
    y_p, r_p = group(x_prompt, pos_p, None, 512, 256, 256, 256, DIFF_HEADS)
    past = (cache_diff_k[layer], cache_diff_v[layer], cache_mla_ckv[layer], cache_mla_kpe[layer])
    y_s, r_s = group(x_sample, pos_s, past, 512, s_s, 512, 256, DIFF_HEADS)
    return (y_p, y_s) + r_p + r_s
```

```python
import functools
import math

import jax
import jax.numpy as jnp
from jax import lax
from jax.experimental import pallas as pl
from jax.experimental.pallas import tpu as pltpu

F32 = jnp.float32
BF16 = jnp.bfloat16

LANES = 128
VMEM_LIMIT = 52 * 1024 * 1024

CHUNK = 64
CHUNK_SHIFT = 6
ROPE_THETA = 10000.0
LN_EPS = 1e-5
RMS_EPS = 1e-6
LOG2E = 1.4426950408889634
NEG_BIG = -1e30
FLASH_SUB = 256

DIFF_HEADS = 4
DIFF_QK = 64
MLA_HEADS = 4
MLA_NOPE = 64
MLA_ROPE = 32
MLA_V = 128
Q_LORA = 384
KV_LORA = 256
N_EXPERTS = 64
N_GROUPS = 8
GROUP_SIZE = N_EXPERTS // N_GROUPS
TOPK_GROUPS = 4
TOP_K = 8
ROUTED_SCALE = 2.5
EXPERT_DIM = 256

MOE_BLOCK = 256
SEG_ALIGN = 16
ORD_SUB = 256
DISP_SUB = 1024
BLOCK_ROWS = -(-(TOP_K * MOE_BLOCK + N_EXPERTS * (SEG_ALIGN - 1)) // DISP_SUB) * DISP_SUB
BLOCK_CHUNKS = BLOCK_ROWS // SEG_ALIGN
FFN_TILE = 1024
FFN_TILE_SMALL = 256

DQ_W = DIFF_HEADS * 2 * DIFF_QK
HEAD_W = DIFF_HEADS * LANES
IN_WIDTH = 3 * DQ_W + Q_LORA + KV_LORA + MLA_ROPE
IN_PAD = 2304
OFF_DK, OFF_DV, OFF_CQ, OFF_CKV, OFF_KPE = 512, 1024, 1536, 1920, 2176


def _cparams(sem):
    return pltpu.CompilerParams(dimension_semantics=sem, vmem_limit_bytes=VMEM_LIMIT)


def _rms(x, g):
    return x * lax.rsqrt(jnp.mean(x * x, axis=-1, keepdims=True) + RMS_EPS) * g


def _layer_norm(x, g, b):
    mu = jnp.mean(x, axis=-1, keepdims=True)
    xc = x - mu
    var = jnp.mean(xc * xc, axis=-1, keepdims=True)
    return xc * lax.rsqrt(var + LN_EPS) * g + b


def _proj_kernel(x_ref, w_ref, wuq_ref, gq_ref, gkv_ref, wuk_ref, place_ref, wuv_ref,
                 cd_ref, sd_ref, cq_ref, sq_ref, ck_ref, sk_ref,
                 qd_ref, kd32_ref, kd16_ref, vd32_ref, vd16_ref, ckv_ref, kpe_ref, qm_ref, km_ref, vm_ref,
                 *, scale_d, scale_m):
    tm = x_ref.shape[0]
    x = x_ref[...].astype(BF16)
    proj = jnp.dot(x, w_ref[...], preferred_element_type=F32)
    lane = lax.broadcasted_iota(jnp.int32, (tm, LANES), 1)

    first_d = (lane & 63) < 32
    cd = cd_ref[...]
    sd = sd_ref[...]

    def rope_d(blk):
        rot = jnp.where(first_d, pltpu.roll(blk, LANES - 32, 1), pltpu.roll(blk, 32, 1))
        return blk * cd + rot * sd

    for j in range(DIFF_HEADS):
        sl = slice(j * LANES, (j + 1) * LANES)
        qd_ref[:, sl] = (rope_d(proj[:, sl]) * scale_d).astype(BF16)
        kr = rope_d(proj[:, OFF_DK + j * LANES:OFF_DK + (j + 1) * LANES])
        kd32_ref[pl.ds(j, tm, stride=DIFF_HEADS), :] = kr
        kd16_ref[:, sl] = kr.astype(BF16)
        dv = proj[:, OFF_DV + j * LANES:OFF_DV + (j + 1) * LANES]
        vd32_ref[pl.ds(j, tm, stride=DIFF_HEADS), :] = dv
        vd16_ref[:, sl] = dv.astype(BF16)

    cqn = _rms(proj[:, OFF_CQ:OFF_CQ + Q_LORA], gq_ref[...])
    q = jnp.dot(cqn.astype(BF16), wuq_ref[...], preferred_element_type=F32)
    cq = cq_ref[...]
    sq = sq_ref[...]
    first_q = lane < (MLA_NOPE + MLA_ROPE // 2)
    for h in range(MLA_HEADS):
        sl = slice(h * LANES, (h + 1) * LANES)
        blk = q[:, sl]
        rot = jnp.where(first_q, pltpu.roll(blk, LANES - 16, 1), pltpu.roll(blk, 16, 1))
        qm_ref[:, sl] = ((blk * cq + rot * sq) * scale_m).astype(BF16)

    ckv_ref[...] = _rms(proj[:, OFF_CKV:OFF_CKV + KV_LORA], gkv_ref[...])

    kb = proj[:, OFF_KPE:OFF_KPE + LANES]
    rot = jnp.where(lane < 16, pltpu.roll(kb, LANES - 16, 1), pltpu.roll(kb, 16, 1))
    kpe_ref[...] = (kb * ck_ref[...] + rot * sk_ref[...])[:, :MLA_ROPE]

    _kvup_kernel(ckv_ref, kpe_ref, wuk_ref, place_ref, wuv_ref, km_ref, vm_ref)


def _proj(x, w_in, wuq, gq, gkv, wuk, place, wuv, tables, tm, n_pat):
    t = x.shape[0]
    row = lambda i: (i, 0)
    const = lambda i: (0, 0)
    pat = lambda i: (i % n_pat, 0)
    tab_spec = pl.BlockSpec((tm, LANES), pat)
    out_w = lambda w, dt: jax.ShapeDtypeStruct((t, w), dt)
    cache_spec = pl.BlockSpec((tm * DIFF_HEADS, LANES), row)
    cache_shape = jax.ShapeDtypeStruct((t * DIFF_HEADS, LANES), F32)
    return pl.pallas_call(
        functools.partial(_proj_kernel, scale_d=LOG2E * DIFF_QK ** -0.5,
                          scale_m=LOG2E * (MLA_NOPE + MLA_ROPE) ** -0.5),
        grid=(t // tm,),
        in_specs=[pl.BlockSpec((tm, x.shape[1]), row),
                  pl.BlockSpec(w_in.shape, const), pl.BlockSpec(wuq.shape, const),
                  pl.BlockSpec(gq.shape, const), pl.BlockSpec(gkv.shape, const),
                  pl.BlockSpec(wuk.shape, const), pl.BlockSpec(place.shape, const), pl.BlockSpec(wuv.shape, const)]
        + [tab_spec] * 6,
        out_specs=[pl.BlockSpec((tm, HEAD_W), row), cache_spec, pl.BlockSpec((tm, HEAD_W), row), cache_spec,
                   pl.BlockSpec((tm, HEAD_W), row), pl.BlockSpec((tm, KV_LORA), row),
                   pl.BlockSpec((tm, MLA_ROPE), row)] + [pl.BlockSpec((tm, HEAD_W), row)] * 3,
        out_shape=[out_w(HEAD_W, BF16), cache_shape, out_w(HEAD_W, BF16), cache_shape,
                   out_w(HEAD_W, BF16), out_w(KV_LORA, F32), out_w(MLA_ROPE, F32)] + [out_w(HEAD_W, BF16)] * 3,
        compiler_params=_cparams(("parallel",)),
        name="proj",
    )(x, w_in, wuq, gq, gkv, wuk, place, wuv, *tables)


def _kvup_kernel(ckv_ref, kpe_ref, wuk_ref, place_ref, wuv_ref, k_ref, v_ref):
    c = ckv_ref[...].astype(BF16)
    k = jnp.dot(c, wuk_ref[...], preferred_element_type=F32)
    k = k + jnp.dot(kpe_ref[...].astype(BF16), place_ref[...], preferred_element_type=F32)
    k_ref[...] = k.astype(BF16)
    v_ref[...] = jnp.dot(c, wuv_ref[...], preferred_element_type=F32).astype(BF16)


def _catcast_kernel(ck_ref, cv_ref, nk_ref, nv_ref, k_ref, v_ref, *, n_past, s_new, tr):
    j = pl.program_id(1)

    @pl.when(j < n_past)
    def _():
        for h in range(DIFF_HEADS):
            sl = slice(h * LANES, (h + 1) * LANES)
            k_ref[:, sl] = ck_ref[pl.ds(h, tr, stride=DIFF_HEADS), :].astype(BF16)
            v_ref[:, sl] = cv_ref[pl.ds(h, tr, stride=DIFF_HEADS), :].astype(BF16)

    @pl.when(j >= n_past)
    def _():
        k_ref[...] = jnp.zeros(k_ref.shape, BF16)
        v_ref[...] = jnp.zeros(v_ref.shape, BF16)
        k_ref[0:s_new, :] = nk_ref[...]
        v_ref[0:s_new, :] = nv_ref[...]


def _catcast(cache_k, cache_v, new_k, new_v, tr):
    b, p = cache_k.shape[:2]
    s = new_k.shape[1]
    assert p % tr == 0 and s <= tr
    n_past = p // tr
    past = pl.BlockSpec((None, tr * DIFF_HEADS, LANES), lambda bi, j: (bi, jnp.minimum(j, n_past - 1), 0))
    new = pl.BlockSpec((None, s, HEAD_W), lambda bi, j: (bi, 0, 0))
    out = pl.BlockSpec((None, tr, HEAD_W), lambda bi, j: (bi, j, 0))
    return pl.pallas_call(
        functools.partial(_catcast_kernel, n_past=n_past, s_new=s, tr=tr),
        grid=(b, n_past + 1),
        in_specs=[past, past, new, new],
        out_specs=[out, out],
        out_shape=[jax.ShapeDtypeStruct((b, p + tr, HEAD_W), BF16)] * 2,
        compiler_params=_cparams(("parallel", "arbitrary")),
        name="catcast",
    )(cache_k.reshape(b, p * DIFF_HEADS, LANES), cache_v.reshape(b, p * DIFF_HEADS, LANES), new_k, new_v)


def _kvup_past_kernel(pc_ref, pp_ref, nc_ref, np_ref, wuk_ref, place_ref, wuv_ref, k_ref, v_ref, c_ref, r_ref,
                      *, n_past, s_new):
    j = pl.program_id(1)

    @pl.when(j < n_past)
    def _():
        c_ref[...] = pc_ref[...]
        r_ref[...] = pp_ref[...]

    @pl.when(j >= n_past)
    def _():
        c_ref[...] = jnp.zeros(c_ref.shape, F32)
        r_ref[...] = jnp.zeros(r_ref.shape, F32)
        c_ref[0:s_new, :] = nc_ref[...]
        r_ref[0:s_new, :] = np_ref[...]

    _kvup_kernel(c_ref, r_ref, wuk_ref, place_ref, wuv_ref, k_ref, v_ref)


def _kvup_past(cache_c, cache_r, new_c, new_r, wuk, place, wuv, tr):
    b, p = cache_c.shape[:2]
    s = new_c.shape[1]
    assert p % tr == 0 and s <= tr
    n_past = p // tr
    clamp = lambda bi, j: (bi, jnp.minimum(j, n_past - 1), 0)
    first = lambda bi, j: (bi, 0, 0)
    const = lambda bi, j: (0, 0)
    out = pl.BlockSpec((None, tr, HEAD_W), lambda bi, j: (bi, j, 0))
    return pl.pallas_call(
        functools.partial(_kvup_past_kernel, n_past=n_past, s_new=s),
        grid=(b, n_past + 1),
        in_specs=[pl.BlockSpec((None, tr, KV_LORA), clamp), pl.BlockSpec((None, tr, MLA_ROPE), clamp),
                  pl.BlockSpec((None, s, KV_LORA), first), pl.BlockSpec((None, s, MLA_ROPE), first),
                  pl.BlockSpec(wuk.shape, const), pl.BlockSpec(place.shape, const), pl.BlockSpec(wuv.shape, const)],
        out_specs=[out, out],
        out_shape=[jax.ShapeDtypeStruct((b, p + tr, HEAD_W), BF16)] * 2,
        scratch_shapes=[pltpu.VMEM((tr, KV_LORA), F32), pltpu.VMEM((tr, MLA_ROPE), F32)],
        compiler_params=_cparams(("parallel", "arbitrary")),
        name="kvup_past",
    )(cache_c, cache_r, new_c, new_r, wuk, place, wuv)


def _flash_kernel(lam_ref, q_ref, k_ref, v_ref, g_ref, o_ref, qs_ref, vx_ref, m_ref, acc_ref,
                  *, n_comp, hps, tq, tk, tkm, sub, sk, q_pos0, out_scale, tiles, static, aligned):
    rows = n_comp * tq
    blocks = [(h, r0) for h in range(hps) for r0 in range(0, rows, sub)]

    @pl.when(pl.program_id(2) == 0)
    def _():
        for h in range(hps):
            vx_ref[:, 2 * h * LANES:(2 * h + 1) * LANES] = v_ref[:, h * LANES:(h + 1) * LANES]
            vx_ref[:, (2 * h + 1) * LANES:(2 * h + 2) * LANES] = jnp.ones((sk, LANES), BF16)

    def absorb(s, start, h, r0):
        rs = slice(h * rows + r0, h * rows + r0 + sub)
        width = s.shape[1]
        vx = vx_ref[pl.ds(start, width), 2 * h * LANES:(2 * h + 2) * LANES]
        m_prev = m_ref[rs, :]
        m_new = jnp.maximum(m_prev, jnp.max(s, axis=-1, keepdims=True))
        alpha = jnp.exp2(m_prev - m_new)
        p = jnp.exp2(s - jnp.concatenate([m_new] * (width // LANES), axis=1))
        pv = jnp.dot(p.astype(BF16), vx, preferred_element_type=F32)
        acc_ref[rs, :] = jnp.concatenate([alpha, alpha], axis=1) * acc_ref[rs, :] + pv
        m_ref[rs, :] = m_new

    def scores(start, width, h, r0):
        rs = slice(h * rows + r0, h * rows + r0 + sub)
        k = k_ref[pl.ds(start, width), h * LANES:(h + 1) * LANES]
        return lax.dot_general(qs_ref[rs, :], k, (((1,), (1,)), ((), ())), preferred_element_type=F32)

    def masked_scores(start, width, h, r0, q0):
        r = (lax.broadcasted_iota(jnp.int32, (sub, width), 0) + r0) & (tq - 1)
        c = lax.broadcasted_iota(jnp.int32, (sub, width), 1)
        ok = ((start + c) >> CHUNK_SHIFT) <= ((q0 + r) >> CHUNK_SHIFT)
        return jnp.where(ok, scores(start, width, h, r0), NEG_BIG)

    def one_tile(qt, q0):
        for h in range(hps):
            q = q_ref[qt * tq:(qt + 1) * tq, h * LANES:(h + 1) * LANES]
            if n_comp == 2:
                lane = lax.broadcasted_iota(jnp.int32, (tq, LANES), 1)
                zero = jnp.zeros_like(q)
                qs_ref[h * rows:h * rows + tq, :] = jnp.where(lane < DIFF_QK, q, zero)
                qs_ref[h * rows + tq:(h + 1) * rows, :] = jnp.where(lane >= DIFF_QK, q, zero)
            else:
                qs_ref[h * rows:(h + 1) * rows, :] = q
        m_ref[...] = jnp.full(m_ref.shape, NEG_BIG, F32)
        acc_ref[...] = jnp.zeros(acc_ref.shape, F32)

        if aligned:
            for j in range(q0 // tk):
                for h, r0 in blocks:
                    absorb(scores(j * tk, tk, h, r0), j * tk, h, r0)
            qc = lax.broadcasted_iota(jnp.int32, (sub, sub), 0) >> CHUNK_SHIFT
            kc = lax.broadcasted_iota(jnp.int32, (sub, sub), 1) >> CHUNK_SHIFT
            for h, r0 in blocks:
                a = r0 % tq
                s = scores(q0, a + sub, h, r0)
                last = jnp.where(kc <= qc, s[:, a:a + sub], NEG_BIG)
                absorb(jnp.concatenate([s[:, 0:a], last], axis=1) if a else last, q0, h, r0)
        elif static:
            lo_vis = min((q0 // CHUNK + 1) * CHUNK, sk)
            hi_vis = min(((q0 + tq - 1) // CHUNK + 1) * CHUNK, sk)
            for j in range(lo_vis // tk):
                for h, r0 in blocks:
                    absorb(scores(j * tk, tk, h, r0), j * tk, h, r0)
            start = lo_vis // tk * tk
            while start < hi_vis:
                width = min(tkm, -(-(hi_vis - start) // LANES) * LANES)
                for h, r0 in blocks:
                    absorb(masked_scores(start, width, h, r0, q0), start, h, r0)
                start += width
        else:
            lo_vis = jnp.minimum(((q0 >> CHUNK_SHIFT) + 1) << CHUNK_SHIFT, sk)
            hi_vis = jnp.minimum((((q0 + tq - 1) >> CHUNK_SHIFT) + 1) << CHUNK_SHIFT, sk)
            n_full = lo_vis // tk

            def full_body(j, carry):
                start = pl.multiple_of(j * tk, tk)
                for h, r0 in blocks:
                    absorb(scores(start, tk, h, r0), start, h, r0)
                return carry

            def masked_body(j, carry):
                start = pl.multiple_of(j * tkm, tkm)
                for h, r0 in blocks:
                    absorb(masked_scores(start, tkm, h, r0, q0), start, h, r0)
                return carry

            lax.fori_loop(0, n_full, full_body, 0)
            lax.fori_loop(n_full * (tk // tkm), (hi_vis + tkm - 1) // tkm, masked_body, 0)

        for h in range(hps):
            hr = slice(h * rows, (h + 1) * rows)
            o = acc_ref[hr, 0:LANES] / acc_ref[hr, LANES:2 * LANES]
            if n_comp == 2:
                o = o[0:tq, :] - lam_ref[0, 0] * o[tq:rows, :]
                o = _rms(o, g_ref[...]) * out_scale
            o_ref[qt * tq:(qt + 1) * tq, h * LANES:(h + 1) * LANES] = o.astype(BF16)

    if static:
        for qt in range(tiles):
            one_tile(qt, q_pos0 + qt * tq)
    else:
        one_tile(0, q_pos0 + pl.program_id(2) * tq)


def _flash(q, k, v, lam, g, *, n_comp, hps, tq, tk, tkm, q_pos0, out_scale):
    b, sq, _ = q.shape
    sk = k.shape[1]
    rows = n_comp * tq
    sub = min(rows, FLASH_SUB)
    assert sq % tq == 0 and sk % tk == 0 and tk % tkm == 0 and tq & (tq - 1) == 0 and rows % sub == 0
    assert DIFF_HEADS % hps == 0
    aligned = q_pos0 % tq == 0 and tq == tk and q_pos0 + sq <= sk and sub % CHUNK == 0 and sub % LANES == 0
    static = aligned or sq == tq
    tiles = sq // tq if static else 1
    hw = hps * LANES
    return pl.pallas_call(
        functools.partial(_flash_kernel, n_comp=n_comp, hps=hps, tq=tq, tk=tk, tkm=tkm, sub=sub, sk=sk,
                          q_pos0=q_pos0, out_scale=out_scale, tiles=tiles, static=static, aligned=aligned),
        grid=(b, DIFF_HEADS // hps, sq // (tq * tiles)),
        in_specs=[pl.BlockSpec(memory_space=pltpu.SMEM),
                  pl.BlockSpec((None, tq * tiles, hw), lambda bi, h, i: (bi, i, h)),
                  pl.BlockSpec((None, sk, hw), lambda bi, h, i: (bi, 0, h)),
                  pl.BlockSpec((None, sk, hw), lambda bi, h, i: (bi, 0, h)),
                  pl.BlockSpec((1, LANES), lambda bi, h, i: (0, 0))],
        out_specs=pl.BlockSpec((None, tq * tiles, hw), lambda bi, h, i: (bi, i, h)),
        out_shape=jax.ShapeDtypeStruct((b, sq, HEAD_W), BF16),
        scratch_shapes=[pltpu.VMEM((hps * rows, LANES), BF16), pltpu.VMEM((sk, 2 * hw), BF16),
                        pltpu.VMEM((hps * rows, LANES), F32), pltpu.VMEM((hps * rows, 2 * LANES), F32)],
        compiler_params=_cparams(("parallel", "parallel", "arbitrary")),
        name="flash_diff" if n_comp == 2 else "flash_mla",
    )(lam, q, k, v, g)


def _sublane_max(x):
    return jnp.max(x, axis=0, keepdims=True)


def _sublane_min(x):
    return jnp.min(x, axis=0, keepdims=True)


def _route_t(scores, biased):
    tm = scores[0].shape[1]
    sub = lax.broadcasted_iota(jnp.int32, (GROUP_SIZE, tm), 0)
    neg_inf = jnp.float32(-jnp.inf)
    gs = jnp.zeros((N_GROUPS, tm), F32)
    for g in range(N_GROUPS):
        bg = biased[g]
        m1 = _sublane_max(bg)
        i1 = _sublane_min(jnp.where(bg == m1, sub, GROUP_SIZE))
        m2 = _sublane_max(jnp.where(sub == i1, neg_inf, bg))
        gs = jnp.where(sub == g, m1 + m2, gs)
    keep = jnp.zeros((N_GROUPS, tm), jnp.bool_)
    cur = gs
    for _ in range(TOPK_GROUPS):
        mx = _sublane_max(cur)
        fi = _sublane_min(jnp.where(cur == mx, sub, N_GROUPS))
        hit = sub == fi
        keep = jnp.logical_or(keep, hit)
        cur = jnp.where(hit, neg_inf, cur)
    keep_f = jnp.where(keep, 1.0, 0.0)
    cand = []
    for g in range(N_GROUPS):
        kg = _sublane_max(jnp.where(sub == g, keep_f, 0.0)) > 0.5
        cand.append(jnp.where(kg, biased[g], neg_inf))
    chosen = [jnp.zeros((GROUP_SIZE, tm), jnp.bool_) for _ in range(N_GROUPS)]
    picks = []
    for _ in range(TOP_K):
        mx = cand[0]
        for g in range(1, N_GROUPS):
            mx = jnp.maximum(mx, cand[g])
        mx = _sublane_max(mx)
        fi = jnp.where(cand[0] == mx, sub, N_EXPERTS)
        for g in range(1, N_GROUPS):
            fi = jnp.minimum(fi, jnp.where(cand[g] == mx, sub + g * GROUP_SIZE, N_EXPERTS))
        fi = _sublane_min(fi)
        picks.append(fi)
        for g in range(N_GROUPS):
            hit = (sub + g * GROUP_SIZE) == fi
            chosen[g] = jnp.logical_or(chosen[g], hit)
            cand[g] = jnp.where(hit, neg_inf, cand[g])
    w = [jnp.where(chosen[g], scores[g], 0.0) for g in range(N_GROUPS)]
    tot = w[0]
    for g in range(1, N_GROUPS):
        tot = tot + w[g]
    tot = jnp.sum(tot, axis=0, keepdims=True)
    return [wg / tot * ROUTED_SCALE for wg in w], chosen, picks


def _mix_kernel(od_ref, om_ref, x_ref, wo_ref, g_ref, b_ref, wr_ref, wrl_ref, br_ref, before_ref, below_ref, eye_ref,
                x1_ref, x1b_ref, sk_ref, cnt_ref, *, alpha):
    mix = jnp.dot(od_ref[...], wo_ref[0:HEAD_W, :], preferred_element_type=F32)
    mix = mix + jnp.dot(om_ref[...], wo_ref[HEAD_W:2 * HEAD_W, :], preferred_element_type=F32)
    x1 = _layer_norm(alpha * x_ref[...] + mix, g_ref[...], b_ref[...])
    x_hi = x1.astype(BF16)
    x1_ref[...] = x1
    x1b_ref[...] = x_hi
    x_lo = (x1 - x_hi.astype(F32)).astype(BF16)
    nt = (((1,), (1,)), ((), ()))
    logits = (lax.dot_general(wr_ref[...], x_hi, nt, preferred_element_type=F32)
              + lax.dot_general(wr_ref[...], x_lo, nt, preferred_element_type=F32)
              + lax.dot_general(wrl_ref[...], x_hi, nt, preferred_element_type=F32))
    sc = 1.0 / (1.0 + jnp.exp(-logits))
    bi = sc + br_ref[...]
    scores = [sc[g * GROUP_SIZE:(g + 1) * GROUP_SIZE, :] for g in range(N_GROUPS)]
    biased = [bi[g * GROUP_SIZE:(g + 1) * GROUP_SIZE, :] for g in range(N_GROUPS)]
    gates, chosen, picks = _route_t(scores, biased)
    tm = x1.shape[0]
    ch = jnp.concatenate([jnp.where(c, 1.0, 0.0) for c in chosen], axis=0)
    gate_all = jnp.concatenate(gates, axis=0)
    sub = lax.broadcasted_iota(jnp.int32, (N_EXPERTS, MOE_BLOCK), 0)
    for blk in range(tm // MOE_BLOCK):
        ls = slice(blk * MOE_BLOCK, (blk + 1) * MOE_BLOCK)
        chb = ch[:, ls]
        rank = jnp.dot(chb.astype(BF16), before_ref[...], preferred_element_type=F32)
        cnt = jnp.sum(chb, axis=1, keepdims=True)
        seg = jnp.floor((cnt + (SEG_ALIGN - 1.0)) * (1.0 / SEG_ALIGN)) * SEG_ALIGN
        seg_b = jnp.broadcast_to(seg, (N_EXPERTS, LANES))
        off = jnp.dot(below_ref[...], seg_b, precision=lax.Precision.HIGHEST, preferred_element_type=F32)
        slot = jnp.concatenate([off] * (MOE_BLOCK // LANES), axis=1) + rank
        cnt_ref[blk] = jnp.sum(seg_b * eye_ref[...], axis=0, keepdims=True)
        gb = gate_all[:, ls]
        for k in range(TOP_K):
            hit = sub == picks[k][:, ls]
            sk_ref[k:k + 1, ls] = jnp.sum(jnp.where(hit, slot, 0.0), axis=0, keepdims=True)
            sk_ref[TOP_K + k:TOP_K + k + 1, ls] = jnp.sum(jnp.where(hit, gb, 0.0), axis=0, keepdims=True)


def _mix(od, om, x, wo, g, b, wr_hi, wr_lo, br, tm, alpha):
    t = x.shape[0]
    d = x.shape[1]
    nblk = tm // MOE_BLOCK
    row = lambda i: (i, 0)
    const = lambda i: (0, 0)
    idx = jnp.arange(MOE_BLOCK)
    before = (idx[:, None] < idx[None, :]).astype(BF16)
    ide = jnp.arange(N_EXPERTS)
    below = (ide[None, :] < ide[:, None]).astype(F32)
    eye = (ide[:, None] == jnp.arange(LANES)[None, :]).astype(F32)
    return pl.pallas_call(
        functools.partial(_mix_kernel, alpha=alpha),
        grid=(t // tm,),
        in_specs=[pl.BlockSpec((tm, HEAD_W), row), pl.BlockSpec((tm, HEAD_W), row), pl.BlockSpec((tm, d), row),
                  pl.BlockSpec(wo.shape, const), pl.BlockSpec(g.shape, const), pl.BlockSpec(b.shape, const),
                  pl.BlockSpec(wr_hi.shape, const), pl.BlockSpec(wr_lo.shape, const), pl.BlockSpec(br.shape, const),
                  pl.BlockSpec(before.shape, const), pl.BlockSpec(below.shape, const), pl.BlockSpec(eye.shape, const)],
        out_specs=[pl.BlockSpec((tm, d), row), pl.BlockSpec((tm, d), row),
                   pl.BlockSpec((2 * TOP_K, tm), lambda i: (0, i)),
                   pl.BlockSpec((nblk, 1, LANES), lambda i: (i, 0, 0))],
        out_shape=[jax.ShapeDtypeStruct((t, d), F32), jax.ShapeDtypeStruct((t, d), BF16),
                   jax.ShapeDtypeStruct((2 * TOP_K, t), F32),
                   jax.ShapeDtypeStruct((t // MOE_BLOCK, 1, LANES), F32)],
        compiler_params=_cparams(("parallel",)),
        name="mix",
    )(od, om, x, wo, g, b, wr_hi, wr_lo, br, before, below, eye)


def _moe_plan(cnt, n_tiles, tile, trash_row):
    seg = cnt.astype(jnp.int32)
    nb = seg.shape[0]
    before_blocks = jnp.cumsum(seg, axis=0) - seg
    length = jnp.sum(seg, axis=0)
    padded = -(-length // tile) * tile
    ends = jnp.cumsum(padded)
    start = ends - padded
    dst = start[None, :] + before_blocks
    off = (jnp.cumsum(seg, axis=1) - seg) // SEG_ALIGN
    tot = jnp.sum(seg, axis=1) // SEG_ALIGN
    chunk = jnp.arange(BLOCK_CHUNKS, dtype=jnp.int32)
    base = dst - SEG_ALIGN * off
    step = jnp.concatenate([base[:, :1], base[:, 1:] - base[:, :-1]], axis=1)
    started = (off[:, None, :] <= chunk[None, :, None]).astype(jnp.int32)
    row = jnp.sum(started * step[:, None, :], axis=2) + SEG_ALIGN * chunk[None, :]
    valid = chunk[None, :] < tot[:, None]
    trash = trash_row + ((jnp.arange(nb, dtype=jnp.int32) % 3) * BLOCK_ROWS)[:, None] + SEG_ALIGN * chunk[None, :]
    first = trash_row + 2 * BLOCK_ROWS + SEG_ALIGN * chunk[None, :]
    put = jnp.concatenate([first, jnp.where(valid, row, trash)], axis=0).reshape(-1)
    get = jnp.concatenate([jnp.where(valid, row, 0), jnp.zeros((2, BLOCK_CHUNKS), jnp.int32)], axis=0).reshape(-1)
    n_used = ends[-1] // tile
    tile_start = jnp.arange(n_tiles, dtype=jnp.int32) * tile
    tile_e = jnp.sum((ends[None, :] <= tile_start[:, None]).astype(jnp.int32), axis=1)
    last_e = tile_e[jnp.maximum(n_used - 1, 0)]
    tile_e = jnp.where(jnp.arange(n_tiles) < n_used, tile_e, last_e)
    tail_start = start + length
    tail_nch = (padded - length) // SEG_ALIGN
    return dict(put=put, get=get, n_used=n_used.reshape(1), tile_e=tile_e, tail_start=tail_start,
                tail_nch=tail_nch, tail_tot=jnp.sum(tail_nch).reshape(1))


def _drain(copy, n):
    def body(i, c):
        copy.wait()
        return c
    lax.fori_loop(0, n, body, 0)


def _order_rows(sk, j0, vals):
    rows = lax.broadcasted_iota(jnp.int32, (ORD_SUB, MOE_BLOCK), 0).astype(F32).astype(BF16)
    out = jnp.zeros((ORD_SUB, MOE_BLOCK), BF16)
    for k in range(TOP_K):
        rel = (sk[k:k + 1, :] - j0).astype(BF16)
        out = jnp.where(rows == rel, vals[k], out)
    return out


def _dispatch_kernel(put_ref, tstart_ref, tnch_ref, ttot_ref, x_ref, sk_ref, xs_ref, buf_ref, zero_ref, sem, tail_sem):
    b = pl.program_id(0)
    last = pl.num_programs(0) - 1
    slot = lax.rem(b, 3)
    send = lax.rem(b + 2, 3)

    def whole(s):
        return pltpu.make_async_copy(buf_ref.at[s], xs_ref.at[pl.ds(0, BLOCK_ROWS), :], sem.at[s])

    @pl.when(b == 0)
    def _():
        buf_ref[...] = jnp.zeros(buf_ref.shape, BF16)

    @pl.when(b >= 2)
    def _():
        whole(slot).wait()

    x = x_ref[...]
    sk = sk_ref[0:TOP_K, :]
    ones = [jnp.ones((1, MOE_BLOCK), BF16)] * TOP_K
    base = b * BLOCK_CHUNKS
    for sub in range(BLOCK_ROWS // DISP_SUB):
        j0 = sub * DISP_SUB
        sel = jnp.concatenate([_order_rows(sk, float(j0 + i * ORD_SUB), ones) for i in range(DISP_SUB // ORD_SUB)],
                              axis=0)
        buf_ref[slot, j0:j0 + DISP_SUB, :] = jnp.dot(sel, x, preferred_element_type=F32).astype(BF16)
        for c in range(sub * (DISP_SUB // SEG_ALIGN), (sub + 1) * (DISP_SUB // SEG_ALIGN)):
            pltpu.make_async_copy(
                buf_ref.at[send, c * SEG_ALIGN:(c + 1) * SEG_ALIGN, :],
                xs_ref.at[pl.ds(pl.multiple_of(put_ref[base + c], SEG_ALIGN), SEG_ALIGN), :], sem.at[send]).start()

    @pl.when(b == last)
    def _():
        whole(send).wait()

        @pl.when(b >= 1)
        def _():
            whole(lax.rem(b + 1, 3)).wait()

        zero_ref[...] = jnp.zeros(zero_ref.shape, BF16)

        def tail_copy(dst_row):
            return pltpu.make_async_copy(zero_ref, xs_ref.at[pl.ds(pl.multiple_of(dst_row, SEG_ALIGN), SEG_ALIGN), :],
                                         tail_sem.at[0])

        def tail_expert(e, carry):
            def tail_chunk(c, carry2):
                tail_copy(tstart_ref[e] + c * SEG_ALIGN).start()
                return carry2
            lax.fori_loop(0, tnch_ref[e], tail_chunk, 0)
            return carry

        lax.fori_loop(0, N_EXPERTS, tail_expert, 0)
        _drain(tail_copy(0), ttot_ref[0])


def _dispatch(plan, x1b, sk, n_rows):
    t, d = x1b.shape
    nb = t // MOE_BLOCK
    grid_spec = pltpu.PrefetchScalarGridSpec(
        num_scalar_prefetch=4,
        grid=(nb + 1,),
        in_specs=[pl.BlockSpec((MOE_BLOCK, d), lambda b, *_: (jnp.minimum(b, nb - 1), 0)),
                  pl.BlockSpec((2 * TOP_K, MOE_BLOCK), lambda b, *_: (0, jnp.minimum(b, nb - 1)))],
        out_specs=pl.BlockSpec(memory_space=pl.ANY),
        scratch_shapes=[pltpu.VMEM((3, BLOCK_ROWS, d), BF16), pltpu.VMEM((SEG_ALIGN, d), BF16),
                        pltpu.SemaphoreType.DMA((3,)), pltpu.SemaphoreType.DMA((1,))],
    )
    return pl.pallas_call(
        _dispatch_kernel,
        grid_spec=grid_spec,
        out_shape=jax.ShapeDtypeStruct((n_rows, d), BF16),
        compiler_params=_cparams(("arbitrary",)),
        name="dispatch",
    )(plan["put"], plan["tail_start"], plan["tail_nch"], plan["tail_tot"], x1b, sk)


def _ffn_kernel(te_ref, nu_ref, xs_ref, wg_ref, wu_ref, wd_ref, y_ref):
    @pl.when(pl.program_id(0) < nu_ref[0])
    def _():
        x = xs_ref[...]
        hg = jnp.dot(x, wg_ref[...], preferred_element_type=F32)
        hu = jnp.dot(x, wu_ref[...], preferred_element_type=F32)
        h = hg * (1.0 / (1.0 + jnp.exp(-hg))) * hu
        y_ref[...] = jnp.dot(h.astype(BF16), wd_ref[...], preferred_element_type=F32).astype(BF16)


def _ffn(plan, xs, wg, wu, wd, n_tiles, tile):
    d = xs.shape[1]
    used = lambda i, te, nu: (jnp.minimum(i, nu[0] - 1), 0)
    wsel = lambda i, te, nu: (te[i], 0, 0)
    grid_spec = pltpu.PrefetchScalarGridSpec(
        num_scalar_prefetch=2,
        grid=(n_tiles,),
        in_specs=[pl.BlockSpec((tile, d), used),
                  pl.BlockSpec((None, d, EXPERT_DIM), wsel), pl.BlockSpec((None, d, EXPERT_DIM), wsel),
                  pl.BlockSpec((None, EXPERT_DIM, d), wsel)],
        out_specs=pl.BlockSpec((tile, d), used),
    )
    return pl.pallas_call(
        _ffn_kernel,
        grid_spec=grid_spec,
        out_shape=jax.ShapeDtypeStruct((n_tiles * tile, d), BF16),
        compiler_params=_cparams(("arbitrary",)),
        name="ffn",
    )(plan["tile_e"], plan["n_used"], xs, wg, wu, wd)


def _combine_kernel(get_ref, y_ref, sk_ref, x1_ref, x1b_ref, wsg_ref, wsu_ref, wsd_ref,
                    g_ref, b_ref, o_ref, buf_ref, sem, *, alpha):
    b = pl.program_id(0)
    slot = lax.rem(b, 3)
    nxt = lax.rem(b + 2, 3)

    def whole(s):
        return pltpu.make_async_copy(y_ref.at[pl.ds(0, BLOCK_ROWS), :], buf_ref.at[s], sem.at[s])

    @pl.when(b == 0)
    def _():
        def first(c, carry):
            s = c // BLOCK_CHUNKS
            pltpu.make_async_copy(
                y_ref.at[pl.ds(pl.multiple_of(get_ref[c], SEG_ALIGN), SEG_ALIGN), :],
                buf_ref.at[s, pl.ds(pl.multiple_of((c - s * BLOCK_CHUNKS) * SEG_ALIGN, SEG_ALIGN), SEG_ALIGN), :],
                sem.at[s]).start()
            return carry
        lax.fori_loop(0, 2 * BLOCK_CHUNKS, first, 0)

    whole(slot).wait()

    sk = sk_ref[0:TOP_K, :]
    gates = [sk_ref[TOP_K + k:TOP_K + k + 1, :].astype(BF16) for k in range(TOP_K)]
    base = (b + 2) * BLOCK_CHUNKS
    acc = None
    for sub in range(BLOCK_ROWS // DISP_SUB):
        j0 = sub * DISP_SUB
        w = jnp.concatenate([_order_rows(sk, float(j0 + i * ORD_SUB), gates) for i in range(DISP_SUB // ORD_SUB)],
                            axis=0)
        part = lax.dot_general(w, buf_ref[slot, j0:j0 + DISP_SUB, :], (((0,), (0,)), ((), ())),
                               preferred_element_type=F32)
        acc = part if acc is None else acc + part
        for c in range(sub * (DISP_SUB // SEG_ALIGN), (sub + 1) * (DISP_SUB // SEG_ALIGN)):
            pltpu.async_copy(
                y_ref.at[pl.ds(pl.multiple_of(get_ref[base + c], SEG_ALIGN), SEG_ALIGN), :],
                buf_ref.at[nxt, c * SEG_ALIGN:(c + 1) * SEG_ALIGN, :], sem.at[nxt], priority=c % 2)

    xb = x1b_ref[...]
    hg = jnp.dot(xb, wsg_ref[...], preferred_element_type=F32)
    hu = jnp.dot(xb, wsu_ref[...], preferred_element_type=F32)
    h = hg * (1.0 / (1.0 + jnp.exp(-hg))) * hu
    acc = acc + jnp.dot(h.astype(BF16), wsd_ref[...], preferred_element_type=F32)
    o_ref[...] = _layer_norm(alpha * x1_ref[...] + acc, g_ref[...], b_ref[...])

    @pl.when(b == pl.num_programs(0) - 1)
    def _():
        whole(lax.rem(b + 1, 3)).wait()
        whole(nxt).wait()


def _combine(plan, y, sk, x1, x1b, wsg, wsu, wsd, g, b, alpha):
    t, d = x1.shape
    nb = t // MOE_BLOCK
    row = lambda i, *_: (i, 0)
    const = lambda i, *_: (0, 0)
    grid_spec = pltpu.PrefetchScalarGridSpec(
        num_scalar_prefetch=1,
        grid=(nb,),
        in_specs=[pl.BlockSpec(memory_space=pl.ANY), pl.BlockSpec((2 * TOP_K, MOE_BLOCK), lambda i, *_: (0, i)),
                  pl.BlockSpec((MOE_BLOCK, d), row), pl.BlockSpec((MOE_BLOCK, d), row),
                  pl.BlockSpec(wsg.shape, const), pl.BlockSpec(wsu.shape, const), pl.BlockSpec(wsd.shape, const),
                  pl.BlockSpec(g.shape, const), pl.BlockSpec(b.shape, const)],
        out_specs=pl.BlockSpec((MOE_BLOCK, d), row),
        scratch_shapes=[pltpu.VMEM((3, BLOCK_ROWS, d), BF16), pltpu.SemaphoreType.DMA((3,))],
    )
    return pl.pallas_call(
        functools.partial(_combine_kernel, alpha=alpha),
        grid_spec=grid_spec,
        out_shape=jax.ShapeDtypeStruct((t, d), F32),
        compiler_params=_cparams(("arbitrary",)),
        name="combine",
    )(plan["get"], y, sk, x1, x1b, wsg, wsu, wsd, g, b)


def _moe(x1, x1b, sk, cnt, wg, wu, wd, wsg, wsu, wsd, g, b, alpha):
    t = x1.shape[0]
    nb = t // MOE_BLOCK
    tile = FFN_TILE if (TOP_K * t) // N_EXPERTS >= FFN_TILE else FFN_TILE_SMALL
    n_tiles = -(-(nb * BLOCK_ROWS) // tile) + N_EXPERTS
    n_rows = n_tiles * tile + 3 * BLOCK_ROWS
    plan = _moe_plan(cnt[:, 0, :N_EXPERTS], n_tiles, tile, n_tiles * tile)
    xs = _dispatch(plan, x1b, sk, n_rows)
    y = _ffn(plan, xs, wg, wu, wd, n_tiles, tile)
    return _combine(plan, y, sk, x1, x1b, wsg, wsu, wsd, g, b, alpha)


def _rope_cs(pos, dim):
    inv = ROPE_THETA ** (-jnp.arange(0, dim, 2, dtype=F32) / dim)
    ang = pos.astype(F32)[:, None] * inv[None, :]
    return jnp.cos(ang), jnp.sin(ang)


def _tables(pos, reps):
    n = pos.shape[0]
    c32, s32 = _rope_cs(pos, DIFF_QK)
    c16, s16 = _rope_cs(pos, MLA_ROPE)
    one = lambda w: jnp.ones((n, w), F32)
    zero = lambda w: jnp.zeros((n, w), F32)
    cd = jnp.concatenate([c32] * 4, axis=1)
    sd = jnp.concatenate([-s32, s32] * 2, axis=1)
    cq = jnp.concatenate([one(MLA_NOPE), c16, c16, one(32)], axis=1)
    sq = jnp.concatenate([zero(MLA_NOPE), -s16, s16, zero(32)], axis=1)
    ck = jnp.concatenate([c16, c16, zero(96)], axis=1)
    sk = jnp.concatenate([-s16, s16, zero(96)], axis=1)
    return tuple(jnp.tile(a, (reps, 1)) for a in (cd, sd, cq, sq, ck, sk))


def kernel(x_prompt, x_sample, cache_diff_k, cache_diff_v, cache_mla_ckv, cache_mla_kpe, w_in, diff_lambda, diff_subln_g, mla_q_norm_g, mla_w_uq, mla_kv_norm_g, mla_w_ukv, w_out, ln1_g, ln1_b, w_router, b_router, w_exp_gate, w_exp_up, w_exp_down, w_sh_gate, w_sh_up, w_sh_down, ln2_g, ln2_b):
    depth = w_in.shape[0]
    assert depth == 1
    d_model = x_prompt.shape[-1]
    alpha = (2.0 * depth) ** 0.25
    past_len = cache_diff_k.shape[2]
    layer = 0
    lambda_init = 0.8 - 0.6 * math.exp(-0.3 * layer)

    w_in_b = jnp.pad(w_in[layer], ((0, 0), (0, IN_PAD - IN_WIDTH))).astype(BF16)
    wuq = jnp.pad(mla_w_uq[layer], ((0, 0), (0, 0), (0, LANES - MLA_NOPE - MLA_ROPE)))
    wuq = wuq.reshape(Q_LORA, HEAD_W).astype(BF16)
    wukv = mla_w_ukv[layer]
    wuk = jnp.pad(wukv[:, :, :MLA_NOPE], ((0, 0), (0, 0), (0, LANES - MLA_NOPE))).reshape(KV_LORA, HEAD_W).astype(BF16)
    wuv = wukv[:, :, MLA_NOPE:].reshape(KV_LORA, HEAD_W).astype(BF16)
    place = jnp.pad(jnp.eye(MLA_ROPE, dtype=F32), ((0, 0), (MLA_NOPE, LANES - MLA_NOPE - MLA_ROPE)))
    place = jnp.tile(place, (1, MLA_HEADS)).astype(BF16)
    gq = mla_q_norm_g[layer].reshape(1, Q_LORA)
    gkv = mla_kv_norm_g[layer].reshape(1, KV_LORA)
    gsub = diff_subln_g[layer].reshape(1, LANES)
    wo = w_out[layer].astype(BF16)
    g1, b1 = ln1_g[layer].reshape(1, d_model), ln1_b[layer].reshape(1, d_model)
    g2, b2 = ln2_g[layer].reshape(1, d_model), ln2_b[layer].reshape(1, d_model)
    wr_t = w_router[layer].T
    wr_hi = wr_t.astype(BF16)
    wr_lo = (wr_t - wr_hi.astype(F32)).astype(BF16)
    br = b_router[layer].reshape(N_EXPERTS, 1)
    wg, wu, wd = (w[layer].astype(BF16) for w in (w_exp_gate, w_exp_up, w_exp_down))
    wsg, wsu, wsd = (w[layer].astype(BF16) for w in (w_sh_gate, w_sh_up, w_sh_down))
    lp = diff_lambda[layer].astype(F32)
    lam = jnp.exp(jnp.sum(lp[0] * lp[1])) - jnp.exp(jnp.sum(lp[2] * lp[3])) + lambda_init
    lam = lam.reshape(1, 1)

    def group(x, pos, past, tm, tq, tk, tkm, hps):
        b, s, _ = x.shape
        t = b * s
        tm = min(tm, t)
        xf = x.reshape(t, d_model)
        reps = max(1, tm // s)
        tables = _tables(pos, reps)
        n_pat = (s * reps) // tm
        qd, kd32, kd16, vd32, vd16, ckv, kpe, qm, km, vm = _proj(xf, w_in_b, wuq, gq, gkv, wuk, place, wuv,
                                                                 tables, tm, n_pat)
        if past is None:
            k_d, v_d = kd16.reshape(b, s, HEAD_W), vd16.reshape(b, s, HEAD_W)
            k_m, v_m = km.reshape(b, s, HEAD_W), vm.reshape(b, s, HEAD_W)
            q_pos0 = 0
        else:
            pk, pv, pc, pp = past
            k_d, v_d = _catcast(pk, pv, kd16.reshape(b, s, HEAD_W), vd16.reshape(b, s, HEAD_W), tk)
            k_m, v_m = _kvup_past(pc, pp, ckv.reshape(b, s, KV_LORA), kpe.reshape(b, s, MLA_ROPE),
                                  wuk, place, wuv, tk)
            q_pos0 = past_len
        o_d = _flash(qd.reshape(b, s, HEAD_W), k_d, v_d, lam, gsub, n_comp=2, hps=hps, tq=tq, tk=tk, tkm=tkm,
                     q_pos0=q_pos0, out_scale=1.0 - lambda_init)
        o_m = _flash(qm.reshape(b, s, HEAD_W), k_m, v_m, lam, gsub,
                     n_comp=1, hps=hps, tq=tq, tk=tk, tkm=tkm, q_pos0=q_pos0, out_scale=1.0)
        x1, x1b, slots, cnt = _mix(o_d.reshape(t, HEAD_W), o_m.reshape(t, HEAD_W), xf, wo, g1, b1, wr_hi, wr_lo, br,
                                   tm, alpha)
        y = _moe(x1, x1b, slots, cnt, wg, wu, wd, wsg, wsu, wsd, g2, b2, alpha)
        rows = (kd32.reshape(1, b, s, DIFF_HEADS, LANES), vd32.reshape(1, b, s, DIFF_HEADS, LANES),
                ckv.reshape(1, b, s, KV_LORA), kpe.reshape(1, b, s, MLA_ROPE))
        return y.reshape(b, s, d_model), rows

    s_p = x_prompt.shape[1]
    s_s = x_sample.shape[1]
    pos_p = jnp.arange(s_p, dtype=jnp.int32)
    pos_s = past_len + jnp.arange(s_s, dtype=jnp.int32)
    y_p, r_p = group(x_prompt, pos_p, None, 512, 256, 256, 256, DIFF_HEADS)
    past = (cache_diff_k[layer], cache_diff_v[layer], cache_mla_ckv[layer], cache_mla_kpe[layer])
    y_s, r_s = group(x_sample, pos_s, past, 512, s_s, 512, 256, DIFF_HEADS)
    return (y_p, y_s) + r_p + r_s
```

```python
import functools
import math

import jax
import jax.numpy as jnp
from jax import lax
from jax.experimental import pallas as pl
from jax.experimental.pallas import tpu as pltpu

F32 = jnp.float32
BF16 = jnp.bfloat16

LANES = 128
VMEM_LIMIT = 52 * 1024 * 1024

CHUNK = 64
CHUNK_SHIFT = 6
ROPE_THETA = 10000.0
LN_EPS = 1e-5
RMS_EPS = 1e-6
LOG2E = 1.4426950408889634
NEG_BIG = -1e30
FLASH_SUB = 256

DIFF_HEADS = 4
DIFF_QK = 64
MLA_HEADS = 4
MLA_NOPE = 64
MLA_ROPE = 32
MLA_V = 128
Q_LORA = 384
KV_LORA = 256
N_EXPERTS = 64
N_GROUPS = 8
GROUP_SIZE = N_EXPERTS // N_GROUPS
TOPK_GROUPS = 4
TOP_K = 8
ROUTED_SCALE = 2.5
EXPERT_DIM = 256

MOE_BLOCK = 256
SEG_ALIGN = 16
ORD_SUB = 256
DISP_SUB = 1024
BLOCK_ROWS = -(-(TOP_K * MOE_BLOCK + N_EXPERTS * (SEG_ALIGN - 1)) // DISP_SUB) * DISP_SUB
BLOCK_CHUNKS = BLOCK_ROWS // SEG_ALIGN
FFN_TILE = 1024
FFN_TILE_SMALL = 256

DQ_W = DIFF_HEADS * 2 * DIFF_QK
HEAD_W = DIFF_HEADS * LANES
IN_WIDTH = 3 * DQ_W + Q_LORA + KV_LORA + MLA_ROPE
IN_PAD = 2304
OFF_DK, OFF_DV, OFF_CQ, OFF_CKV, OFF_KPE = 512, 1024, 1536, 1920, 2176


def _cparams(sem):
    return pltpu.CompilerParams(dimension_semantics=sem, vmem_limit_bytes=VMEM_LIMIT)


def _rms(x, g):
    return x * lax.rsqrt(jnp.mean(x * x, axis=-1, keepdims=True) + RMS_EPS) * g


def _layer_norm(x, g, b):
    mu = jnp.mean(x, axis=-1, keepdims=True)
    xc = x - mu
    var = jnp.mean(xc * xc, axis=-1, keepdims=True)
    return xc * lax.rsqrt(var + LN_EPS) * g + b


def _proj_kernel(x_ref, w_ref, wuq_ref, gq_ref, gkv_ref, cd_ref, sd_ref, cq_ref, sq_ref, ck_ref, sk_ref,
                 qd_ref, kd32_ref, kd16_ref, vd32_ref, vd16_ref, ckv_ref, kpe_ref, qm_ref,
                 *, scale_d, scale_m):
    tm = x_ref.shape[0]
    x = x_ref[...].astype(BF16)
    proj = jnp.dot(x, w_ref[...], preferred_element_type=F32)
    lane = lax.broadcasted_iota(jnp.int32, (tm, LANES), 1)

    first_d = (lane & 63) < 32
    cd = cd_ref[...]
    sd = sd_ref[...]

    def rope_d(blk):
        rot = jnp.where(first_d, pltpu.roll(blk, LANES - 32, 1), pltpu.roll(blk, 32, 1))
        return blk * cd + rot * sd

    for j in range(DIFF_HEADS):
        sl = slice(j * LANES, (j + 1) * LANES)
        qd_ref[:, sl] = (rope_d(proj[:, sl]) * scale_d).astype(BF16)
        kr = rope_d(proj[:, OFF_DK + j * LANES:OFF_DK + (j + 1) * LANES])
        kd32_ref[pl.ds(j, tm, stride=DIFF_HEADS), :] = kr
        kd16_ref[:, sl] = kr.astype(BF16)
        dv = proj[:, OFF_DV + j * LANES:OFF_DV + (j + 1) * LANES]
        vd32_ref[pl.ds(j, tm, stride=DIFF_HEADS), :] = dv
        vd16_ref[:, sl] = dv.astype(BF16)

    cqn = _rms(proj[:, OFF_CQ:OFF_CQ + Q_LORA], gq_ref[...])
    q = jnp.dot(cqn.astype(BF16), wuq_ref[...], preferred_element_type=F32)
    cq = cq_ref[...]
    sq = sq_ref[...]
    first_q = lane < (MLA_NOPE + MLA_ROPE // 2)
    for h in range(MLA_HEADS):
        sl = slice(h * LANES, (h + 1) * LANES)
        blk = q[:, sl]
        rot = jnp.where(first_q, pltpu.roll(blk, LANES - 16, 1), pltpu.roll(blk, 16, 1))
        qm_ref[:, sl] = ((blk * cq + rot * sq) * scale_m).astype(BF16)

    ckv_ref[...] = _rms(proj[:, OFF_CKV:OFF_CKV + KV_LORA], gkv_ref[...])

    kb = proj[:, OFF_KPE:OFF_KPE + LANES]
    rot = jnp.where(lane < 16, pltpu.roll(kb, LANES - 16, 1), pltpu.roll(kb, 16, 1))
    kpe_ref[...] = (kb * ck_ref[...] + rot * sk_ref[...])[:, :MLA_ROPE]


def _proj(x, w_in, wuq, gq, gkv, tables, tm, n_pat):
    t = x.shape[0]
    row = lambda i: (i, 0)
    const = lambda i: (0, 0)
    pat = lambda i: (i % n_pat, 0)
    tab_spec = pl.BlockSpec((tm, LANES), pat)
    out_w = lambda w, dt: jax.ShapeDtypeStruct((t, w), dt)
    cache_spec = pl.BlockSpec((tm * DIFF_HEADS, LANES), row)
    cache_shape = jax.ShapeDtypeStruct((t * DIFF_HEADS, LANES), F32)
    return pl.pallas_call(
        functools.partial(_proj_kernel, scale_d=LOG2E * DIFF_QK ** -0.5,
                          scale_m=LOG2E * (MLA_NOPE + MLA_ROPE) ** -0.5),
        grid=(t // tm,),
        in_specs=[pl.BlockSpec((tm, x.shape[1]), row),
                  pl.BlockSpec(w_in.shape, const), pl.BlockSpec(wuq.shape, const),
                  pl.BlockSpec(gq.shape, const), pl.BlockSpec(gkv.shape, const)] + [tab_spec] * 6,
        out_specs=[pl.BlockSpec((tm, HEAD_W), row), cache_spec, pl.BlockSpec((tm, HEAD_W), row), cache_spec,
                   pl.BlockSpec((tm, HEAD_W), row), pl.BlockSpec((tm, KV_LORA), row),
                   pl.BlockSpec((tm, MLA_ROPE), row), pl.BlockSpec((tm, HEAD_W), row)],
        out_shape=[out_w(HEAD_W, BF16), cache_shape, out_w(HEAD_W, BF16), cache_shape,
                   out_w(HEAD_W, BF16), out_w(KV_LORA, F32), out_w(MLA_ROPE, F32), out_w(HEAD_W, BF16)],
        compiler_params=_cparams(("parallel",)),
        name="proj",
    )(x, w_in, wuq, gq, gkv, *tables)


def _kvup_kernel(ckv_ref, kpe_ref, wuk_ref, place_ref, wuv_ref, k_ref, v_ref):
    c = ckv_ref[...].astype(BF16)
    k = jnp.dot(c, wuk_ref[...], preferred_element_type=F32)
    k = k + jnp.dot(kpe_ref[...].astype(BF16), place_ref[...], preferred_element_type=F32)
    k_ref[...] = k.astype(BF16)
    v_ref[...] = jnp.dot(c, wuv_ref[...], preferred_element_type=F32).astype(BF16)


def _kvup(ckv, kpe, wuk, place, wuv, tm):
    r = ckv.shape[0]
    row = lambda i: (i, 0)
    const = lambda i: (0, 0)
    return pl.pallas_call(
        _kvup_kernel,
        grid=(r // tm,),
        in_specs=[pl.BlockSpec((tm, KV_LORA), row), pl.BlockSpec((tm, MLA_ROPE), row),
                  pl.BlockSpec(wuk.shape, const), pl.BlockSpec(place.shape, const), pl.BlockSpec(wuv.shape, const)],
        out_specs=[pl.BlockSpec((tm, HEAD_W), row)] * 2,
        out_shape=[jax.ShapeDtypeStruct((r, HEAD_W), BF16)] * 2,
        compiler_params=_cparams(("parallel",)),
        name="kvup",
    )(ckv, kpe, wuk, place, wuv)


def _catcast_kernel(ck_ref, cv_ref, nk_ref, nv_ref, k_ref, v_ref, *, n_past, s_new, tr):
    j = pl.program_id(1)

    @pl.when(j < n_past)
    def _():
        for h in range(DIFF_HEADS):
            sl = slice(h * LANES, (h + 1) * LANES)
            k_ref[:, sl] = ck_ref[pl.ds(h, tr, stride=DIFF_HEADS), :].astype(BF16)
            v_ref[:, sl] = cv_ref[pl.ds(h, tr, stride=DIFF_HEADS), :].astype(BF16)

    @pl.when(j >= n_past)
    def _():
        k_ref[...] = jnp.zeros(k_ref.shape, BF16)
        v_ref[...] = jnp.zeros(v_ref.shape, BF16)
        k_ref[0:s_new, :] = nk_ref[...]
        v_ref[0:s_new, :] = nv_ref[...]


def _catcast(cache_k, cache_v, new_k, new_v, tr):
    b, p = cache_k.shape[:2]
    s = new_k.shape[1]
    assert p % tr == 0 and s <= tr
    n_past = p // tr
    past = pl.BlockSpec((None, tr * DIFF_HEADS, LANES), lambda bi, j: (bi, jnp.minimum(j, n_past - 1), 0))
    new = pl.BlockSpec((None, s, HEAD_W), lambda bi, j: (bi, 0, 0))
    out = pl.BlockSpec((None, tr, HEAD_W), lambda bi, j: (bi, j, 0))
    return pl.pallas_call(
        functools.partial(_catcast_kernel, n_past=n_past, s_new=s, tr=tr),
        grid=(b, n_past + 1),
        in_specs=[past, past, new, new],
        out_specs=[out, out],
        out_shape=[jax.ShapeDtypeStruct((b, p + tr, HEAD_W), BF16)] * 2,
        compiler_params=_cparams(("parallel", "arbitrary")),
        name="catcast",
    )(cache_k.reshape(b, p * DIFF_HEADS, LANES), cache_v.reshape(b, p * DIFF_HEADS, LANES), new_k, new_v)


def _kvup_past_kernel(pc_ref, pp_ref, nc_ref, np_ref, wuk_ref, place_ref, wuv_ref, k_ref, v_ref, c_ref, r_ref,
                      *, n_past, s_new):
    j = pl.program_id(1)

    @pl.when(j < n_past)
    def _():
        c_ref[...] = pc_ref[...]
        r_ref[...] = pp_ref[...]

    @pl.when(j >= n_past)
    def _():
        c_ref[...] = jnp.zeros(c_ref.shape, F32)
        r_ref[...] = jnp.zeros(r_ref.shape, F32)
        c_ref[0:s_new, :] = nc_ref[...]
        r_ref[0:s_new, :] = np_ref[...]

    _kvup_kernel(c_ref, r_ref, wuk_ref, place_ref, wuv_ref, k_ref, v_ref)


def _kvup_past(cache_c, cache_r, new_c, new_r, wuk, place, wuv, tr):
    b, p = cache_c.shape[:2]
    s = new_c.shape[1]
    assert p % tr == 0 and s <= tr
    n_past = p // tr
    clamp = lambda bi, j: (bi, jnp.minimum(j, n_past - 1), 0)
    first = lambda bi, j: (bi, 0, 0)
    const = lambda bi, j: (0, 0)
    out = pl.BlockSpec((None, tr, HEAD_W), lambda bi, j: (bi, j, 0))
    return pl.pallas_call(
        functools.partial(_kvup_past_kernel, n_past=n_past, s_new=s),
        grid=(b, n_past + 1),
        in_specs=[pl.BlockSpec((None, tr, KV_LORA), clamp), pl.BlockSpec((None, tr, MLA_ROPE), clamp),
                  pl.BlockSpec((None, s, KV_LORA), first), pl.BlockSpec((None, s, MLA_ROPE), first),
                  pl.BlockSpec(wuk.shape, const), pl.BlockSpec(place.shape, const), pl.BlockSpec(wuv.shape, const)],
        out_specs=[out, out],
        out_shape=[jax.ShapeDtypeStruct((b, p + tr, HEAD_W), BF16)] * 2,
        scratch_shapes=[pltpu.VMEM((tr, KV_LORA), F32), pltpu.VMEM((tr, MLA_ROPE), F32)],
        compiler_params=_cparams(("parallel", "arbitrary")),
        name="kvup_past",
    )(cache_c, cache_r, new_c, new_r, wuk, place, wuv)


def _flash_kernel(lam_ref, q_ref, k_ref, v_ref, g_ref, o_ref, qs_ref, vx_ref, m_ref, acc_ref,
                  *, n_comp, hps, tq, tk, tkm, sub, sk, q_pos0, out_scale, tiles, static, aligned):
    rows = n_comp * tq
    blocks = [(h, r0) for h in range(hps) for r0 in range(0, rows, sub)]

    @pl.when(pl.program_id(2) == 0)
    def _():
        for h in range(hps):
            vx_ref[:, 2 * h * LANES:(2 * h + 1) * LANES] = v_ref[:, h * LANES:(h + 1) * LANES]
            vx_ref[:, (2 * h + 1) * LANES:(2 * h + 2) * LANES] = jnp.ones((sk, LANES), BF16)

    def absorb(s, start, h, r0):
        rs = slice(h * rows + r0, h * rows + r0 + sub)
        width = s.shape[1]
        vx = vx_ref[pl.ds(start, width), 2 * h * LANES:(2 * h + 2) * LANES]
        m_prev = m_ref[rs, :]
        m_new = jnp.maximum(m_prev, jnp.max(s, axis=-1, keepdims=True))
        alpha = jnp.exp2(m_prev - m_new)
        p = jnp.exp2(s - jnp.concatenate([m_new] * (width // LANES), axis=1))
        pv = jnp.dot(p.astype(BF16), vx, preferred_element_type=F32)
        acc_ref[rs, :] = jnp.concatenate([alpha, alpha], axis=1) * acc_ref[rs, :] + pv
        m_ref[rs, :] = m_new

    def scores(start, width, h, r0):
        rs = slice(h * rows + r0, h * rows + r0 + sub)
        k = k_ref[pl.ds(start, width), h * LANES:(h + 1) * LANES]
        return lax.dot_general(qs_ref[rs, :], k, (((1,), (1,)), ((), ())), preferred_element_type=F32)

    def masked_scores(start, width, h, r0, q0):
        r = (lax.broadcasted_iota(jnp.int32, (sub, width), 0) + r0) & (tq - 1)
        c = lax.broadcasted_iota(jnp.int32, (sub, width), 1)
        ok = ((start + c) >> CHUNK_SHIFT) <= ((q0 + r) >> CHUNK_SHIFT)
        return jnp.where(ok, scores(start, width, h, r0), NEG_BIG)

    def one_tile(qt, q0):
        for h in range(hps):
            q = q_ref[qt * tq:(qt + 1) * tq, h * LANES:(h + 1) * LANES]
            if n_comp == 2:
                lane = lax.broadcasted_iota(jnp.int32, (tq, LANES), 1)
                zero = jnp.zeros_like(q)
                qs_ref[h * rows:h * rows + tq, :] = jnp.where(lane < DIFF_QK, q, zero)
                qs_ref[h * rows + tq:(h + 1) * rows, :] = jnp.where(lane >= DIFF_QK, q, zero)
            else:
                qs_ref[h * rows:(h + 1) * rows, :] = q
        m_ref[...] = jnp.full(m_ref.shape, NEG_BIG, F32)
        acc_ref[...] = jnp.zeros(acc_ref.shape, F32)

        if aligned:
            for j in range(q0 // tk):
                for h, r0 in blocks:
                    absorb(scores(j * tk, tk, h, r0), j * tk, h, r0)
            qc = lax.broadcasted_iota(jnp.int32, (sub, sub), 0) >> CHUNK_SHIFT
            kc = lax.broadcasted_iota(jnp.int32, (sub, sub), 1) >> CHUNK_SHIFT
            for h, r0 in blocks:
                a = r0 % tq
                s = scores(q0, a + sub, h, r0)
                last = jnp.where(kc <= qc, s[:, a:a + sub], NEG_BIG)
                absorb(jnp.concatenate([s[:, 0:a], last], axis=1) if a else last, q0, h, r0)
        elif static:
            lo_vis = min((q0 // CHUNK + 1) * CHUNK, sk)
            hi_vis = min(((q0 + tq - 1) // CHUNK + 1) * CHUNK, sk)
            for j in range(lo_vis // tk):
                for h, r0 in blocks:
                    absorb(scores(j * tk, tk, h, r0), j * tk, h, r0)
            start = lo_vis // tk * tk
            while start < hi_vis:
                width = min(tkm, -(-(hi_vis - start) // LANES) * LANES)
                for h, r0 in blocks:
                    absorb(masked_scores(start, width, h, r0, q0), start, h, r0)
                start += width
        else:
            lo_vis = jnp.minimum(((q0 >> CHUNK_SHIFT) + 1) << CHUNK_SHIFT, sk)
            hi_vis = jnp.minimum((((q0 + tq - 1) >> CHUNK_SHIFT) + 1) << CHUNK_SHIFT, sk)
            n_full = lo_vis // tk

            def full_body(j, carry):
                start = pl.multiple_of(j * tk, tk)
                for h, r0 in blocks:
                    absorb(scores(start, tk, h, r0), start, h, r0)
                return carry

            def masked_body(j, carry):
                start = pl.multiple_of(j * tkm, tkm)
                for h, r0 in blocks:
                    absorb(masked_scores(start, tkm, h, r0, q0), start, h, r0)
                return carry

            lax.fori_loop(0, n_full, full_body, 0)
            lax.fori_loop(n_full * (tk // tkm), (hi_vis + tkm - 1) // tkm, masked_body, 0)

        for h in range(hps):
            hr = slice(h * rows, (h + 1) * rows)
            o = acc_ref[hr, 0:LANES] / acc_ref[hr, LANES:2 * LANES]
            if n_comp == 2:
                o = o[0:tq, :] - lam_ref[0, 0] * o[tq:rows, :]
                o = _rms(o, g_ref[...]) * out_scale
            o_ref[qt * tq:(qt + 1) * tq, h * LANES:(h + 1) * LANES] = o.astype(BF16)

    if static:
        for qt in range(tiles):
            one_tile(qt, q_pos0 + qt * tq)
    else:
        one_tile(0, q_pos0 + pl.program_id(2) * tq)


def _flash(q, k, v, lam, g, *, n_comp, hps, tq, tk, tkm, q_pos0, out_scale):
    b, sq, _ = q.shape
    sk = k.shape[1]
    rows = n_comp * tq
    sub = min(rows, FLASH_SUB)
    assert sq % tq == 0 and sk % tk == 0 and tk % tkm == 0 and tq & (tq - 1) == 0 and rows % sub == 0
    assert DIFF_HEADS % hps == 0
    aligned = q_pos0 % tq == 0 and tq == tk and q_pos0 + sq <= sk and sub % CHUNK == 0 and sub % LANES == 0
    static = aligned or sq == tq
    tiles = sq // tq if static else 1
    hw = hps * LANES
    return pl.pallas_call(
        functools.partial(_flash_kernel, n_comp=n_comp, hps=hps, tq=tq, tk=tk, tkm=tkm, sub=sub, sk=sk,
                          q_pos0=q_pos0, out_scale=out_scale, tiles=tiles, static=static, aligned=aligned),
        grid=(b, DIFF_HEADS // hps, sq // (tq * tiles)),
        in_specs=[pl.BlockSpec(memory_space=pltpu.SMEM),
                  pl.BlockSpec((None, tq * tiles, hw), lambda bi, h, i: (bi, i, h)),
                  pl.BlockSpec((None, sk, hw), lambda bi, h, i: (bi, 0, h)),
                  pl.BlockSpec((None, sk, hw), lambda bi, h, i: (bi, 0, h)),
                  pl.BlockSpec((1, LANES), lambda bi, h, i: (0, 0))],
        out_specs=pl.BlockSpec((None, tq * tiles, hw), lambda bi, h, i: (bi, i, h)),
        out_shape=jax.ShapeDtypeStruct((b, sq, HEAD_W), BF16),
        scratch_shapes=[pltpu.VMEM((hps * rows, LANES), BF16), pltpu.VMEM((sk, 2 * hw), BF16),
                        pltpu.VMEM((hps * rows, LANES), F32), pltpu.VMEM((hps * rows, 2 * LANES), F32)],
        compiler_params=_cparams(("parallel", "parallel", "arbitrary")),
        name="flash_diff" if n_comp == 2 else "flash_mla",
    )(lam, q, k, v, g)


def _sublane_max(x):
    return jnp.max(x, axis=0, keepdims=True)


def _sublane_min(x):
    return jnp.min(x, axis=0, keepdims=True)


def _route_t(scores, biased):
    tm = scores[0].shape[1]
    sub = lax.broadcasted_iota(jnp.int32, (GROUP_SIZE, tm), 0)
    neg_inf = jnp.float32(-jnp.inf)
    gs = jnp.zeros((N_GROUPS, tm), F32)
    for g in range(N_GROUPS):
        bg = biased[g]
        m1 = _sublane_max(bg)
        i1 = _sublane_min(jnp.where(bg == m1, sub, GROUP_SIZE))
        m2 = _sublane_max(jnp.where(sub == i1, neg_inf, bg))
        gs = jnp.where(sub == g, m1 + m2, gs)
    keep = jnp.zeros((N_GROUPS, tm), jnp.bool_)
    cur = gs
    for _ in range(TOPK_GROUPS):
        mx = _sublane_max(cur)
        fi = _sublane_min(jnp.where(cur == mx, sub, N_GROUPS))
        hit = sub == fi
        keep = jnp.logical_or(keep, hit)
        cur = jnp.where(hit, neg_inf, cur)
    keep_f = jnp.where(keep, 1.0, 0.0)
    cand = []
    for g in range(N_GROUPS):
        kg = _sublane_max(jnp.where(sub == g, keep_f, 0.0)) > 0.5
        cand.append(jnp.where(kg, biased[g], neg_inf))
    chosen = [jnp.zeros((GROUP_SIZE, tm), jnp.bool_) for _ in range(N_GROUPS)]
    picks = []
    for _ in range(TOP_K):
        mx = cand[0]
        for g in range(1, N_GROUPS):
            mx = jnp.maximum(mx, cand[g])
        mx = _sublane_max(mx)
        fi = jnp.where(cand[0] == mx, sub, N_EXPERTS)
        for g in range(1, N_GROUPS):
            fi = jnp.minimum(fi, jnp.where(cand[g] == mx, sub + g * GROUP_SIZE, N_EXPERTS))
        fi = _sublane_min(fi)
        picks.append(fi)
        for g in range(N_GROUPS):
            hit = (sub + g * GROUP_SIZE) == fi
            chosen[g] = jnp.logical_or(chosen[g], hit)
            cand[g] = jnp.where(hit, neg_inf, cand[g])
    w = [jnp.where(chosen[g], scores[g], 0.0) for g in range(N_GROUPS)]
    tot = w[0]
    for g in range(1, N_GROUPS):
        tot = tot + w[g]
    tot = jnp.sum(tot, axis=0, keepdims=True)
    return [wg / tot * ROUTED_SCALE for wg in w], chosen, picks


def _mix_kernel(od_ref, om_ref, x_ref, wo_ref, g_ref, b_ref, wr_ref, wrl_ref, br_ref, before_ref, below_ref, eye_ref,
                x1_ref, x1b_ref, sk_ref, cnt_ref, *, alpha):
    mix = jnp.dot(od_ref[...], wo_ref[0:HEAD_W, :], preferred_element_type=F32)
    mix = mix + jnp.dot(om_ref[...], wo_ref[HEAD_W:2 * HEAD_W, :], preferred_element_type=F32)
    x1 = _layer_norm(alpha * x_ref[...] + mix, g_ref[...], b_ref[...])
    x_hi = x1.astype(BF16)
    x1_ref[...] = x1
    x1b_ref[...] = x_hi
    x_lo = (x1 - x_hi.astype(F32)).astype(BF16)
    nt = (((1,), (1,)), ((), ()))
    logits = (lax.dot_general(wr_ref[...], x_hi, nt, preferred_element_type=F32)
              + lax.dot_general(wr_ref[...], x_lo, nt, preferred_element_type=F32)
              + lax.dot_general(wrl_ref[...], x_hi, nt, preferred_element_type=F32))
    sc = 1.0 / (1.0 + jnp.exp(-logits))
    bi = sc + br_ref[...]
    scores = [sc[g * GROUP_SIZE:(g + 1) * GROUP_SIZE, :] for g in range(N_GROUPS)]
    biased = [bi[g * GROUP_SIZE:(g + 1) * GROUP_SIZE, :] for g in range(N_GROUPS)]
    gates, chosen, picks = _route_t(scores, biased)
    tm = x1.shape[0]
    ch = jnp.concatenate([jnp.where(c, 1.0, 0.0) for c in chosen], axis=0)
    gate_all = jnp.concatenate(gates, axis=0)
    sub = lax.broadcasted_iota(jnp.int32, (N_EXPERTS, MOE_BLOCK), 0)
    for blk in range(tm // MOE_BLOCK):
        ls = slice(blk * MOE_BLOCK, (blk + 1) * MOE_BLOCK)
        chb = ch[:, ls]
        rank = jnp.dot(chb.astype(BF16), before_ref[...], preferred_element_type=F32)
        cnt = jnp.sum(chb, axis=1, keepdims=True)
        seg = jnp.floor((cnt + (SEG_ALIGN - 1.0)) * (1.0 / SEG_ALIGN)) * SEG_ALIGN
        seg_b = jnp.broadcast_to(seg, (N_EXPERTS, LANES))
        off = jnp.dot(below_ref[...], seg_b, precision=lax.Precision.HIGHEST, preferred_element_type=F32)
        slot = jnp.concatenate([off] * (MOE_BLOCK // LANES), axis=1) + rank
        cnt_ref[blk] = jnp.sum(seg_b * eye_ref[...], axis=0, keepdims=True)
        gb = gate_all[:, ls]
        for k in range(TOP_K):
            hit = sub == picks[k][:, ls]
            sk_ref[k:k + 1, ls] = jnp.sum(jnp.where(hit, slot, 0.0), axis=0, keepdims=True)
            sk_ref[TOP_K + k:TOP_K + k + 1, ls] = jnp.sum(jnp.where(hit, gb, 0.0), axis=0, keepdims=True)


def _mix(od, om, x, wo, g, b, wr_hi, wr_lo, br, tm, alpha):
    t = x.shape[0]
    d = x.shape[1]
    nblk = tm // MOE_BLOCK
    row = lambda i: (i, 0)
    const = lambda i: (0, 0)
    idx = jnp.arange(MOE_BLOCK)
    before = (idx[:, None] < idx[None, :]).astype(BF16)
    ide = jnp.arange(N_EXPERTS)
    below = (ide[None, :] < ide[:, None]).astype(F32)
    eye = (ide[:, None] == jnp.arange(LANES)[None, :]).astype(F32)
    return pl.pallas_call(
        functools.partial(_mix_kernel, alpha=alpha),
        grid=(t // tm,),
        in_specs=[pl.BlockSpec((tm, HEAD_W), row), pl.BlockSpec((tm, HEAD_W), row), pl.BlockSpec((tm, d), row),
                  pl.BlockSpec(wo.shape, const), pl.BlockSpec(g.shape, const), pl.BlockSpec(b.shape, const),
                  pl.BlockSpec(wr_hi.shape, const), pl.BlockSpec(wr_lo.shape, const), pl.BlockSpec(br.shape, const),
                  pl.BlockSpec(before.shape, const), pl.BlockSpec(below.shape, const), pl.BlockSpec(eye.shape, const)],
        out_specs=[pl.BlockSpec((tm, d), row), pl.BlockSpec((tm, d), row),
                   pl.BlockSpec((2 * TOP_K, tm), lambda i: (0, i)),
                   pl.BlockSpec((nblk, 1, LANES), lambda i: (i, 0, 0))],
        out_shape=[jax.ShapeDtypeStruct((t, d), F32), jax.ShapeDtypeStruct((t, d), BF16),
                   jax.ShapeDtypeStruct((2 * TOP_K, t), F32),
                   jax.ShapeDtypeStruct((t // MOE_BLOCK, 1, LANES), F32)],
        compiler_params=_cparams(("parallel",)),
        name="mix",
    )(od, om, x, wo, g, b, wr_hi, wr_lo, br, before, below, eye)


def _moe_plan(cnt, n_tiles, tile, trash_row):
    seg = cnt.astype(jnp.int32)
    nb = seg.shape[0]
    before_blocks = jnp.cumsum(seg, axis=0) - seg
    length = jnp.sum(seg, axis=0)
    padded = -(-length // tile) * tile
    ends = jnp.cumsum(padded)
    start = ends - padded
    dst = start[None, :] + before_blocks
    off = (jnp.cumsum(seg, axis=1) - seg) // SEG_ALIGN
    tot = jnp.sum(seg, axis=1) // SEG_ALIGN
    chunk = jnp.arange(BLOCK_CHUNKS, dtype=jnp.int32)
    base = dst - SEG_ALIGN * off
    step = jnp.concatenate([base[:, :1], base[:, 1:] - base[:, :-1]], axis=1)
    started = (off[:, None, :] <= chunk[None, :, None]).astype(jnp.int32)
    row = jnp.sum(started * step[:, None, :], axis=2) + SEG_ALIGN * chunk[None, :]
    valid = chunk[None, :] < tot[:, None]
    trash = trash_row + ((jnp.arange(nb, dtype=jnp.int32) % 3) * BLOCK_ROWS)[:, None] + SEG_ALIGN * chunk[None, :]
    first = trash_row + 2 * BLOCK_ROWS + SEG_ALIGN * chunk[None, :]
    put = jnp.concatenate([first, jnp.where(valid, row, trash)], axis=0).reshape(-1)
    get = jnp.concatenate([jnp.where(valid, row, 0), jnp.zeros((2, BLOCK_CHUNKS), jnp.int32)], axis=0).reshape(-1)
    n_used = ends[-1] // tile
    tile_start = jnp.arange(n_tiles, dtype=jnp.int32) * tile
    tile_e = jnp.sum((ends[None, :] <= tile_start[:, None]).astype(jnp.int32), axis=1)
    last_e = tile_e[jnp.maximum(n_used - 1, 0)]
    tile_e = jnp.where(jnp.arange(n_tiles) < n_used, tile_e, last_e)
    tail_start = start + length
    tail_nch = (padded - length) // SEG_ALIGN
    return dict(put=put, get=get, n_used=n_used.reshape(1), tile_e=tile_e, tail_start=tail_start,
                tail_nch=tail_nch, tail_tot=jnp.sum(tail_nch).reshape(1))


def _drain(copy, n):
    def body(i, c):
        copy.wait()
        return c
    lax.fori_loop(0, n, body, 0)


def _order_rows(sk, j0, vals):
    rows = lax.broadcasted_iota(jnp.int32, (ORD_SUB, MOE_BLOCK), 0).astype(F32).astype(BF16)
    out = jnp.zeros((ORD_SUB, MOE_BLOCK), BF16)
    for k in range(TOP_K):
        rel = (sk[k:k + 1, :] - j0).astype(BF16)
        out = jnp.where(rows == rel, vals[k], out)
    return out


def _dispatch_kernel(put_ref, tstart_ref, tnch_ref, ttot_ref, x_ref, sk_ref, xs_ref, buf_ref, zero_ref, sem, tail_sem):
    b = pl.program_id(0)
    last = pl.num_programs(0) - 1
    slot = lax.rem(b, 3)
    send = lax.rem(b + 2, 3)

    def whole(s):
        return pltpu.make_async_copy(buf_ref.at[s], xs_ref.at[pl.ds(0, BLOCK_ROWS), :], sem.at[s])

    @pl.when(b == 0)
    def _():
        buf_ref[...] = jnp.zeros(buf_ref.shape, BF16)

    @pl.when(b >= 2)
    def _():
        whole(slot).wait()

    x = x_ref[...]
    sk = sk_ref[0:TOP_K, :]
    ones = [jnp.ones((1, MOE_BLOCK), BF16)] * TOP_K
    base = b * BLOCK_CHUNKS
    for sub in range(BLOCK_ROWS // DISP_SUB):
        j0 = sub * DISP_SUB
        sel = jnp.concatenate([_order_rows(sk, float(j0 + i * ORD_SUB), ones) for i in range(DISP_SUB // ORD_SUB)],
                              axis=0)
        buf_ref[slot, j0:j0 + DISP_SUB, :] = jnp.dot(sel, x, preferred_element_type=F32).astype(BF16)
        for c in range(sub * (DISP_SUB // SEG_ALIGN), (sub + 1) * (DISP_SUB // SEG_ALIGN)):
            pltpu.make_async_copy(
                buf_ref.at[send, c * SEG_ALIGN:(c + 1) * SEG_ALIGN, :],
                xs_ref.at[pl.ds(pl.multiple_of(put_ref[base + c], SEG_ALIGN), SEG_ALIGN), :], sem.at[send]).start()

    @pl.when(b == last)
    def _():
        whole(send).wait()

        @pl.when(b >= 1)
        def _():
            whole(lax.rem(b + 1, 3)).wait()

        zero_ref[...] = jnp.zeros(zero_ref.shape, BF16)

        def tail_copy(dst_row):
            return pltpu.make_async_copy(zero_ref, xs_ref.at[pl.ds(pl.multiple_of(dst_row, SEG_ALIGN), SEG_ALIGN), :],
                                         tail_sem.at[0])

        def tail_expert(e, carry):
            def tail_chunk(c, carry2):
                tail_copy(tstart_ref[e] + c * SEG_ALIGN).start()
                return carry2
            lax.fori_loop(0, tnch_ref[e], tail_chunk, 0)
            return carry

        lax.fori_loop(0, N_EXPERTS, tail_expert, 0)
        _drain(tail_copy(0), ttot_ref[0])


def _dispatch(plan, x1b, sk, n_rows):
    t, d = x1b.shape
    nb = t // MOE_BLOCK
    grid_spec = pltpu.PrefetchScalarGridSpec(
        num_scalar_prefetch=4,
        grid=(nb + 1,),
        in_specs=[pl.BlockSpec((MOE_BLOCK, d), lambda b, *_: (jnp.minimum(b, nb - 1), 0)),
                  pl.BlockSpec((2 * TOP_K, MOE_BLOCK), lambda b, *_: (0, jnp.minimum(b, nb - 1)))],
        out_specs=pl.BlockSpec(memory_space=pl.ANY),
        scratch_shapes=[pltpu.VMEM((3, BLOCK_ROWS, d), BF16), pltpu.VMEM((SEG_ALIGN, d), BF16),
                        pltpu.SemaphoreType.DMA((3,)), pltpu.SemaphoreType.DMA((1,))],
    )
    return pl.pallas_call(
        _dispatch_kernel,
        grid_spec=grid_spec,
        out_shape=jax.ShapeDtypeStruct((n_rows, d), BF16),
        compiler_params=_cparams(("arbitrary",)),
        name="dispatch",
    )(plan["put"], plan["tail_start"], plan["tail_nch"], plan["tail_tot"], x1b, sk)


def _ffn_kernel(te_ref, nu_ref, xs_ref, wg_ref, wu_ref, wd_ref, y_ref):
    @pl.when(pl.program_id(0) < nu_ref[0])
    def _():
        x = xs_ref[...]
        hg = jnp.dot(x, wg_ref[...], preferred_element_type=F32)
        hu = jnp.dot(x, wu_ref[...], preferred_element_type=F32)
        h = hg * (1.0 / (1.0 + jnp.exp(-hg))) * hu
        y_ref[...] = jnp.dot(h.astype(BF16), wd_ref[...], preferred_element_type=F32).astype(BF16)


def _ffn(plan, xs, wg, wu, wd, n_tiles, tile):
    d = xs.shape[1]
    used = lambda i, te, nu: (jnp.minimum(i, nu[0] - 1), 0)
    wsel = lambda i, te, nu: (te[i], 0, 0)
    grid_spec = pltpu.PrefetchScalarGridSpec(
        num_scalar_prefetch=2,
        grid=(n_tiles,),
        in_specs=[pl.BlockSpec((tile, d), used),
                  pl.BlockSpec((None, d, EXPERT_DIM), wsel), pl.BlockSpec((None, d, EXPERT_DIM), wsel),
                  pl.BlockSpec((None, EXPERT_DIM, d), wsel)],
        out_specs=pl.BlockSpec((tile, d), used),
    )
    return pl.pallas_call(
        _ffn_kernel,
        grid_spec=grid_spec,
        out_shape=jax.ShapeDtypeStruct((n_tiles * tile, d), BF16),
        compiler_params=_cparams(("arbitrary",)),
        name="ffn",
    )(plan["tile_e"], plan["n_used"], xs, wg, wu, wd)


def _combine_kernel(get_ref, y_ref, sk_ref, x1_ref, x1b_ref, wsg_ref, wsu_ref, wsd_ref,
                    g_ref, b_ref, o_ref, buf_ref, sem, *, alpha):
    b = pl.program_id(0)
    slot = lax.rem(b, 3)
    nxt = lax.rem(b + 2, 3)

    def whole(s):
        return pltpu.make_async_copy(y_ref.at[pl.ds(0, BLOCK_ROWS), :], buf_ref.at[s], sem.at[s])

    @pl.when(b == 0)
    def _():
        def first(c, carry):
            s = c // BLOCK_CHUNKS
            pltpu.make_async_copy(
                y_ref.at[pl.ds(pl.multiple_of(get_ref[c], SEG_ALIGN), SEG_ALIGN), :],
                buf_ref.at[s, pl.ds(pl.multiple_of((c - s * BLOCK_CHUNKS) * SEG_ALIGN, SEG_ALIGN), SEG_ALIGN), :],
                sem.at[s]).start()
            return carry
        lax.fori_loop(0, 2 * BLOCK_CHUNKS, first, 0)

    whole(slot).wait()

    sk = sk_ref[0:TOP_K, :]
    gates = [sk_ref[TOP_K + k:TOP_K + k + 1, :].astype(BF16) for k in range(TOP_K)]
    base = (b + 2) * BLOCK_CHUNKS
    acc = None
    for sub in range(BLOCK_ROWS // DISP_SUB):
        j0 = sub * DISP_SUB
        w = jnp.concatenate([_order_rows(sk, float(j0 + i * ORD_SUB), gates) for i in range(DISP_SUB // ORD_SUB)],
                            axis=0)
        part = lax.dot_general(w, buf_ref[slot, j0:j0 + DISP_SUB, :], (((0,), (0,)), ((), ())),
                               preferred_element_type=F32)
        acc = part if acc is None else acc + part
        for c in range(sub * (DISP_SUB // SEG_ALIGN), (sub + 1) * (DISP_SUB // SEG_ALIGN)):
            pltpu.async_copy(
                y_ref.at[pl.ds(pl.multiple_of(get_ref[base + c], SEG_ALIGN), SEG_ALIGN), :],
                buf_ref.at[nxt, c * SEG_ALIGN:(c + 1) * SEG_ALIGN, :], sem.at[nxt], priority=c % 2)

    xb = x1b_ref[...]
    hg = jnp.dot(xb, wsg_ref[...], preferred_element_type=F32)
    hu = jnp.dot(xb, wsu_ref[...], preferred_element_type=F32)
    h = hg * (1.0 / (1.0 + jnp.exp(-hg))) * hu
    acc = acc + jnp.dot(h.astype(BF16), wsd_ref[...], preferred_element_type=F32)
    o_ref[...] = _layer_norm(alpha * x1_ref[...] + acc, g_ref[...], b_ref[...])

    @pl.when(b == pl.num_programs(0) - 1)
    def _():
        whole(lax.rem(b + 1, 3)).wait()
        whole(nxt).wait()


def _combine(plan, y, sk, x1, x1b, wsg, wsu, wsd, g, b, alpha):
    t, d = x1.shape
    nb = t // MOE_BLOCK
    row = lambda i, *_: (i, 0)
    const = lambda i, *_: (0, 0)
    grid_spec = pltpu.PrefetchScalarGridSpec(
        num_scalar_prefetch=1,
        grid=(nb,),
        in_specs=[pl.BlockSpec(memory_space=pl.ANY), pl.BlockSpec((2 * TOP_K, MOE_BLOCK), lambda i, *_: (0, i)),
                  pl.BlockSpec((MOE_BLOCK, d), row), pl.BlockSpec((MOE_BLOCK, d), row),
                  pl.BlockSpec(wsg.shape, const), pl.BlockSpec(wsu.shape, const), pl.BlockSpec(wsd.shape, const),
                  pl.BlockSpec(g.shape, const), pl.BlockSpec(b.shape, const)],
        out_specs=pl.BlockSpec((MOE_BLOCK, d), row),
        scratch_shapes=[pltpu.VMEM((3, BLOCK_ROWS, d), BF16), pltpu.SemaphoreType.DMA((3,))],
    )
    return pl.pallas_call(
        functools.partial(_combine_kernel, alpha=alpha),
        grid_spec=grid_spec,
        out_shape=jax.ShapeDtypeStruct((t, d), F32),
        compiler_params=_cparams(("arbitrary",)),
        name="combine",
    )(plan["get"], y, sk, x1, x1b, wsg, wsu, wsd, g, b)


def _moe(x1, x1b, sk, cnt, wg, wu, wd, wsg, wsu, wsd, g, b, alpha):
    t = x1.shape[0]
    nb = t // MOE_BLOCK
    tile = FFN_TILE if (TOP_K * t) // N_EXPERTS >= FFN_TILE else FFN_TILE_SMALL
    n_tiles = -(-(nb * BLOCK_ROWS) // tile) + N_EXPERTS
    n_rows = n_tiles * tile + 3 * BLOCK_ROWS
    plan = _moe_plan(cnt[:, 0, :N_EXPERTS], n_tiles, tile, n_tiles * tile)
    xs = _dispatch(plan, x1b, sk, n_rows)
    y = _ffn(plan, xs, wg, wu, wd, n_tiles, tile)
    return _combine(plan, y, sk, x1, x1b, wsg, wsu, wsd, g, b, alpha)


def _rope_cs(pos, dim):
    inv = ROPE_THETA ** (-jnp.arange(0, dim, 2, dtype=F32) / dim)
    ang = pos.astype(F32)[:, None] * inv[None, :]
    return jnp.cos(ang), jnp.sin(ang)


def _tables(pos, reps):
    n = pos.shape[0]
    c32, s32 = _rope_cs(pos, DIFF_QK)
    c16, s16 = _rope_cs(pos, MLA_ROPE)
    one = lambda w: jnp.ones((n, w), F32)
    zero = lambda w: jnp.zeros((n, w), F32)
    cd = jnp.concatenate([c32] * 4, axis=1)
    sd = jnp.concatenate([-s32, s32] * 2, axis=1)
    cq = jnp.concatenate([one(MLA_NOPE), c16, c16, one(32)], axis=1)
    sq = jnp.concatenate([zero(MLA_NOPE), -s16, s16, zero(32)], axis=1)
    ck = jnp.concatenate([c16, c16, zero(96)], axis=1)
    sk = jnp.concatenate([-s16, s16, zero(96)], axis=1)
    return tuple(jnp.tile(a, (reps, 1)) for a in (cd, sd, cq, sq, ck, sk))


def kernel(x_prompt, x_sample, cache_diff_k, cache_diff_v, cache_mla_ckv, cache_mla_kpe, w_in, diff_lambda, diff_subln_g, mla_q_norm_g, mla_w_uq, mla_kv_norm_g, mla_w_ukv, w_out, ln1_g, ln1_b, w_router, b_router, w_exp_gate, w_exp_up, w_exp_down, w_sh_gate, w_sh_up, w_sh_down, ln2_g, ln2_b):
    depth = w_in.shape[0]
    assert depth == 1
    d_model = x_prompt.shape[-1]
    alpha = (2.0 * depth) ** 0.25
    past_len = cache_diff_k.shape[2]
    layer = 0
    lambda_init = 0.8 - 0.6 * math.exp(-0.3 * layer)

    w_in_b = jnp.pad(w_in[layer], ((0, 0), (0, IN_PAD - IN_WIDTH))).astype(BF16)
    wuq = jnp.pad(mla_w_uq[layer], ((0, 0), (0, 0), (0, LANES - MLA_NOPE - MLA_ROPE)))
    wuq = wuq.reshape(Q_LORA, HEAD_W).astype(BF16)
    wukv = mla_w_ukv[layer]
    wuk = jnp.pad(wukv[:, :, :MLA_NOPE], ((0, 0), (0, 0), (0, LANES - MLA_NOPE))).reshape(KV_LORA, HEAD_W).astype(BF16)
    wuv = wukv[:, :, MLA_NOPE:].reshape(KV_LORA, HEAD_W).astype(BF16)
    place = jnp.pad(jnp.eye(MLA_ROPE, dtype=F32), ((0, 0), (MLA_NOPE, LANES - MLA_NOPE - MLA_ROPE)))
    place = jnp.tile(place, (1, MLA_HEADS)).astype(BF16)
    gq = mla_q_norm_g[layer].reshape(1, Q_LORA)
    gkv = mla_kv_norm_g[layer].reshape(1, KV_LORA)
    gsub = diff_subln_g[layer].reshape(1, LANES)
    wo = w_out[layer].astype(BF16)
    g1, b1 = ln1_g[layer].reshape(1, d_model), ln1_b[layer].reshape(1, d_model)
    g2, b2 = ln2_g[layer].reshape(1, d_model), ln2_b[layer].reshape(1, d_model)
    wr_t = w_router[layer].T
    wr_hi = wr_t.astype(BF16)
    wr_lo = (wr_t - wr_hi.astype(F32)).astype(BF16)
    br = b_router[layer].reshape(N_EXPERTS, 1)
    wg, wu, wd = (w[layer].astype(BF16) for w in (w_exp_gate, w_exp_up, w_exp_down))
    wsg, wsu, wsd = (w[layer].astype(BF16) for w in (w_sh_gate, w_sh_up, w_sh_down))
    lp = diff_lambda[layer].astype(F32)
    lam = jnp.exp(jnp.sum(lp[0] * lp[1])) - jnp.exp(jnp.sum(lp[2] * lp[3])) + lambda_init
    lam = lam.reshape(1, 1)

    def group(x, pos, past, tm, tq, tk, tkm, hps):
        b, s, _ = x.shape
        t = b * s
        tm = min(tm, t)
        xf = x.reshape(t, d_model)
        reps = max(1, tm // s)
        tables = _tables(pos, reps)
        n_pat = (s * reps) // tm
        qd, kd32, kd16, vd32, vd16, ckv, kpe, qm = _proj(xf, w_in_b, wuq, gq, gkv, tables, tm, n_pat)
        if past is None:
            k_d, v_d = kd16.reshape(b, s, HEAD_W), vd16.reshape(b, s, HEAD_W)
            k_m, v_m = _kvup(ckv, kpe, wuk, place, wuv, min(1024, t))
            k_m, v_m = k_m.reshape(b, s, HEAD_W), v_m.reshape(b, s, HEAD_W)
            q_pos0 = 0
        else:
            pk, pv, pc, pp = past
            k_d, v_d = _catcast(pk, pv, kd16.reshape(b, s, HEAD_W), vd16.reshape(b, s, HEAD_W), tk)
            k_m, v_m = _kvup_past(pc, pp, ckv.reshape(b, s, KV_LORA), kpe.reshape(b, s, MLA_ROPE),
                                  wuk, place, wuv, tk)
            q_pos0 = past_len
        o_d = _flash(qd.reshape(b, s, HEAD_W), k_d, v_d, lam, gsub, n_comp=2, hps=hps, tq=tq, tk=tk, tkm=tkm,
                     q_pos0=q_pos0, out_scale=1.0 - lambda_init)
        o_m = _flash(qm.reshape(b, s, HEAD_W), k_m, v_m, lam, gsub,
                     n_comp=1, hps=hps, tq=tq, tk=tk, tkm=tkm, q_pos0=q_pos0, out_scale=1.0)
        x1, x1b, slots, cnt = _mix(o_d.reshape(t, HEAD_W), o_m.reshape(t, HEAD_W), xf, wo, g1, b1, wr_hi, wr_lo, br,
                                   tm, alpha)
        y = _moe(x1, x1b, slots, cnt, wg, wu, wd, wsg, wsu, wsd, g2, b2, alpha)
        rows = (kd32.reshape(1, b, s, DIFF_HEADS, LANES), vd32.reshape(1, b, s, DIFF_HEADS, LANES),
                ckv.reshape(1, b, s, KV_LORA), kpe.reshape(1, b, s, MLA_ROPE))
        return y.reshape(b, s, d_model), rows

    s_p = x_prompt.shape[1]
    s_s = x_sample.shape[1]
    pos_p = jnp.arange(s_p, dtype=jnp.int32)
    pos_s = past_len + jnp.arange(s_s, dtype=jnp.int32)
    y_p, r_p = group(x_prompt, pos_p, None, 512, 256, 256, 256, DIFF_HEADS)
    past = (cache_diff_k[layer], cache_diff_v[layer], cache_mla_ckv[layer], cache_mla_kpe[layer])
    y_s, r_s = group(x_sample, pos_s, past, 512, s_s, 512, 256, DIFF_HEADS)
    return (y_p, y_s) + r_p + r_s
```

```python
import functools
import math

import jax
import jax.numpy as jnp
from jax import lax
from jax.experimental import pallas as pl
from jax.experimental.pallas import tpu as pltpu

F32 = jnp.float32
BF16 = jnp.bfloat16

LANES = 128
VMEM_LIMIT = 52 * 1024 * 1024

CHUNK = 64
CHUNK_SHIFT = 6
ROPE_THETA = 10000.0
LN_EPS = 1e-5
RMS_EPS = 1e-6
LOG2E = 1.4426950408889634
NEG_BIG = -1e30
FLASH_SUB = 256

DIFF_HEADS = 4
DIFF_QK = 64
MLA_HEADS = 4
MLA_NOPE = 64
MLA_ROPE = 32
MLA_V = 128
Q_LORA = 384
KV_LORA = 256
N_EXPERTS = 64
N_GROUPS = 8
GROUP_SIZE = N_EXPERTS // N_GROUPS
TOPK_GROUPS = 4
TOP_K = 8
ROUTED_SCALE = 2.5
EXPERT_DIM = 256

MOE_BLOCK = 256
SEG_ALIGN = 16
ORD_SUB = 256
DISP_SUB = 1024
BLOCK_ROWS = -(-(TOP_K * MOE_BLOCK + N_EXPERTS * (SEG_ALIGN - 1)) // DISP_SUB) * DISP_SUB
BLOCK_CHUNKS = BLOCK_ROWS // SEG_ALIGN
FFN_TILE = 1024
FFN_TILE_SMALL = 256

DQ_W = DIFF_HEADS * 2 * DIFF_QK
HEAD_W = DIFF_HEADS * LANES
IN_WIDTH = 3 * DQ_W + Q_LORA + KV_LORA + MLA_ROPE
IN_PAD = 2304
OFF_DK, OFF_DV, OFF_CQ, OFF_CKV, OFF_KPE = 512, 1024, 1536, 1920, 2176


def _cparams(sem):
    return pltpu.CompilerParams(dimension_semantics=sem, vmem_limit_bytes=VMEM_LIMIT)


def _rms(x, g):
    return x * lax.rsqrt(jnp.mean(x * x, axis=-1, keepdims=True) + RMS_EPS) * g


def _layer_norm(x, g, b):
    mu = jnp.mean(x, axis=-1, keepdims=True)
    xc = x - mu
    var = jnp.mean(xc * xc, axis=-1, keepdims=True)
    return xc * lax.rsqrt(var + LN_EPS) * g + b


def _proj_kernel(x_ref, w_ref, wuq_ref, gq_ref, gkv_ref, cd_ref, sd_ref, cq_ref, sq_ref, ck_ref, sk_ref,
                 qd_ref, kd32_ref, kd16_ref, vd32_ref, vd16_ref, ckv_ref, kpe_ref, qm_ref,
                 *, scale_d, scale_m):
    tm = x_ref.shape[0]
    x = x_ref[...].astype(BF16)
    proj = jnp.dot(x, w_ref[...], preferred_element_type=F32)
    lane = lax.broadcasted_iota(jnp.int32, (tm, LANES), 1)

    first_d = (lane & 63) < 32
    cd = cd_ref[...]
    sd = sd_ref[...]

    def rope_d(blk):
        rot = jnp.where(first_d, pltpu.roll(blk, LANES - 32, 1), pltpu.roll(blk, 32, 1))
        return blk * cd + rot * sd

    for j in range(DIFF_HEADS):
        sl = slice(j * LANES, (j + 1) * LANES)
        qd_ref[:, sl] = (rope_d(proj[:, sl]) * scale_d).astype(BF16)
        kr = rope_d(proj[:, OFF_DK + j * LANES:OFF_DK + (j + 1) * LANES])
        kd32_ref[pl.ds(j, tm, stride=DIFF_HEADS), :] = kr
        kd16_ref[:, sl] = kr.astype(BF16)
        dv = proj[:, OFF_DV + j * LANES:OFF_DV + (j + 1) * LANES]
        vd32_ref[pl.ds(j, tm, stride=DIFF_HEADS), :] = dv
        vd16_ref[:, sl] = dv.astype(BF16)

    cqn = _rms(proj[:, OFF_CQ:OFF_CQ + Q_LORA], gq_ref[...])
    q = jnp.dot(cqn.astype(BF16), wuq_ref[...], preferred_element_type=F32)
    cq = cq_ref[...]
    sq = sq_ref[...]
    first_q = lane < (MLA_NOPE + MLA_ROPE // 2)
    for h in range(MLA_HEADS):
        sl = slice(h * LANES, (h + 1) * LANES)
        blk = q[:, sl]
        rot = jnp.where(first_q, pltpu.roll(blk, LANES - 16, 1), pltpu.roll(blk, 16, 1))
        qm_ref[:, sl] = ((blk * cq + rot * sq) * scale_m).astype(BF16)

    ckv_ref[...] = _rms(proj[:, OFF_CKV:OFF_CKV + KV_LORA], gkv_ref[...])

    kb = proj[:, OFF_KPE:OFF_KPE + LANES]
    rot = jnp.where(lane < 16, pltpu.roll(kb, LANES - 16, 1), pltpu.roll(kb, 16, 1))
    kpe_ref[...] = (kb * ck_ref[...] + rot * sk_ref[...])[:, :MLA_ROPE]


def _proj(x, w_in, wuq, gq, gkv, tables, tm, n_pat):
    t = x.shape[0]
    row = lambda i: (i, 0)
    const = lambda i: (0, 0)
    pat = lambda i: (i % n_pat, 0)
    tab_spec = pl.BlockSpec((tm, LANES), pat)
    out_w = lambda w, dt: jax.ShapeDtypeStruct((t, w), dt)
    cache_spec = pl.BlockSpec((tm * DIFF_HEADS, LANES), row)
    cache_shape = jax.ShapeDtypeStruct((t * DIFF_HEADS, LANES), F32)
    return pl.pallas_call(
        functools.partial(_proj_kernel, scale_d=LOG2E * DIFF_QK ** -0.5,
                          scale_m=LOG2E * (MLA_NOPE + MLA_ROPE) ** -0.5),
        grid=(t // tm,),
        in_specs=[pl.BlockSpec((tm, x.shape[1]), row),
                  pl.BlockSpec(w_in.shape, const), pl.BlockSpec(wuq.shape, const),
                  pl.BlockSpec(gq.shape, const), pl.BlockSpec(gkv.shape, const)] + [tab_spec] * 6,
        out_specs=[pl.BlockSpec((tm, HEAD_W), row), cache_spec, pl.BlockSpec((tm, HEAD_W), row), cache_spec,
                   pl.BlockSpec((tm, HEAD_W), row), pl.BlockSpec((tm, KV_LORA), row),
                   pl.BlockSpec((tm, MLA_ROPE), row), pl.BlockSpec((tm, HEAD_W), row)],
        out_shape=[out_w(HEAD_W, BF16), cache_shape, out_w(HEAD_W, BF16), cache_shape,
                   out_w(HEAD_W, BF16), out_w(KV_LORA, F32), out_w(MLA_ROPE, F32), out_w(HEAD_W, BF16)],
        compiler_params=_cparams(("parallel",)),
        name="proj",
    )(x, w_in, wuq, gq, gkv, *tables)


def _kvup_kernel(ckv_ref, kpe_ref, wuk_ref, place_ref, wuv_ref, k_ref, v_ref):
    c = ckv_ref[...].astype(BF16)
    k = jnp.dot(c, wuk_ref[...], preferred_element_type=F32)
    k = k + jnp.dot(kpe_ref[...].astype(BF16), place_ref[...], preferred_element_type=F32)
    k_ref[...] = k.astype(BF16)
    v_ref[...] = jnp.dot(c, wuv_ref[...], preferred_element_type=F32).astype(BF16)


def _kvup(ckv, kpe, wuk, place, wuv, tm):
    r = ckv.shape[0]
    row = lambda i: (i, 0)
    const = lambda i: (0, 0)
    return pl.pallas_call(
        _kvup_kernel,
        grid=(r // tm,),
        in_specs=[pl.BlockSpec((tm, KV_LORA), row), pl.BlockSpec((tm, MLA_ROPE), row),
                  pl.BlockSpec(wuk.shape, const), pl.BlockSpec(place.shape, const), pl.BlockSpec(wuv.shape, const)],
        out_specs=[pl.BlockSpec((tm, HEAD_W), row)] * 2,
        out_shape=[jax.ShapeDtypeStruct((r, HEAD_W), BF16)] * 2,
        compiler_params=_cparams(("parallel",)),
        name="kvup",
    )(ckv, kpe, wuk, place, wuv)


def _catcast_kernel(ck_ref, cv_ref, nk_ref, nv_ref, k_ref, v_ref, *, n_past, s_new, tr):
    j = pl.program_id(1)

    @pl.when(j < n_past)
    def _():
        for h in range(DIFF_HEADS):
            sl = slice(h * LANES, (h + 1) * LANES)
            k_ref[:, sl] = ck_ref[pl.ds(h, tr, stride=DIFF_HEADS), :].astype(BF16)
            v_ref[:, sl] = cv_ref[pl.ds(h, tr, stride=DIFF_HEADS), :].astype(BF16)

    @pl.when(j >= n_past)
    def _():
        k_ref[...] = jnp.zeros(k_ref.shape, BF16)
        v_ref[...] = jnp.zeros(v_ref.shape, BF16)
        k_ref[0:s_new, :] = nk_ref[...]
        v_ref[0:s_new, :] = nv_ref[...]


def _catcast(cache_k, cache_v, new_k, new_v, tr):
    b, p = cache_k.shape[:2]
    s = new_k.shape[1]
    assert p % tr == 0 and s <= tr
    n_past = p // tr
    past = pl.BlockSpec((None, tr * DIFF_HEADS, LANES), lambda bi, j: (bi, jnp.minimum(j, n_past - 1), 0))
    new = pl.BlockSpec((None, s, HEAD_W), lambda bi, j: (bi, 0, 0))
    out = pl.BlockSpec((None, tr, HEAD_W), lambda bi, j: (bi, j, 0))
    return pl.pallas_call(
        functools.partial(_catcast_kernel, n_past=n_past, s_new=s, tr=tr),
        grid=(b, n_past + 1),
        in_specs=[past, past, new, new],
        out_specs=[out, out],
        out_shape=[jax.ShapeDtypeStruct((b, p + tr, HEAD_W), BF16)] * 2,
        compiler_params=_cparams(("parallel", "arbitrary")),
        name="catcast",
    )(cache_k.reshape(b, p * DIFF_HEADS, LANES), cache_v.reshape(b, p * DIFF_HEADS, LANES), new_k, new_v)


def _kvup_past_kernel(pc_ref, pp_ref, nc_ref, np_ref, wuk_ref, place_ref, wuv_ref, k_ref, v_ref, c_ref, r_ref,
                      *, n_past, s_new):
    j = pl.program_id(1)

    @pl.when(j < n_past)
    def _():
        c_ref[...] = pc_ref[...]
        r_ref[...] = pp_ref[...]

    @pl.when(j >= n_past)
    def _():
        c_ref[...] = jnp.zeros(c_ref.shape, F32)
        r_ref[...] = jnp.zeros(r_ref.shape, F32)
        c_ref[0:s_new, :] = nc_ref[...]
        r_ref[0:s_new, :] = np_ref[...]

    _kvup_kernel(c_ref, r_ref, wuk_ref, place_ref, wuv_ref, k_ref, v_ref)


def _kvup_past(cache_c, cache_r, new_c, new_r, wuk, place, wuv, tr):
    b, p = cache_c.shape[:2]
    s = new_c.shape[1]
    assert p % tr == 0 and s <= tr
    n_past = p // tr
    clamp = lambda bi, j: (bi, jnp.minimum(j, n_past - 1), 0)
    first = lambda bi, j: (bi, 0, 0)
    const = lambda bi, j: (0, 0)
    out = pl.BlockSpec((None, tr, HEAD_W), lambda bi, j: (bi, j, 0))
    return pl.pallas_call(
        functools.partial(_kvup_past_kernel, n_past=n_past, s_new=s),
        grid=(b, n_past + 1),
        in_specs=[pl.BlockSpec((None, tr, KV_LORA), clamp), pl.BlockSpec((None, tr, MLA_ROPE), clamp),
                  pl.BlockSpec((None, s, KV_LORA), first), pl.BlockSpec((None, s, MLA_ROPE), first),
                  pl.BlockSpec(wuk.shape, const), pl.BlockSpec(place.shape, const), pl.BlockSpec(wuv.shape, const)],
        out_specs=[out, out],
        out_shape=[jax.ShapeDtypeStruct((b, p + tr, HEAD_W), BF16)] * 2,
        scratch_shapes=[pltpu.VMEM((tr, KV_LORA), F32), pltpu.VMEM((tr, MLA_ROPE), F32)],
        compiler_params=_cparams(("parallel", "arbitrary")),
        name="kvup_past",
    )(cache_c, cache_r, new_c, new_r, wuk, place, wuv)


def _flash_kernel(lam_ref, q_ref, k_ref, v_ref, g_ref, o_ref, qs_ref, vx_ref, m_ref, acc_ref,
                  *, n_comp, hps, tq, tk, tkm, sub, sk, q_pos0, out_scale, tiles, static, aligned):
    rows = n_comp * tq
    blocks = [(h, r0) for h in range(hps) for r0 in range(0, rows, sub)]

    @pl.when(pl.program_id(2) == 0)
    def _():
        for h in range(hps):
            vx_ref[:, 2 * h * LANES:(2 * h + 1) * LANES] = v_ref[:, h * LANES:(h + 1) * LANES]
            vx_ref[:, (2 * h + 1) * LANES:(2 * h + 2) * LANES] = jnp.ones((sk, LANES), BF16)

    def absorb(s, start, h, r0):
        rs = slice(h * rows + r0, h * rows + r0 + sub)
        width = s.shape[1]
        vx = vx_ref[pl.ds(start, width), 2 * h * LANES:(2 * h + 2) * LANES]
        m_prev = m_ref[rs, :]
        m_new = jnp.maximum(m_prev, jnp.max(s, axis=-1, keepdims=True))
        alpha = jnp.exp2(m_prev - m_new)
        p = jnp.exp2(s - jnp.concatenate([m_new] * (width // LANES), axis=1))
        pv = jnp.dot(p.astype(BF16), vx, preferred_element_type=F32)
        acc_ref[rs, :] = jnp.concatenate([alpha, alpha], axis=1) * acc_ref[rs, :] + pv
        m_ref[rs, :] = m_new

    def scores(start, width, h, r0):
        rs = slice(h * rows + r0, h * rows + r0 + sub)
        k = k_ref[pl.ds(start, width), h * LANES:(h + 1) * LANES]
        return lax.dot_general(qs_ref[rs, :], k, (((1,), (1,)), ((), ())), preferred_element_type=F32)

    def masked_scores(start, width, h, r0, q0):
        r = (lax.broadcasted_iota(jnp.int32, (sub, width), 0) + r0) & (tq - 1)
        c = lax.broadcasted_iota(jnp.int32, (sub, width), 1)
        ok = ((start + c) >> CHUNK_SHIFT) <= ((q0 + r) >> CHUNK_SHIFT)
        return jnp.where(ok, scores(start, width, h, r0), NEG_BIG)

    def one_tile(qt, q0):
        for h in range(hps):
            q = q_ref[qt * tq:(qt + 1) * tq, h * LANES:(h + 1) * LANES]
            if n_comp == 2:
                lane = lax.broadcasted_iota(jnp.int32, (tq, LANES), 1)
                zero = jnp.zeros_like(q)
                qs_ref[h * rows:h * rows + tq, :] = jnp.where(lane < DIFF_QK, q, zero)
                qs_ref[h * rows + tq:(h + 1) * rows, :] = jnp.where(lane >= DIFF_QK, q, zero)
            else:
                qs_ref[h * rows:(h + 1) * rows, :] = q
        m_ref[...] = jnp.full(m_ref.shape, NEG_BIG, F32)
        acc_ref[...] = jnp.zeros(acc_ref.shape, F32)

        if aligned:
            for j in range(q0 // tk):
                for h, r0 in blocks:
                    absorb(scores(j * tk, tk, h, r0), j * tk, h, r0)
            qc = lax.broadcasted_iota(jnp.int32, (sub, sub), 0) >> CHUNK_SHIFT
            kc = lax.broadcasted_iota(jnp.int32, (sub, sub), 1) >> CHUNK_SHIFT
            for h, r0 in blocks:
                a = r0 % tq
                s = scores(q0, a + sub, h, r0)
                last = jnp.where(kc <= qc, s[:, a:a + sub], NEG_BIG)
                absorb(jnp.concatenate([s[:, 0:a], last], axis=1) if a else last, q0, h, r0)
        elif static:
            lo_vis = min((q0 // CHUNK + 1) * CHUNK, sk)
            hi_vis = min(((q0 + tq - 1) // CHUNK + 1) * CHUNK, sk)
            for j in range(lo_vis // tk):
                for h, r0 in blocks:
                    absorb(scores(j * tk, tk, h, r0), j * tk, h, r0)
            start = lo_vis // tk * tk
            while start < hi_vis:
                width = min(tkm, -(-(hi_vis - start) // LANES) * LANES)
                for h, r0 in blocks:
                    absorb(masked_scores(start, width, h, r0, q0), start, h, r0)
                start += width
        else:
            lo_vis = jnp.minimum(((q0 >> CHUNK_SHIFT) + 1) << CHUNK_SHIFT, sk)
            hi_vis = jnp.minimum((((q0 + tq - 1) >> CHUNK_SHIFT) + 1) << CHUNK_SHIFT, sk)
            n_full = lo_vis // tk

            def full_body(j, carry):
                start = pl.multiple_of(j * tk, tk)
                for h, r0 in blocks:
                    absorb(scores(start, tk, h, r0), start, h, r0)
                return carry

            def masked_body(j, carry):
                start = pl.multiple_of(j * tkm, tkm)
                for h, r0 in blocks:
                    absorb(masked_scores(start, tkm, h, r0, q0), start, h, r0)
                return carry

            lax.fori_loop(0, n_full, full_body, 0)
            lax.fori_loop(n_full * (tk // tkm), (hi_vis + tkm - 1) // tkm, masked_body, 0)

        for h in range(hps):
            hr = slice(h * rows, (h + 1) * rows)
            o = acc_ref[hr, 0:LANES] / acc_ref[hr, LANES:2 * LANES]
            if n_comp == 2:
                o = o[0:tq, :] - lam_ref[0, 0] * o[tq:rows, :]
                o = _rms(o, g_ref[...]) * out_scale
            o_ref[qt * tq:(qt + 1) * tq, h * LANES:(h + 1) * LANES] = o.astype(BF16)

    if static:
        for qt in range(tiles):
            one_tile(qt, q_pos0 + qt * tq)
    else:
        one_tile(0, q_pos0 + pl.program_id(2) * tq)


def _flash(q, k, v, lam, g, *, n_comp, hps, tq, tk, tkm, q_pos0, out_scale):
    b, sq, _ = q.shape
    sk = k.shape[1]
    rows = n_comp * tq
    sub = min(rows, FLASH_SUB)
    assert sq % tq == 0 and sk % tk == 0 and tk % tkm == 0 and tq & (tq - 1) == 0 and rows % sub == 0
    assert DIFF_HEADS % hps == 0
    aligned = q_pos0 % tq == 0 and tq == tk and q_pos0 + sq <= sk and sub % CHUNK == 0 and sub % LANES == 0
    static = aligned or sq == tq
    tiles = sq // tq if static else 1
    hw = hps * LANES
    return pl.pallas_call(
        functools.partial(_flash_kernel, n_comp=n_comp, hps=hps, tq=tq, tk=tk, tkm=tkm, sub=sub, sk=sk,
                          q_pos0=q_pos0, out_scale=out_scale, tiles=tiles, static=static, aligned=aligned),
        grid=(b, DIFF_HEADS // hps, sq // (tq * tiles)),
        in_specs=[pl.BlockSpec(memory_space=pltpu.SMEM),
                  pl.BlockSpec((None, tq * tiles, hw), lambda bi, h, i: (bi, i, h)),
                  pl.BlockSpec((None, sk, hw), lambda bi, h, i: (bi, 0, h)),
                  pl.BlockSpec((None, sk, hw), lambda bi, h, i: (bi, 0, h)),
                  pl.BlockSpec((1, LANES), lambda bi, h, i: (0, 0))],
        out_specs=pl.BlockSpec((None, tq * tiles, hw), lambda bi, h, i: (bi, i, h)),
        out_shape=jax.ShapeDtypeStruct((b, sq, HEAD_W), BF16),
        scratch_shapes=[pltpu.VMEM((hps * rows, LANES), BF16), pltpu.VMEM((sk, 2 * hw), BF16),
                        pltpu.VMEM((hps * rows, LANES), F32), pltpu.VMEM((hps * rows, 2 * LANES), F32)],
        compiler_params=_cparams(("parallel", "parallel", "arbitrary")),
        name="flash_diff" if n_comp == 2 else "flash_mla",
    )(lam, q, k, v, g)


def _sublane_max(x):
    return jnp.max(x, axis=0, keepdims=True)


def _sublane_min(x):
    return jnp.min(x, axis=0, keepdims=True)


def _route_t(scores, biased):
    tm = scores[0].shape[1]
    sub = lax.broadcasted_iota(jnp.int32, (GROUP_SIZE, tm), 0)
    neg_inf = jnp.float32(-jnp.inf)
    gs = jnp.zeros((N_GROUPS, tm), F32)
    for g in range(N_GROUPS):
        bg = biased[g]
        m1 = _sublane_max(bg)
        i1 = _sublane_min(jnp.where(bg == m1, sub, GROUP_SIZE))
        m2 = _sublane_max(jnp.where(sub == i1, neg_inf, bg))
        gs = jnp.where(sub == g, m1 + m2, gs)
    keep = jnp.zeros((N_GROUPS, tm), jnp.bool_)
    cur = gs
    for _ in range(TOPK_GROUPS):
        mx = _sublane_max(cur)
        fi = _sublane_min(jnp.where(cur == mx, sub, N_GROUPS))
        hit = sub == fi
        keep = jnp.logical_or(keep, hit)
        cur = jnp.where(hit, neg_inf, cur)
    keep_f = jnp.where(keep, 1.0, 0.0)
    cand = []
    for g in range(N_GROUPS):
        kg = _sublane_max(jnp.where(sub == g, keep_f, 0.0)) > 0.5
        cand.append(jnp.where(kg, biased[g], neg_inf))
    chosen = [jnp.zeros((GROUP_SIZE, tm), jnp.bool_) for _ in range(N_GROUPS)]
    picks = []
    for _ in range(TOP_K):
        mx = cand[0]
        for g in range(1, N_GROUPS):
            mx = jnp.maximum(mx, cand[g])
        mx = _sublane_max(mx)
        fi = jnp.where(cand[0] == mx, sub, N_EXPERTS)
        for g in range(1, N_GROUPS):
            fi = jnp.minimum(fi, jnp.where(cand[g] == mx, sub + g * GROUP_SIZE, N_EXPERTS))
        fi = _sublane_min(fi)
        picks.append(fi)
        for g in range(N_GROUPS):
            hit = (sub + g * GROUP_SIZE) == fi
            chosen[g] = jnp.logical_or(chosen[g], hit)
            cand[g] = jnp.where(hit, neg_inf, cand[g])
    w = [jnp.where(chosen[g], scores[g], 0.0) for g in range(N_GROUPS)]
    tot = w[0]
    for g in range(1, N_GROUPS):
        tot = tot + w[g]
    tot = jnp.sum(tot, axis=0, keepdims=True)
    return [wg / tot * ROUTED_SCALE for wg in w], chosen, picks


def _mix_kernel(od_ref, om_ref, x_ref, wo_ref, g_ref, b_ref, wr_ref, wrl_ref, br_ref, before_ref, below_ref, eye_ref,
                x1_ref, x1b_ref, sk_ref, cnt_ref, *, alpha):
    mix = jnp.dot(od_ref[...], wo_ref[0:HEAD_W, :], preferred_element_type=F32)
    mix = mix + jnp.dot(om_ref[...], wo_ref[HEAD_W:2 * HEAD_W, :], preferred_element_type=F32)
    x1 = _layer_norm(alpha * x_ref[...] + mix, g_ref[...], b_ref[...])
    x_hi = x1.astype(BF16)
    x1_ref[...] = x1
    x1b_ref[...] = x_hi
    x_lo = (x1 - x_hi.astype(F32)).astype(BF16)
    nt = (((1,), (1,)), ((), ()))
    logits = (lax.dot_general(wr_ref[...], x_hi, nt, preferred_element_type=F32)
              + lax.dot_general(wr_ref[...], x_lo, nt, preferred_element_type=F32)
              + lax.dot_general(wrl_ref[...], x_hi, nt, preferred_element_type=F32))
    sc = 1.0 / (1.0 + jnp.exp(-logits))
    bi = sc + br_ref[...]
    scores = [sc[g * GROUP_SIZE:(g + 1) * GROUP_SIZE, :] for g in range(N_GROUPS)]
    biased = [bi[g * GROUP_SIZE:(g + 1) * GROUP_SIZE, :] for g in range(N_GROUPS)]
    gates, chosen, picks = _route_t(scores, biased)
    tm = x1.shape[0]
    ch = jnp.concatenate([jnp.where(c, 1.0, 0.0) for c in chosen], axis=0)
    gate_all = jnp.concatenate(gates, axis=0)
    sub = lax.broadcasted_iota(jnp.int32, (N_EXPERTS, MOE_BLOCK), 0)
    for blk in range(tm // MOE_BLOCK):
        ls = slice(blk * MOE_BLOCK, (blk + 1) * MOE_BLOCK)
        chb = ch[:, ls]
        rank = jnp.dot(chb.astype(BF16), before_ref[...], preferred_element_type=F32)
        cnt = jnp.sum(chb, axis=1, keepdims=True)
        seg = jnp.floor((cnt + (SEG_ALIGN - 1.0)) * (1.0 / SEG_ALIGN)) * SEG_ALIGN
        seg_b = jnp.broadcast_to(seg, (N_EXPERTS, LANES))
        off = jnp.dot(below_ref[...], seg_b, precision=lax.Precision.HIGHEST, preferred_element_type=F32)
        slot = jnp.concatenate([off] * (MOE_BLOCK // LANES), axis=1) + rank
        cnt_ref[blk] = jnp.sum(seg_b * eye_ref[...], axis=0, keepdims=True)
        gb = gate_all[:, ls]
        for k in range(TOP_K):
            hit = sub == picks[k][:, ls]
            sk_ref[k:k + 1, ls] = jnp.sum(jnp.where(hit, slot, 0.0), axis=0, keepdims=True)
            sk_ref[TOP_K + k:TOP_K + k + 1, ls] = jnp.sum(jnp.where(hit, gb, 0.0), axis=0, keepdims=True)


def _mix(od, om, x, wo, g, b, wr_hi, wr_lo, br, tm, alpha):
    t = x.shape[0]
    d = x.shape[1]
    nblk = tm // MOE_BLOCK
    row = lambda i: (i, 0)
    const = lambda i: (0, 0)
    idx = jnp.arange(MOE_BLOCK)
    before = (idx[:, None] < idx[None, :]).astype(BF16)
    ide = jnp.arange(N_EXPERTS)
    below = (ide[None, :] < ide[:, None]).astype(F32)
    eye = (ide[:, None] == jnp.arange(LANES)[None, :]).astype(F32)
    return pl.pallas_call(
        functools.partial(_mix_kernel, alpha=alpha),
        grid=(t // tm,),
        in_specs=[pl.BlockSpec((tm, HEAD_W), row), pl.BlockSpec((tm, HEAD_W), row), pl.BlockSpec((tm, d), row),
                  pl.BlockSpec(wo.shape, const), pl.BlockSpec(g.shape, const), pl.BlockSpec(b.shape, const),
                  pl.BlockSpec(wr_hi.shape, const), pl.BlockSpec(wr_lo.shape, const), pl.BlockSpec(br.shape, const),
                  pl.BlockSpec(before.shape, const), pl.BlockSpec(below.shape, const), pl.BlockSpec(eye.shape, const)],
        out_specs=[pl.BlockSpec((tm, d), row), pl.BlockSpec((tm, d), row),
                   pl.BlockSpec((2 * TOP_K, tm), lambda i: (0, i)),
                   pl.BlockSpec((nblk, 1, LANES), lambda i: (i, 0, 0))],
        out_shape=[jax.ShapeDtypeStruct((t, d), F32), jax.ShapeDtypeStruct((t, d), BF16),
                   jax.ShapeDtypeStruct((2 * TOP_K, t), F32),
                   jax.ShapeDtypeStruct((t // MOE_BLOCK, 1, LANES), F32)],
        compiler_params=_cparams(("parallel",)),
        name="mix",
    )(od, om, x, wo, g, b, wr_hi, wr_lo, br, before, below, eye)


def _moe_plan(cnt, n_tiles, tile, trash_row):
    seg = cnt.astype(jnp.int32)
    nb = seg.shape[0]
    before_blocks = jnp.cumsum(seg, axis=0) - seg
    length = jnp.sum(seg, axis=0)
    padded = -(-length // tile) * tile
    ends = jnp.cumsum(padded)
    start = ends - padded
    dst = start[None, :] + before_blocks
    off = (jnp.cumsum(seg, axis=1) - seg) // SEG_ALIGN
    tot = jnp.sum(seg, axis=1) // SEG_ALIGN
    chunk = jnp.arange(BLOCK_CHUNKS, dtype=jnp.int32)
    base = dst - SEG_ALIGN * off
    step = jnp.concatenate([base[:, :1], base[:, 1:] - base[:, :-1]], axis=1)
    started = (off[:, None, :] <= chunk[None, :, None]).astype(jnp.int32)
    row = jnp.sum(started * step[:, None, :], axis=2) + SEG_ALIGN * chunk[None, :]
    valid = chunk[None, :] < tot[:, None]
    trash = trash_row + ((jnp.arange(nb, dtype=jnp.int32) % 3) * BLOCK_ROWS)[:, None] + SEG_ALIGN * chunk[None, :]
    first = trash_row + 2 * BLOCK_ROWS + SEG_ALIGN * chunk[None, :]
    put = jnp.concatenate([first, jnp.where(valid, row, trash)], axis=0).reshape(-1)
    get = jnp.concatenate([jnp.where(valid, row, 0), jnp.zeros((2, BLOCK_CHUNKS), jnp.int32)], axis=0).reshape(-1)
    n_used = ends[-1] // tile
    tile_start = jnp.arange(n_tiles, dtype=jnp.int32) * tile
    tile_e = jnp.sum((ends[None, :] <= tile_start[:, None]).astype(jnp.int32), axis=1)
    last_e = tile_e[jnp.maximum(n_used - 1, 0)]
    tile_e = jnp.where(jnp.arange(n_tiles) < n_used, tile_e, last_e)
    tail_start = start + length
    tail_nch = (padded - length) // SEG_ALIGN
    return dict(put=put, get=get, n_used=n_used.reshape(1), tile_e=tile_e, tail_start=tail_start,
                tail_nch=tail_nch, tail_tot=jnp.sum(tail_nch).reshape(1))


def _drain(copy, n):
    def body(i, c):
        copy.wait()
        return c
    lax.fori_loop(0, n, body, 0)


def _order_rows(sk, j0, vals):
    rows = lax.broadcasted_iota(jnp.int32, (ORD_SUB, MOE_BLOCK), 0).astype(F32).astype(BF16)
    out = jnp.zeros((ORD_SUB, MOE_BLOCK), BF16)
    for k in range(TOP_K):
        rel = (sk[k:k + 1, :] - j0).astype(BF16)
        out = jnp.where(rows == rel, vals[k], out)
    return out


def _dispatch_kernel(put_ref, tstart_ref, tnch_ref, ttot_ref, x_ref, sk_ref, xs_ref, buf_ref, zero_ref, sem, tail_sem):
    b = pl.program_id(0)
    last = pl.num_programs(0) - 1
    slot = lax.rem(b, 3)
    send = lax.rem(b + 2, 3)

    def whole(s):
        return pltpu.make_async_copy(buf_ref.at[s], xs_ref.at[pl.ds(0, BLOCK_ROWS), :], sem.at[s])

    @pl.when(b == 0)
    def _():
        buf_ref[...] = jnp.zeros(buf_ref.shape, BF16)

    @pl.when(b >= 2)
    def _():
        whole(slot).wait()

    x = x_ref[...]
    sk = sk_ref[0:TOP_K, :]
    ones = [jnp.ones((1, MOE_BLOCK), BF16)] * TOP_K
    base = b * BLOCK_CHUNKS
    for sub in range(BLOCK_ROWS // DISP_SUB):
        j0 = sub * DISP_SUB
        sel = jnp.concatenate([_order_rows(sk, float(j0 + i * ORD_SUB), ones) for i in range(DISP_SUB // ORD_SUB)],
                              axis=0)
        buf_ref[slot, j0:j0 + DISP_SUB, :] = jnp.dot(sel, x, preferred_element_type=F32).astype(BF16)
        for c in range(sub * (DISP_SUB // SEG_ALIGN), (sub + 1) * (DISP_SUB // SEG_ALIGN)):
            pltpu.make_async_copy(
                buf_ref.at[send, c * SEG_ALIGN:(c + 1) * SEG_ALIGN, :],
                xs_ref.at[pl.ds(pl.multiple_of(put_ref[base + c], SEG_ALIGN), SEG_ALIGN), :], sem.at[send]).start()

    @pl.when(b == last)
    def _():
        whole(send).wait()

        @pl.when(b >= 1)
        def _():
            whole(lax.rem(b + 1, 3)).wait()

        zero_ref[...] = jnp.zeros(zero_ref.shape, BF16)

        def tail_copy(dst_row):
            return pltpu.make_async_copy(zero_ref, xs_ref.at[pl.ds(pl.multiple_of(dst_row, SEG_ALIGN), SEG_ALIGN), :],
                                         tail_sem.at[0])

        def tail_expert(e, carry):
            def tail_chunk(c, carry2):
                tail_copy(tstart_ref[e] + c * SEG_ALIGN).start()
                return carry2
            lax.fori_loop(0, tnch_ref[e], tail_chunk, 0)
            return carry

        lax.fori_loop(0, N_EXPERTS, tail_expert, 0)
        _drain(tail_copy(0), ttot_ref[0])


def _dispatch(plan, x1b, sk, n_rows):
    t, d = x1b.shape
    nb = t // MOE_BLOCK
    grid_spec = pltpu.PrefetchScalarGridSpec(
        num_scalar_prefetch=4,
        grid=(nb + 1,),
        in_specs=[pl.BlockSpec((MOE_BLOCK, d), lambda b, *_: (jnp.minimum(b, nb - 1), 0)),
                  pl.BlockSpec((2 * TOP_K, MOE_BLOCK), lambda b, *_: (0, jnp.minimum(b, nb - 1)))],
        out_specs=pl.BlockSpec(memory_space=pl.ANY),
        scratch_shapes=[pltpu.VMEM((3, BLOCK_ROWS, d), BF16), pltpu.VMEM((SEG_ALIGN, d), BF16),
                        pltpu.SemaphoreType.DMA((3,)), pltpu.SemaphoreType.DMA((1,))],
    )
    return pl.pallas_call(
        _dispatch_kernel,
        grid_spec=grid_spec,
        out_shape=jax.ShapeDtypeStruct((n_rows, d), BF16),
        compiler_params=_cparams(("arbitrary",)),
        name="dispatch",
    )(plan["put"], plan["tail_start"], plan["tail_nch"], plan["tail_tot"], x1b, sk)


def _ffn_kernel(te_ref, nu_ref, xs_ref, wg_ref, wu_ref, wd_ref, y_ref):
    @pl.when(pl.program_id(0) < nu_ref[0])
    def _():
        x = xs_ref[...]
        hg = jnp.dot(x, wg_ref[...], preferred_element_type=F32)
        hu = jnp.dot(x, wu_ref[...], preferred_element_type=F32)
        h = hg * (1.0 / (1.0 + jnp.exp(-hg))) * hu
        y_ref[...] = jnp.dot(h.astype(BF16), wd_ref[...], preferred_element_type=F32).astype(BF16)


def _ffn(plan, xs, wg, wu, wd, n_tiles, tile):
    d = xs.shape[1]
    used = lambda i, te, nu: (jnp.minimum(i, nu[0] - 1), 0)
    wsel = lambda i, te, nu: (te[i], 0, 0)
    grid_spec = pltpu.PrefetchScalarGridSpec(
        num_scalar_prefetch=2,
        grid=(n_tiles,),
        in_specs=[pl.BlockSpec((tile, d), used),
                  pl.BlockSpec((None, d, EXPERT_DIM), wsel), pl.BlockSpec((None, d, EXPERT_DIM), wsel),
                  pl.BlockSpec((None, EXPERT_DIM, d), wsel)],
        out_specs=pl.BlockSpec((tile, d), used),
    )
    return pl.pallas_call(
        _ffn_kernel,
        grid_spec=grid_spec,
        out_shape=jax.ShapeDtypeStruct((n_tiles * tile, d), BF16),
        compiler_params=_cparams(("arbitrary",)),
        name="ffn",
    )(plan["tile_e"], plan["n_used"], xs, wg, wu, wd)


def _combine_kernel(get_ref, y_ref, sk_ref, x1_ref, x1b_ref, wsg_ref, wsu_ref, wsd_ref,
                    g_ref, b_ref, o_ref, buf_ref, sem, *, alpha):
    b = pl.program_id(0)
    slot = lax.rem(b, 3)
    nxt = lax.rem(b + 2, 3)

    def whole(s):
        return pltpu.make_async_copy(y_ref.at[pl.ds(0, BLOCK_ROWS), :], buf_ref.at[s], sem.at[s])

    @pl.when(b == 0)
    def _():
        def first(c, carry):
            s = c // BLOCK_CHUNKS
            pltpu.make_async_copy(
                y_ref.at[pl.ds(pl.multiple_of(get_ref[c], SEG_ALIGN), SEG_ALIGN), :],
                buf_ref.at[s, pl.ds(pl.multiple_of((c - s * BLOCK_CHUNKS) * SEG_ALIGN, SEG_ALIGN), SEG_ALIGN), :],
                sem.at[s]).start()
            return carry
        lax.fori_loop(0, 2 * BLOCK_CHUNKS, first, 0)

    whole(slot).wait()

    sk = sk_ref[0:TOP_K, :]
    gates = [sk_ref[TOP_K + k:TOP_K + k + 1, :].astype(BF16) for k in range(TOP_K)]
    base = (b + 2) * BLOCK_CHUNKS
    acc = None
    for sub in range(BLOCK_ROWS // DISP_SUB):
        j0 = sub * DISP_SUB
        w = jnp.concatenate([_order_rows(sk, float(j0 + i * ORD_SUB), gates) for i in range(DISP_SUB // ORD_SUB)],
                            axis=0)
        part = lax.dot_general(w, buf_ref[slot, j0:j0 + DISP_SUB, :], (((0,), (0,)), ((), ())),
                               preferred_element_type=F32)
        acc = part if acc is None else acc + part
        for c in range(sub * (DISP_SUB // SEG_ALIGN), (sub + 1) * (DISP_SUB // SEG_ALIGN)):
            pltpu.async_copy(
                y_ref.at[pl.ds(pl.multiple_of(get_ref[base + c], SEG_ALIGN), SEG_ALIGN), :],
                buf_ref.at[nxt, c * SEG_ALIGN:(c + 1) * SEG_ALIGN, :], sem.at[nxt], priority=c % 2)

    xb = x1b_ref[...]
    hg = jnp.dot(xb, wsg_ref[...], preferred_element_type=F32)
    hu = jnp.dot(xb, wsu_ref[...], preferred_element_type=F32)
    h = hg * (1.0 / (1.0 + jnp.exp(-hg))) * hu
    acc = acc + jnp.dot(h.astype(BF16), wsd_ref[...], preferred_element_type=F32)
    o_ref[...] = _layer_norm(alpha * x1_ref[...] + acc, g_ref[...], b_ref[...])

    @pl.when(b == pl.num_programs(0) - 1)
    def _():
        whole(lax.rem(b + 1, 3)).wait()
        whole(nxt).wait()


def _combine(plan, y, sk, x1, x1b, wsg, wsu, wsd, g, b, alpha):
    t, d = x1.shape
    nb = t // MOE_BLOCK
    row = lambda i, *_: (i, 0)
    const = lambda i, *_: (0, 0)
    grid_spec = pltpu.PrefetchScalarGridSpec(
        num_scalar_prefetch=1,
        grid=(nb,),
        in_specs=[pl.BlockSpec(memory_space=pl.ANY), pl.BlockSpec((2 * TOP_K, MOE_BLOCK), lambda i, *_: (0, i)),
                  pl.BlockSpec((MOE_BLOCK, d), row), pl.BlockSpec((MOE_BLOCK, d), row),
                  pl.BlockSpec(wsg.shape, const), pl.BlockSpec(wsu.shape, const), pl.BlockSpec(wsd.shape, const),
                  pl.BlockSpec(g.shape, const), pl.BlockSpec(b.shape, const)],
        out_specs=pl.BlockSpec((MOE_BLOCK, d), row),
        scratch_shapes=[pltpu.VMEM((3, BLOCK_ROWS, d), BF16), pltpu.SemaphoreType.DMA((3,))],
    )
    return pl.pallas_call(
        functools.partial(_combine_kernel, alpha=alpha),
        grid_spec=grid_spec,
        out_shape=jax.ShapeDtypeStruct((t, d), F32),
        compiler_params=_cparams(("arbitrary",)),
        name="combine",
    )(plan["get"], y, sk, x1, x1b, wsg, wsu, wsd, g, b)


def _moe(x1, x1b, sk, cnt, wg, wu, wd, wsg, wsu, wsd, g, b, alpha):
    t = x1.shape[0]
    nb = t // MOE_BLOCK
    tile = FFN_TILE if (TOP_K * t) // N_EXPERTS >= FFN_TILE else FFN_TILE_SMALL
    n_tiles = -(-(nb * BLOCK_ROWS) // tile) + N_EXPERTS
    n_rows = n_tiles * tile + 3 * BLOCK_ROWS
    plan = _moe_plan(cnt[:, 0, :N_EXPERTS], n_tiles, tile, n_tiles * tile)
    xs = _dispatch(plan, x1b, sk, n_rows)
    y = _ffn(plan, xs, wg, wu, wd, n_tiles, tile)
    return _combine(plan, y, sk, x1, x1b, wsg, wsu, wsd, g, b, alpha)


def _rope_cs(pos, dim):
    inv = ROPE_THETA ** (-jnp.arange(0, dim, 2, dtype=F32) / dim)
    ang = pos.astype(F32)[:, None] * inv[None, :]
    return jnp.cos(ang), jnp.sin(ang)


def _tables(pos, reps):
    n = pos.shape[0]
    c32, s32 = _rope_cs(pos, DIFF_QK)
    c16, s16 = _rope_cs(pos, MLA_ROPE)
    one = lambda w: jnp.ones((n, w), F32)
    zero = lambda w: jnp.zeros((n, w), F32)
    cd = jnp.concatenate([c32] * 4, axis=1)
    sd = jnp.concatenate([-s32, s32] * 2, axis=1)
    cq = jnp.concatenate([one(MLA_NOPE), c16, c16, one(32)], axis=1)
    sq = jnp.concatenate([zero(MLA_NOPE), -s16, s16, zero(32)], axis=1)
    ck = jnp.concatenate([c16, c16, zero(96)], axis=1)
    sk = jnp.concatenate([-s16, s16, zero(96)], axis=1)
    return tuple(jnp.tile(a, (reps, 1)) for a in (cd, sd, cq, sq, ck, sk))


def kernel(x_prompt, x_sample, cache_diff_k, cache_diff_v, cache_mla_ckv, cache_mla_kpe, w_in, diff_lambda, diff_subln_g, mla_q_norm_g, mla_w_uq, mla_kv_norm_g, mla_w_ukv, w_out, ln1_g, ln1_b, w_router, b_router, w_exp_gate, w_exp_up, w_exp_down, w_sh_gate, w_sh_up, w_sh_down, ln2_g, ln2_b):
    depth = w_in.shape[0]
    assert depth == 1
    d_model = x_prompt.shape[-1]
    alpha = (2.0 * depth) ** 0.25
    past_len = cache_diff_k.shape[2]
    layer = 0
    lambda_init = 0.8 - 0.6 * math.exp(-0.3 * layer)

    w_in_b = jnp.pad(w_in[layer], ((0, 0), (0, IN_PAD - IN_WIDTH))).astype(BF16)
    wuq = jnp.pad(mla_w_uq[layer], ((0, 0), (0, 0), (0, LANES - MLA_NOPE - MLA_ROPE)))
    wuq = wuq.reshape(Q_LORA, HEAD_W).astype(BF16)
    wukv = mla_w_ukv[layer]
    wuk = jnp.pad(wukv[:, :, :MLA_NOPE], ((0, 0), (0, 0), (0, LANES - MLA_NOPE))).reshape(KV_LORA, HEAD_W).astype(BF16)
    wuv = wukv[:, :, MLA_NOPE:].reshape(KV_LORA, HEAD_W).astype(BF16)
    place = jnp.pad(jnp.eye(MLA_ROPE, dtype=F32), ((0, 0), (MLA_NOPE, LANES - MLA_NOPE - MLA_ROPE)))
    place = jnp.tile(place, (1, MLA_HEADS)).astype(BF16)
    gq = mla_q_norm_g[layer].reshape(1, Q_LORA)
    gkv = mla_kv_norm_g[layer].reshape(1, KV_LORA)
    gsub = diff_subln_g[layer].reshape(1, LANES)
    wo = w_out[layer].astype(BF16)
    g1, b1 = ln1_g[layer].reshape(1, d_model), ln1_b[layer].reshape(1, d_model)
    g2, b2 = ln2_g[layer].reshape(1, d_model), ln2_b[layer].reshape(1, d_model)
    wr_t = w_router[layer].T
    wr_hi = wr_t.astype(BF16)
    wr_lo = (wr_t - wr_hi.astype(F32)).astype(BF16)
    br = b_router[layer].reshape(N_EXPERTS, 1)
    wg, wu, wd = (w[layer].astype(BF16) for w in (w_exp_gate, w_exp_up, w_exp_down))
    wsg, wsu, wsd = (w[layer].astype(BF16) for w in (w_sh_gate, w_sh_up, w_sh_down))
    lp = diff_lambda[layer].astype(F32)
    lam = jnp.exp(jnp.sum(lp[0] * lp[1])) - jnp.exp(jnp.sum(lp[2] * lp[3])) + lambda_init
    lam = lam.reshape(1, 1)

    def group(x, pos, past, tm, tq, tk, tkm, hps):
        b, s, _ = x.shape
        t = b * s
        tm = min(tm, t)
        xf = x.reshape(t, d_model)
        reps = max(1, tm // s)
        tables = _tables(pos, reps)
        n_pat = (s * reps) // tm
        qd, kd32, kd16, vd32, vd16, ckv, kpe, qm = _proj(xf, w_in_b, wuq, gq, gkv, tables, tm, n_pat)
        if past is None:
            k_d, v_d = kd16.reshape(b, s, HEAD_W), vd16.reshape(b, s, HEAD_W)
            k_m, v_m = _kvup(ckv, kpe, wuk, place, wuv, min(1024, t))
            k_m, v_m = k_m.reshape(b, s, HEAD_W), v_m.reshape(b, s, HEAD_W)
            q_pos0 = 0
        else:
            pk, pv, pc, pp = past
            k_d, v_d = _catcast(pk, pv, kd16.reshape(b, s, HEAD_W), vd16.reshape(b, s, HEAD_W), tk)
            k_m, v_m = _kvup_past(pc, pp, ckv.reshape(b, s, KV_LORA), kpe.reshape(b, s, MLA_ROPE),
                                  wuk, place, wuv, tk)
            q_pos0 = past_len
        o_d = _flash(qd.reshape(b, s, HEAD_W), k_d, v_d, lam, gsub, n_comp=2, hps=hps, tq=tq, tk=tk, tkm=tkm,
                     q_pos0=q_pos0, out_scale=1.0 - lambda_init)
        o_m = _flash(qm.reshape(b, s, HEAD_W), k_m, v_m, lam, gsub,
                     n_comp=1, hps=hps, tq=tq, tk=tk, tkm=tkm, q_pos0=q_pos0, out_scale=1.0)
        x1, x1b, slots, cnt = _mix(o_d.reshape(t, HEAD_W), o_m.reshape(t, HEAD_W), xf, wo, g1, b1, wr_hi, wr_lo, br,
                                   tm, alpha)
        y = _moe(x1, x1b, slots, cnt, wg, wu, wd, wsg, wsu, wsd, g2, b2, alpha)
        rows = (kd32.reshape(1, b, s, DIFF_HEADS, LANES), vd32.reshape(1, b, s, DIFF_HEADS, LANES),
                ckv.reshape(1, b, s, KV_LORA), kpe.reshape(1, b, s, MLA_ROPE))
        return y.reshape(b, s, d_model), rows

    s_p = x_prompt.shape[1]
    s_s = x_sample.shape[1]
    pos_p = jnp.arange(s_p, dtype=jnp.int32)
    pos_s = past_len + jnp.arange(s_s, dtype=jnp.int32)
    y_p, r_p = group(x_prompt, pos_p, None, 512, 256, 256, 256, DIFF_HEADS)
    past = (cache_diff_k[layer], cache_diff_v[layer], cache_mla_ckv[layer], cache_mla_kpe[layer])
    y_s, r_s = group(x_sample, pos_s, past, 512, s_s, 1024, 256, DIFF_HEADS)
    return (y_p, y_s) + r_p + r_s
```
